```python
import math
import jax, jax.numpy as jnp
from jax import lax
import numpy as np

D_MODEL = 2048
BATCH = 2
SEQ = 4096
DEPTH = 2

CHUNK = 64
N_META = 16
Q_BLOCK = 128
N_MIXERS = 2
RMS_EPS = 1e-6

DA_HEADS = 8
DA_HEAD_DIM = D_MODEL // (2 * DA_HEADS)
DA_V_DIM = 2 * DA_HEAD_DIM
ROPE_THETA = 500000.0
ROPE_DIM = DA_HEAD_DIM // 4

POOL_WINDOWS = (2, 4, 8, 16)
POOL_GROUPS = len(POOL_WINDOWS)
POOL_GROUP_DIM = D_MODEL // POOL_GROUPS

N_ATTN_LAYERS = (DEPTH + 1) // 2
N_POOL_LAYERS = DEPTH // 2

kernel_name = "hybrid_diffattn_pool_streaming_trunk"


def rms_norm(x, g):
    xf = x.astype(jnp.float32)
    y = xf * lax.rsqrt(jnp.mean(xf * xf, axis=-1, keepdims=True) + RMS_EPS)
    return (y * g.astype(jnp.float32)).astype(x.dtype)


def chunk_ids(length):
    p = jnp.arange(length)
    return jnp.where(p < N_META, 0, (p - N_META) // CHUNK + 1)


def rope_tables(length):
    pos = jnp.arange(length, dtype=jnp.float32)
    inv = ROPE_THETA ** (-jnp.arange(0, ROPE_DIM, 2, dtype=jnp.float32) / ROPE_DIM)
    ang = pos[:, None] * inv[None, :]
    return jnp.cos(ang), jnp.sin(ang)


def apply_partial_rope(t, cos, sin):
    half = ROPE_DIM // 2
    c = cos.astype(t.dtype)
    s = sin.astype(t.dtype)
    r1 = t[..., :half]
    r2 = t[..., half:ROPE_DIM]
    return jnp.concatenate([r1 * c - r2 * s, r2 * c + r1 * s, t[..., ROPE_DIM:]], axis=-1)


def diff_attend(q1, q2, k1, k2, v, qc, kc, lam):
    scale = DA_HEAD_DIM ** -0.5
    mask = kc[None, :] <= qc[:, None]
    neg = jnp.finfo(jnp.float32).min
    s1 = jnp.einsum('bhqd,bhkd->bhqk', q1, k1).astype(jnp.float32) * scale
    s2 = jnp.einsum('bhqd,bhkd->bhqk', q2, k2).astype(jnp.float32) * scale
    p1 = jax.nn.softmax(jnp.where(mask, s1, neg), axis=-1)
    p2 = jax.nn.softmax(jnp.where(mask, s2, neg), axis=-1)
    a = p1 - lam * p2
    return jnp.einsum('bhqk,bhkd->bhqd', a.astype(v.dtype), v)


def diff_attention_mixer(h, w_in, w_out, lq1, lk1, lq2, lk2, subln_g, layer_idx, cos, sin):
    B, L, D = h.shape
    proj = h @ w_in
    q, k, v, g = jnp.split(proj, 4, axis=-1)
    q = q.reshape(B, L, DA_HEADS, 2, DA_HEAD_DIM).transpose(0, 2, 3, 1, 4)
    k = k.reshape(B, L, DA_HEADS, 2, DA_HEAD_DIM).transpose(0, 2, 3, 1, 4)
    v = v.reshape(B, L, DA_HEADS, DA_V_DIM).transpose(0, 2, 1, 3)
    q1 = apply_partial_rope(q[:, :, 0], cos, sin)
    q2 = apply_partial_rope(q[:, :, 1], cos, sin)
    k1 = apply_partial_rope(k[:, :, 0], cos, sin)
    k2 = apply_partial_rope(k[:, :, 1], cos, sin)

    lam_init = 0.8 - 0.6 * math.exp(-0.3 * layer_idx)
    lam = (jnp.exp(jnp.sum(lq1.astype(jnp.float32) * lk1.astype(jnp.float32)))
           - jnp.exp(jnp.sum(lq2.astype(jnp.float32) * lk2.astype(jnp.float32)))
           + lam_init)

    cid = chunk_ids(L)
    outs = [diff_attend(q1[:, :, :N_META], q2[:, :, :N_META], k1[:, :, :N_META],
                        k2[:, :, :N_META], v[:, :, :N_META], cid[:N_META], cid[:N_META], lam)]
    n_blocks = (L - N_META) // Q_BLOCK
    for b in range(n_blocks):
        q0 = N_META + b * Q_BLOCK
        q1e = q0 + Q_BLOCK
        outs.append(diff_attend(q1[:, :, q0:q1e], q2[:, :, q0:q1e], k1[:, :, :q1e],
                                k2[:, :, :q1e], v[:, :, :q1e], cid[q0:q1e], cid[:q1e], lam))
    o = jnp.concatenate(outs, axis=2)
    o = rms_norm(o, subln_g) * (1.0 - lam_init)
    o = o.transpose(0, 2, 1, 3).reshape(B, L, D)
    return (o * jax.nn.silu(g)) @ w_out


def pool_mixer(h, w_in, w_group, scale, w_out):
    B, L, D = h.shape
    proj = h @ w_in
    u, g = jnp.split(proj, 2, axis=-1)
    ug = u.reshape(B, L, POOL_GROUPS, POOL_GROUP_DIM)
    cs = jnp.cumsum(ug.astype(jnp.float32), axis=1)
    csp = jnp.concatenate([jnp.zeros((B, 1, POOL_GROUPS, POOL_GROUP_DIM), jnp.float32), cs], axis=1)
    t = jnp.arange(L)
    pooled = []
    for gi, w in enumerate(POOL_WINDOWS):
        start = jnp.maximum(t + 1 - w, 0)
        cnt = jnp.minimum(t + 1, w).astype(jnp.float32)
        s = csp[:, 1:, gi] - csp[:, start, gi]
        pooled.append(s / cnt[None, :, None])
    pooled = jnp.stack(pooled, axis=2).astype(u.dtype)
    mixed = pooled - ug
    mixed = jnp.einsum('blgc,gcd->blgd', mixed, w_group).reshape(B, L, D) * scale
    return (mixed * jax.nn.silu(g)) @ w_out


def setup_inputs(seed: int = 0) -> dict:
    key = jax.random.key(seed)
    ks = jax.random.split(key, 16)
    D = D_MODEL
    f = jnp.float32
    nrm = jax.random.normal
    return {
        "x": nrm(ks[0], (BATCH, SEQ, D), f),
        "meta_tokens": nrm(ks[1], (N_META, D), f),
        "pre_norm_g": 1.0 + 0.05 * nrm(ks[2], (DEPTH, D), f),
        "post_norm_g": 1.0 + 0.05 * nrm(ks[3], (DEPTH, D), f),
        "attn_w_in": nrm(ks[4], (N_ATTN_LAYERS, D, 4 * D), f) * D ** -0.5,
        "attn_w_out": nrm(ks[5], (N_ATTN_LAYERS, D, D), f) * D ** -0.5,
        "attn_lambda_q1": 0.1 * nrm(ks[6], (N_ATTN_LAYERS, DA_HEAD_DIM), f),
        "attn_lambda_k1": 0.1 * nrm(ks[7], (N_ATTN_LAYERS, DA_HEAD_DIM), f),
        "attn_lambda_q2": 0.1 * nrm(ks[8], (N_ATTN_LAYERS, DA_HEAD_DIM), f),
        "attn_lambda_k2": 0.1 * nrm(ks[9], (N_ATTN_LAYERS, DA_HEAD_DIM), f),
        "attn_subln_g": 1.0 + 0.05 * nrm(ks[10], (N_ATTN_LAYERS, DA_V_DIM), f),
        "pool_w_in": nrm(ks[11], (N_POOL_LAYERS, D, 2 * D), f) * D ** -0.5,
        "pool_w_group": nrm(ks[12], (N_POOL_LAYERS, POOL_GROUPS, POOL_GROUP_DIM, POOL_GROUP_DIM), f) * POOL_GROUP_DIM ** -0.5,
        "pool_scale": 1.0 + 0.1 * nrm(ks[13], (N_POOL_LAYERS, D), f),
        "pool_w_out": nrm(ks[14], (N_POOL_LAYERS, D, D), f) * D ** -0.5,
    }


def reference(x, meta_tokens, pre_norm_g, post_norm_g, attn_w_in, attn_w_out,
              attn_lambda_q1, attn_lambda_k1, attn_lambda_q2, attn_lambda_k2, attn_subln_g,
              pool_w_in, pool_w_group, pool_scale, pool_w_out):
    B = x.shape[0]
    meta = jnp.broadcast_to(meta_tokens.astype(x.dtype)[None], (B, N_META, x.shape[-1]))
    h = jnp.concatenate([meta, x], axis=1)
    L = h.shape[1]
    cos, sin = rope_tables(L)
    for i in range(DEPTH):
        j = i // N_MIXERS
        hn = rms_norm(h, pre_norm_g[i])
        if i % N_MIXERS == 0:
            y = diff_attention_mixer(hn, attn_w_in[j], attn_w_out[j], attn_lambda_q1[j],
                                     attn_lambda_k1[j], attn_lambda_q2[j], attn_lambda_k2[j],
                                     attn_subln_g[j], i, cos, sin)
        else:
            y = pool_mixer(hn, pool_w_in[j], pool_w_group[j], pool_scale[j], pool_w_out[j])
        h = h + rms_norm(y, post_norm_g[i])
    return h[:, N_META:]
```

```python
import functools
import math

import jax
import jax.numpy as jnp
from jax import lax
from jax.experimental import pallas as pl
from jax.experimental.pallas import tpu as pltpu

F32 = jnp.float32
BF16 = jnp.bfloat16

N_META = 16
CHUNK = 64
RMS_EPS = 1e-6
DA_HEADS = 8
ROPE_THETA = 500000.0
POOL_WINDOWS = (2, 4, 8, 16)
LAMBDA_INIT_L0 = 0.8 - 0.6 * math.exp(-0.3 * 0)

LANES = 128
VMEM_LIMIT_BYTES = 56 * 1024 * 1024

PROJ_TM = 512
PROJ_TN = 512
ATTN_TQ = 256
ATTN_TK = 256
OUT_TM = 512
POOL_TM = 256

NEG_INF = float(jnp.finfo(jnp.float32).min)


def _rms(x, g):
    ms = jnp.mean(x * x, axis=-1, keepdims=True)
    return x * lax.rsqrt(ms + RMS_EPS) * g


def _silu(g):
    return g * (1.0 / (1.0 + jnp.exp(-g)))


def _dot(a, b):
    return jnp.dot(a, b, preferred_element_type=F32)


def _dot_nt(a, b):
    return lax.dot_general(a, b, (((1,), (1,)), ((), ())), preferred_element_type=F32)


def _norm_proj_kernel(x_ref, g_ref, w_ref, cos_ref, sa_ref, sb_ref, o_ref, hn_ref, *,
                      n_col_tiles_per_part, q_scale):
    j = pl.program_id(1)
    npt = n_col_tiles_per_part

    @pl.when(j == 0)
    def _():
        hn_ref[...] = _rms(x_ref[...], g_ref[...]).astype(BF16)

    acc = _dot(hn_ref[...], w_ref[...])
    tn = acc.shape[1]

    @pl.when(j < 2 * npt)
    def _():
        scale = jnp.where(j < npt, q_scale, 1.0).astype(F32)
        c = cos_ref[...] * scale
        sa = sa_ref[...] * scale
        sb = sb_ref[...] * scale
        half = 16
        for blk in range(tn // LANES):
            t = acc[:, blk * LANES:(blk + 1) * LANES]
            r = (t * c + pltpu.roll(t, half, 1) * sa
                 + pltpu.roll(t, LANES - half, 1) * sb)
            o_ref[:, blk * LANES:(blk + 1) * LANES] = r.astype(o_ref.dtype)

    @pl.when(jnp.logical_and(j >= 2 * npt, j < 3 * npt))
    def _():
        o_ref[...] = acc.astype(o_ref.dtype)

    @pl.when(j >= 3 * npt)
    def _():
        o_ref[...] = _silu(acc).astype(o_ref.dtype)


def _norm_proj(x, g, w, cos_t, sa_t, sb_t, *, tm, tn, rows_per_seq, q_scale):
    m, d = x.shape
    n = w.shape[1]
    tiles_per_seq = rows_per_seq // tm
    kern = functools.partial(_norm_proj_kernel, n_col_tiles_per_part=(n // 4) // tn,
                             q_scale=q_scale)
    tab_spec = pl.BlockSpec((tm, LANES), lambda i, j: (i % tiles_per_seq, 0))
    return pl.pallas_call(
        kern,
        grid=(m // tm, n // tn),
        in_specs=[
            pl.BlockSpec((tm, d), lambda i, j: (i, 0)),
            pl.BlockSpec((1, d), lambda i, j: (0, 0)),
            pl.BlockSpec((d, tn), lambda i, j: (0, j)),
            tab_spec, tab_spec, tab_spec,
        ],
        out_specs=pl.BlockSpec((tm, tn), lambda i, j: (i, j)),
        out_shape=jax.ShapeDtypeStruct((m, n), BF16),
        scratch_shapes=[pltpu.VMEM((tm, d), BF16)],
        compiler_params=pltpu.CompilerParams(
            dimension_semantics=("parallel", "arbitrary"),
            vmem_limit_bytes=VMEM_LIMIT_BYTES),
        name="l0_norm_proj",
    )(x, g, w, cos_t, sa_t, sb_t)


def _lambda_full(lq1_ref, lk1_ref, lq2_ref, lk2_ref):
    a = jnp.sum(lq1_ref[...] * lk1_ref[...], axis=-1, keepdims=True)
    b = jnp.sum(lq2_ref[...] * lk2_ref[...], axis=-1, keepdims=True)
    return jnp.exp(a) - jnp.exp(b) + LAMBDA_INIT_L0


def _diff_finalize(o1, o2, lam, subg, gate):
    o = o1 - lam * o2
    o = _rms(o, subg) * (1.0 - LAMBDA_INIT_L0)
    return (o * gate.astype(F32)).astype(BF16)


def _attn_kernel(q_ref, k_ref, v_ref, gate_ref, km_ref, vm_ref, lq1_ref, lk1_ref, lq2_ref,
                 lk2_ref, subg_ref, o_ref, m_ref, l_ref, acc_ref, *, tq, tk, dh):
    i = pl.program_id(2)
    q = q_ref[...]
    q1 = q[:, :dh]
    q2 = q[:, dh:]
    dv = acc_ref.shape[1]

    def scores(kb):
        return jnp.concatenate([_dot_nt(q1, kb[:, :dh]), _dot_nt(q2, kb[:, dh:])], axis=0)

    s = scores(km_ref[...])
    m0 = jnp.max(s, axis=-1, keepdims=True)
    p = jnp.exp(s - m0)
    m_ref[...] = jnp.broadcast_to(m0, m_ref.shape)
    l_ref[...] = jnp.broadcast_to(jnp.sum(p, axis=-1, keepdims=True), l_ref.shape)
    acc_ref[...] = _dot(p.astype(BF16), vm_ref[...])

    def update(s, vb):
        m_prev = m_ref[...]
        m_new = jnp.maximum(m_prev, jnp.max(s, axis=-1, keepdims=True))
        alpha = jnp.exp(m_prev - m_new)
        p = jnp.exp(s - jnp.tile(m_new, (1, s.shape[1] // LANES)))
        l_ref[...] = alpha * l_ref[...] + jnp.sum(p, axis=-1, keepdims=True)
        m_ref[...] = m_new
        acc_ref[...] = (acc_ref[...] * jnp.tile(alpha, (1, dv // LANES))
                        + _dot(p.astype(BF16), vb))

    def full_block(jb, carry):
        off = pl.multiple_of(jb * tk, tk)
        update(scores(k_ref[pl.ds(off, tk), :]), v_ref[pl.ds(off, tk), :])
        return carry

    lax.fori_loop(0, i * (tq // tk), full_block, 0)

    row_chunk = (lax.broadcasted_iota(jnp.int32, (2 * tq, tk), 0) % tq) // CHUNK
    col = lax.broadcasted_iota(jnp.int32, (2 * tq, tk), 1)
    for d in range(tq // tk):
        off = pl.multiple_of(i * tq + d * tk, tk)
        s = scores(k_ref[pl.ds(off, tk), :])
        s = jnp.where((col + d * tk) // CHUNK <= row_chunk, s, NEG_INF)
        update(s, v_ref[pl.ds(off, tk), :])

    inv_l = 1.0 / l_ref[...]
    acc = acc_ref[...]
    o1 = acc[:tq] * jnp.tile(inv_l[:tq], (1, dv // LANES))
    o2 = acc[tq:] * jnp.tile(inv_l[tq:], (1, dv // LANES))
    lam = _lambda_full(lq1_ref, lk1_ref, lq2_ref, lk2_ref)
    o_ref[...] = _diff_finalize(o1, o2, lam, subg_ref[...], gate_ref[...])


def _attention(qkvg, qkvg_meta, lq1, lk1, lq2, lk2, subg, *, batch, seq, d_model):
    dv = d_model // DA_HEADS
    dh = dv // 2
    tq, tk = ATTN_TQ, ATTN_TK
    nq = seq // tq
    h_ = DA_HEADS
    kern = functools.partial(_attn_kernel, tq=tq, tk=tk, dh=dh)
    vec = lambda n: pl.BlockSpec((1, n), lambda b, h, i: (0, 0))
    return pl.pallas_call(
        kern,
        grid=(batch, h_, nq),
        in_specs=[
            pl.BlockSpec((tq, dv), lambda b, h, i: (b * nq + i, h)),
            pl.BlockSpec((seq, dv), lambda b, h, i: (b, h_ + h)),
            pl.BlockSpec((seq, dv), lambda b, h, i: (b, 2 * h_ + h)),
            pl.BlockSpec((tq, dv), lambda b, h, i: (b * nq + i, 3 * h_ + h)),
            pl.BlockSpec((N_META, dv), lambda b, h, i: (0, h_ + h)),
            pl.BlockSpec((N_META, dv), lambda b, h, i: (0, 2 * h_ + h)),
            vec(dh), vec(dh), vec(dh), vec(dh), vec(dv),
        ],
        out_specs=pl.BlockSpec((tq, dv), lambda b, h, i: (b * nq + i, h)),
        out_shape=jax.ShapeDtypeStruct((batch * seq, d_model), BF16),
        scratch_shapes=[pltpu.VMEM((2 * tq, LANES), F32), pltpu.VMEM((2 * tq, LANES), F32),
                        pltpu.VMEM((2 * tq, dv), F32)],
        compiler_params=pltpu.CompilerParams(
            dimension_semantics=("parallel", "parallel", "arbitrary"),
            vmem_limit_bytes=VMEM_LIMIT_BYTES),
        name="l0_diff_attn",
    )(qkvg, qkvg, qkvg, qkvg, qkvg_meta, qkvg_meta, lq1, lk1, lq2, lk2, subg)


def _meta_attn_kernel(q_ref, k_ref, v_ref, gate_ref, lq1_ref, lk1_ref, lq2_ref, lk2_ref,
                      subg_ref, o_ref, *, dh):
    q = q_ref[...]
    k = k_ref[...]
    v = v_ref[...]

    def attend(qm, km):
        s = _dot_nt(qm, km)
        p = jnp.exp(s - jnp.max(s, axis=-1, keepdims=True))
        return _dot(p.astype(BF16), v) * (1.0 / jnp.sum(p, axis=-1, keepdims=True))

    lam = _lambda_full(lq1_ref, lk1_ref, lq2_ref, lk2_ref)
    o_ref[...] = _diff_finalize(attend(q[:, :dh], k[:, :dh]), attend(q[:, dh:], k[:, dh:]),
                                lam, subg_ref[...], gate_ref[...])


def _meta_attention(qkvg_meta, lq1, lk1, lq2, lk2, subg, *, d_model):
    dv = d_model // DA_HEADS
    dh = dv // 2
    h_ = DA_HEADS
    vec = lambda n: pl.BlockSpec((1, n), lambda h: (0, 0))
    blk = lambda part: pl.BlockSpec((N_META, dv), lambda h: (0, part * h_ + h))
    return pl.pallas_call(
        functools.partial(_meta_attn_kernel, dh=dh),
        grid=(h_,),
        in_specs=[blk(0), blk(1), blk(2), blk(3), vec(dh), vec(dh), vec(dh), vec(dh), vec(dv)],
        out_specs=pl.BlockSpec((N_META, dv), lambda h: (0, h)),
        out_shape=jax.ShapeDtypeStruct((N_META, d_model), BF16),
        compiler_params=pltpu.CompilerParams(dimension_semantics=("parallel",)),
        name="l0_meta_attn",
    )(qkvg_meta, qkvg_meta, qkvg_meta, qkvg_meta, lq1, lk1, lq2, lk2, subg)


def _out_proj_kernel(a_ref, w_ref, x_ref, g_ref, o_ref):
    y = _dot(a_ref[...], w_ref[...])
    o_ref[...] = x_ref[...] + _rms(y, g_ref[...])


def _out_proj(a, w, x, g, *, tm):
    m, d = x.shape
    return pl.pallas_call(
        _out_proj_kernel,
        grid=(m // tm,),
        in_specs=[
            pl.BlockSpec((tm, d), lambda i: (i, 0)),
            pl.BlockSpec((d, d), lambda i: (0, 0), pipeline_mode=pl.Buffered(1)),
            pl.BlockSpec((tm, d), lambda i: (i, 0)),
            pl.BlockSpec((1, d), lambda i: (0, 0)),
        ],
        out_specs=pl.BlockSpec((tm, d), lambda i: (i, 0)),
        out_shape=jax.ShapeDtypeStruct((m, d), F32),
        compiler_params=pltpu.CompilerParams(
            dimension_semantics=("parallel",), vmem_limit_bytes=VMEM_LIMIT_BYTES),
        name="l0_out_proj",
    )(a, w, x, g)


def _pool_layer_kernel(h_ref, hm_ref, pre_g_ref, w_in_ref, w_grp_ref, scale_ref, w_out_ref,
                       post_g_ref, o_ref, u_ext_ref, z_ref, *, tm, d, halo):
    i = pl.program_id(1)
    pre_g = pre_g_ref[...]

    @pl.when(i == 0)
    def _():
        hmn = _rms(hm_ref[...], pre_g).astype(BF16)
        u_ext_ref[0:halo, :] = _dot(hmn, w_in_ref[:, 0:d])

    @pl.when(i > 0)
    def _():
        u_ext_ref[0:halo, :] = u_ext_ref[tm:tm + halo, :]

    h = h_ref[...]
    hn = _rms(h, pre_g).astype(BF16)
    u_ext_ref[halo:halo + tm, :] = _dot(hn, w_in_ref[:, 0:d])

    gsz = d // len(POOL_WINDOWS)
    for gi, w in enumerate(POOL_WINDOWS):
        c0 = gi * gsz
        u = u_ext_ref[halo:halo + tm, c0:c0 + gsz]
        s = u
        for back in range(1, w):
            s = s + u_ext_ref[halo - back:halo - back + tm, c0:c0 + gsz]
        mixed = (s * (1.0 / w) - u).astype(BF16)
        mo = _dot(mixed, w_grp_ref[gi]) * scale_ref[:, c0:c0 + gsz]
        gate = _dot(hn, w_in_ref[:, d + c0:d + c0 + gsz])
        z_ref[:, c0:c0 + gsz] = (mo * _silu(gate)).astype(BF16)

    y = _dot(z_ref[...], w_out_ref[...])
    o_ref[...] = h + _rms(y, post_g_ref[...])


def _pool_layer(h, h_meta, pre_g, w_in, w_grp, scale, w_out, post_g, *, batch, seq, tm):
    m, d = h.shape
    nt = seq // tm
    halo = N_META
    assert max(POOL_WINDOWS) <= halo and halo <= tm
    const2 = lambda b, i: (0, 0)
    resident = lambda shape, imap: pl.BlockSpec(shape, imap, pipeline_mode=pl.Buffered(1))
    kern = functools.partial(_pool_layer_kernel, tm=tm, d=d, halo=halo)
    return pl.pallas_call(
        kern,
        grid=(batch, nt),
        in_specs=[
            pl.BlockSpec((tm, d), lambda b, i: (b * nt + i, 0)),
            resident((N_META, d), const2),
            pl.BlockSpec((1, d), const2),
            resident((d, 2 * d), const2),
            resident(w_grp.shape, lambda b, i: (0, 0, 0)),
            pl.BlockSpec((1, d), const2),
            resident((d, d), const2),
            pl.BlockSpec((1, d), const2),
        ],
        out_specs=pl.BlockSpec((tm, d), lambda b, i: (b * nt + i, 0)),
        out_shape=jax.ShapeDtypeStruct((m, d), F32),
        scratch_shapes=[pltpu.VMEM((halo + tm, d), F32), pltpu.VMEM((tm, d), BF16)],
        compiler_params=pltpu.CompilerParams(
            dimension_semantics=("arbitrary", "arbitrary"),
            vmem_limit_bytes=VMEM_LIMIT_BYTES),
        name="l1_pool_layer",
    )(h, h_meta, pre_g, w_in, w_grp, scale, w_out, post_g)


def _rope_tables(length, rope_dim):
    half = rope_dim // 2
    pos = jnp.arange(length, dtype=F32)
    inv = ROPE_THETA ** (-jnp.arange(0, rope_dim, 2, dtype=F32) / rope_dim)
    ang = pos[:, None] * inv[None, :]
    cos, sin = jnp.cos(ang), jnp.sin(ang)
    zeros = lambda n: jnp.zeros((length, n), F32)
    cos_t = jnp.concatenate([cos, cos, jnp.ones((length, LANES - rope_dim), F32)], axis=1)
    sa_t = jnp.concatenate([zeros(half), sin, zeros(LANES - rope_dim)], axis=1)
    sb_t = jnp.concatenate([-sin, zeros(LANES - half)], axis=1)
    return cos_t, sa_t, sb_t


def kernel(x, meta_tokens, pre_norm_g, post_norm_g, attn_w_in, attn_w_out, attn_lambda_q1,
           attn_lambda_k1, attn_lambda_q2, attn_lambda_k2, attn_subln_g, pool_w_in,
           pool_w_group, pool_scale, pool_w_out):
    batch, seq, d = x.shape
    dv = d // DA_HEADS
    dh = dv // 2
    assert dh == LANES and seq % CHUNK == 0
    xf = x.reshape(batch * seq, d)
    meta = meta_tokens.astype(x.dtype)

    cos_t, sa_t, sb_t = _rope_tables(N_META + seq, dh // 4)
    w_in0 = attn_w_in[0].astype(BF16)
    w_out0 = attn_w_out[0].astype(BF16)
    pre_g0, pre_g1 = pre_norm_g[0:1], pre_norm_g[1:2]
    post_g0, post_g1 = post_norm_g[0:1], post_norm_g[1:2]
    q_scale = float(dh) ** -0.5
    lam = [p[0:1] for p in (attn_lambda_q1, attn_lambda_k1, attn_lambda_q2, attn_lambda_k2)]
    subg = attn_subln_g[0:1]

    proj = functools.partial(_norm_proj, tn=PROJ_TN, q_scale=q_scale)
    qkvg_meta = proj(meta, pre_g0, w_in0, cos_t[:N_META], sa_t[:N_META], sb_t[:N_META],
                     tm=N_META, rows_per_seq=N_META)
    qkvg = proj(xf, pre_g0, w_in0, cos_t[N_META:], sa_t[N_META:], sb_t[N_META:],
                tm=PROJ_TM, rows_per_seq=seq)
    a_meta = _meta_attention(qkvg_meta, *lam, subg, d_model=d)
    a = _attention(qkvg, qkvg_meta, *lam, subg, batch=batch, seq=seq, d_model=d)
    h1_meta = _out_proj(a_meta, w_out0, meta, post_g0, tm=N_META)
    h1 = _out_proj(a, w_out0, xf, post_g0, tm=OUT_TM)

    out = _pool_layer(h1, h1_meta, pre_g1, pool_w_in[0].astype(BF16),
                      pool_w_group[0].astype(BF16), pool_scale[0:1],
                      pool_w_out[0].astype(BF16), post_g1, batch=batch, seq=seq, tm=POOL_TM)
    return out.reshape(batch, seq, d)
```

```python
import functools
import math

import jax
import jax.numpy as jnp
from jax import lax
from jax.experimental import pallas as pl
from jax.experimental.pallas import tpu as pltpu

F32 = jnp.float32
BF16 = jnp.bfloat16

N_META = 16
CHUNK = 64
RMS_EPS = 1e-6
DA_HEADS = 8
ROPE_THETA = 500000.0
POOL_WINDOWS = (2, 4, 8, 16)
LAMBDA_INIT_L0 = 0.8 - 0.6 * math.exp(-0.3 * 0)

LANES = 128
VMEM_LIMIT_BYTES = 56 * 1024 * 1024

PROJ_TM = 512
PROJ_TN = 512
ATTN_TQ = 1024
ATTN_TK = 256
ATTN_SUBTILES = 4
ATTN_KEY_UNROLL = 4
OUT_TM = 512
POOL_TM = 256

NEG_INF = float(jnp.finfo(jnp.float32).min)


def _rms(x, g):
    ms = jnp.mean(x * x, axis=-1, keepdims=True)
    return x * lax.rsqrt(ms + RMS_EPS) * g


def _silu(g):
    return g * (1.0 / (1.0 + jnp.exp(-g)))


def _dot(a, b):
    return jnp.dot(a, b, preferred_element_type=F32)


def _dot_nt(a, b):
    return lax.dot_general(a, b, (((1,), (1,)), ((), ())), preferred_element_type=F32)


def _norm_proj_kernel(x_ref, g_ref, w_ref, cos_ref, sa_ref, sb_ref, o_ref, hn_ref, *,
                      n_col_tiles_per_part, q_scale):
    j = pl.program_id(1)
    npt = n_col_tiles_per_part

    @pl.when(j == 0)
    def _():
        hn_ref[...] = _rms(x_ref[...], g_ref[...]).astype(BF16)

    acc = _dot(hn_ref[...], w_ref[...])
    tn = acc.shape[1]

    @pl.when(j < 2 * npt)
    def _():
        scale = jnp.where(j < npt, q_scale, 1.0).astype(F32)
        c = cos_ref[...] * scale
        sa = sa_ref[...] * scale
        sb = sb_ref[...] * scale
        half = 16
        for blk in range(tn // LANES):
            t = acc[:, blk * LANES:(blk + 1) * LANES]
            r = (t * c + pltpu.roll(t, half, 1) * sa
                 + pltpu.roll(t, LANES - half, 1) * sb)
            o_ref[:, blk * LANES:(blk + 1) * LANES] = r.astype(o_ref.dtype)

    @pl.when(jnp.logical_and(j >= 2 * npt, j < 3 * npt))
    def _():
        o_ref[...] = acc.astype(o_ref.dtype)

    @pl.when(j >= 3 * npt)
    def _():
        o_ref[...] = _silu(acc).astype(o_ref.dtype)


def _norm_proj(x, g, w, cos_t, sa_t, sb_t, *, tm, tn, rows_per_seq, q_scale):
    m, d = x.shape
    n = w.shape[1]
    tiles_per_seq = rows_per_seq // tm
    kern = functools.partial(_norm_proj_kernel, n_col_tiles_per_part=(n // 4) // tn,
                             q_scale=q_scale)
    tab_spec = pl.BlockSpec((tm, LANES), lambda i, j: (i % tiles_per_seq, 0))
    return pl.pallas_call(
        kern,
        grid=(m // tm, n // tn),
        in_specs=[
            pl.BlockSpec((tm, d), lambda i, j: (i, 0)),
            pl.BlockSpec((1, d), lambda i, j: (0, 0)),
            pl.BlockSpec((d, tn), lambda i, j: (0, j)),
            tab_spec, tab_spec, tab_spec,
        ],
        out_specs=pl.BlockSpec((tm, tn), lambda i, j: (i, j)),
        out_shape=jax.ShapeDtypeStruct((m, n), BF16),
        scratch_shapes=[pltpu.VMEM((tm, d), BF16)],
        compiler_params=pltpu.CompilerParams(
            dimension_semantics=("parallel", "arbitrary"),
            vmem_limit_bytes=VMEM_LIMIT_BYTES),
        name="l0_norm_proj",
    )(x, g, w, cos_t, sa_t, sb_t)


def _lambda_full(lq1_ref, lk1_ref, lq2_ref, lk2_ref):
    a = jnp.sum(lq1_ref[...] * lk1_ref[...], axis=-1, keepdims=True)
    b = jnp.sum(lq2_ref[...] * lk2_ref[...], axis=-1, keepdims=True)
    return jnp.exp(a) - jnp.exp(b) + LAMBDA_INIT_L0


def _diff_finalize(o1, o2, lam, subg, gate):
    o = o1 - lam * o2
    o = _rms(o, subg) * (1.0 - LAMBDA_INIT_L0)
    return (o * gate.astype(F32)).astype(BF16)


def _attn_kernel(q_ref, k_ref, v_ref, gate_ref, km_ref, vm_ref, lq1_ref, lk1_ref, lq2_ref,
                 lk2_ref, subg_ref, o_ref, m_ref, l_ref, acc_ref, *, tq, tk, dh, n_sub,
                 unroll):
    i = pl.program_id(2)
    ts = tq // n_sub
    dv = acc_ref.shape[2]

    def scores(c, kb):
        q = q_ref[c * ts:(c + 1) * ts, :]
        return jnp.concatenate([_dot_nt(q[:, :dh], kb[:, :dh]), _dot_nt(q[:, dh:], kb[:, dh:])],
                               axis=0)

    lane0 = lax.broadcasted_iota(jnp.int32, (2 * ts, LANES), 1) == 0
    for c in range(n_sub):
        s = scores(c, km_ref[...])
        m0 = jnp.max(s, axis=-1, keepdims=True)
        p = jnp.exp2(s - m0)
        m_ref[c] = jnp.broadcast_to(m0, (2 * ts, LANES))
        l_ref[c] = jnp.where(lane0, jnp.sum(p, axis=-1, keepdims=True), 0.0)
        acc_ref[c] = _dot(p.astype(BF16), vm_ref[...])

    def update(c, s, vb):
        m_prev = m_ref[c]
        m_new = jnp.maximum(m_prev, jnp.max(s, axis=-1, keepdims=True))
        alpha = jnp.exp2(m_prev - m_new)
        p = jnp.exp2(s - jnp.tile(m_new, (1, s.shape[1] // LANES)))
        psum = p[:, 0:LANES]
        for blk in range(1, s.shape[1] // LANES):
            psum = psum + p[:, blk * LANES:(blk + 1) * LANES]
        l_ref[c] = alpha * l_ref[c] + psum
        m_ref[c] = m_new
        acc_ref[c] = (acc_ref[c] * jnp.tile(alpha, (1, dv // LANES))
                      + _dot(p.astype(BF16), vb))

    def full_blocks(jb, carry):
        for u in range(unroll):
            off = pl.multiple_of((jb * unroll + u) * tk, tk)
            kb = k_ref[pl.ds(off, tk), :]
            vb = v_ref[pl.ds(off, tk), :]
            for c in range(n_sub):
                update(c, scores(c, kb), vb)
        return carry

    lax.fori_loop(0, i * (tq // tk // unroll), full_blocks, 0)

    row = lax.broadcasted_iota(jnp.int32, (2 * ts, tk), 0) % ts
    col = lax.broadcasted_iota(jnp.int32, (2 * ts, tk), 1)
    for d in range(tq // tk):
        off = pl.multiple_of(i * tq + d * tk, tk)
        kb = k_ref[pl.ds(off, tk), :]
        vb = v_ref[pl.ds(off, tk), :]
        for c in range(n_sub):
            if d * tk >= (c + 1) * ts:
                continue
            s = scores(c, kb)
            if (d + 1) * tk > c * ts:
                visible = (col + d * tk) // CHUNK <= (row + c * ts) // CHUNK
                s = jnp.where(visible, s, NEG_INF)
            update(c, s, vb)

    lam = _lambda_full(lq1_ref, lk1_ref, lq2_ref, lk2_ref)
    for c in range(n_sub):
        inv_l = 1.0 / jnp.sum(l_ref[c], axis=-1, keepdims=True)
        acc = acc_ref[c]
        o_ref[c * ts:(c + 1) * ts, :] = _diff_finalize(
            acc[:ts] * inv_l[:ts], acc[ts:] * inv_l[ts:], lam, subg_ref[...],
            gate_ref[c * ts:(c + 1) * ts, :])


def _attention(qkvg, qkvg_meta, lq1, lk1, lq2, lk2, subg, *, batch, seq, d_model):
    dv = d_model // DA_HEADS
    dh = dv // 2
    tq, tk, n_sub = ATTN_TQ, ATTN_TK, ATTN_SUBTILES
    ts = tq // n_sub
    unroll = ATTN_KEY_UNROLL
    assert ts % CHUNK == 0 and tk % CHUNK == 0 and tq % (tk * unroll) == 0
    nq = seq // tq
    h_ = DA_HEADS
    kern = functools.partial(_attn_kernel, tq=tq, tk=tk, dh=dh, n_sub=n_sub, unroll=unroll)
    vec = lambda n: pl.BlockSpec((1, n), lambda b, h, i: (0, 0))
    return pl.pallas_call(
        kern,
        grid=(batch, h_, nq),
        in_specs=[
            pl.BlockSpec((tq, dv), lambda b, h, i: (b * nq + i, h)),
            pl.BlockSpec((seq, dv), lambda b, h, i: (b, h_ + h)),
            pl.BlockSpec((seq, dv), lambda b, h, i: (b, 2 * h_ + h)),
            pl.BlockSpec((tq, dv), lambda b, h, i: (b * nq + i, 3 * h_ + h)),
            pl.BlockSpec((N_META, dv), lambda b, h, i: (0, h_ + h)),
            pl.BlockSpec((N_META, dv), lambda b, h, i: (0, 2 * h_ + h)),
            vec(dh), vec(dh), vec(dh), vec(dh), vec(dv),
        ],
        out_specs=pl.BlockSpec((tq, dv), lambda b, h, i: (b * nq + i, h)),
        out_shape=jax.ShapeDtypeStruct((batch * seq, d_model), BF16),
        scratch_shapes=[pltpu.VMEM((n_sub, 2 * ts, LANES), F32),
                        pltpu.VMEM((n_sub, 2 * ts, LANES), F32),
                        pltpu.VMEM((n_sub, 2 * ts, dv), F32)],
        compiler_params=pltpu.CompilerParams(
            dimension_semantics=("parallel", "parallel", "arbitrary"),
            vmem_limit_bytes=VMEM_LIMIT_BYTES),
        name="l0_diff_attn",
    )(qkvg, qkvg, qkvg, qkvg, qkvg_meta, qkvg_meta, lq1, lk1, lq2, lk2, subg)


def _meta_attn_kernel(q_ref, k_ref, v_ref, gate_ref, lq1_ref, lk1_ref, lq2_ref, lk2_ref,
                      subg_ref, o_ref, *, dh):
    q = q_ref[...]
    k = k_ref[...]
    v = v_ref[...]

    def attend(qm, km):
        s = _dot_nt(qm, km)
        p = jnp.exp2(s - jnp.max(s, axis=-1, keepdims=True))
        return _dot(p.astype(BF16), v) * (1.0 / jnp.sum(p, axis=-1, keepdims=True))

    lam = _lambda_full(lq1_ref, lk1_ref, lq2_ref, lk2_ref)
    o_ref[...] = _diff_finalize(attend(q[:, :dh], k[:, :dh]), attend(q[:, dh:], k[:, dh:]),
                                lam, subg_ref[...], gate_ref[...])


def _meta_attention(qkvg_meta, lq1, lk1, lq2, lk2, subg, *, d_model):
    dv = d_model // DA_HEADS
    dh = dv // 2
    h_ = DA_HEADS
    vec = lambda n: pl.BlockSpec((1, n), lambda h: (0, 0))
    blk = lambda part: pl.BlockSpec((N_META, dv), lambda h: (0, part * h_ + h))
    return pl.pallas_call(
        functools.partial(_meta_attn_kernel, dh=dh),
        grid=(h_,),
        in_specs=[blk(0), blk(1), blk(2), blk(3), vec(dh), vec(dh), vec(dh), vec(dh), vec(dv)],
        out_specs=pl.BlockSpec((N_META, dv), lambda h: (0, h)),
        out_shape=jax.ShapeDtypeStruct((N_META, d_model), BF16),
        compiler_params=pltpu.CompilerParams(dimension_semantics=("parallel",)),
        name="l0_meta_attn",
    )(qkvg_meta, qkvg_meta, qkvg_meta, qkvg_meta, lq1, lk1, lq2, lk2, subg)


def _out_proj_kernel(a_ref, w_ref, x_ref, g_ref, o_ref):
    y = _dot(a_ref[...], w_ref[...])
    o_ref[...] = x_ref[...] + _rms(y, g_ref[...])


def _out_proj(a, w, x, g, *, tm):
    m, d = x.shape
    return pl.pallas_call(
        _out_proj_kernel,
        grid=(m // tm,),
        in_specs=[
            pl.BlockSpec((tm, d), lambda i: (i, 0)),
            pl.BlockSpec((d, d), lambda i: (0, 0), pipeline_mode=pl.Buffered(1)),
            pl.BlockSpec((tm, d), lambda i: (i, 0)),
            pl.BlockSpec((1, d), lambda i: (0, 0)),
        ],
        out_specs=pl.BlockSpec((tm, d), lambda i: (i, 0)),
        out_shape=jax.ShapeDtypeStruct((m, d), F32),
        compiler_params=pltpu.CompilerParams(
            dimension_semantics=("parallel",), vmem_limit_bytes=VMEM_LIMIT_BYTES),
        name="l0_out_proj",
    )(a, w, x, g)


def _pool_layer_kernel(h_ref, hm_ref, pre_g_ref, w_in_ref, w_grp_ref, scale_ref, w_out_ref,
                       post_g_ref, o_ref, u_ext_ref, z_ref, *, tm, d, halo):
    i = pl.program_id(1)
    pre_g = pre_g_ref[...]

    @pl.when(i == 0)
    def _():
        hmn = _rms(hm_ref[...], pre_g).astype(BF16)
        u_ext_ref[0:halo, :] = _dot(hmn, w_in_ref[:, 0:d])

    @pl.when(i > 0)
    def _():
        u_ext_ref[0:halo, :] = u_ext_ref[tm:tm + halo, :]

    h = h_ref[...]
    hn = _rms(h, pre_g).astype(BF16)
    u_ext_ref[halo:halo + tm, :] = _dot(hn, w_in_ref[:, 0:d])

    gsz = d // len(POOL_WINDOWS)
    for gi, w in enumerate(POOL_WINDOWS):
        c0 = gi * gsz
        u = u_ext_ref[halo:halo + tm, c0:c0 + gsz]
        s = u
        for back in range(1, w):
            s = s + u_ext_ref[halo - back:halo - back + tm, c0:c0 + gsz]
        mixed = (s * (1.0 / w) - u).astype(BF16)
        mo = _dot(mixed, w_grp_ref[gi]) * scale_ref[:, c0:c0 + gsz]
        gate = _dot(hn, w_in_ref[:, d + c0:d + c0 + gsz])
        z_ref[:, c0:c0 + gsz] = (mo * _silu(gate)).astype(BF16)

    y = _dot(z_ref[...], w_out_ref[...])
    o_ref[...] = h + _rms(y, post_g_ref[...])


def _pool_layer(h, h_meta, pre_g, w_in, w_grp, scale, w_out, post_g, *, batch, seq, tm):
    m, d = h.shape
    nt = seq // tm
    halo = N_META
    assert max(POOL_WINDOWS) <= halo and halo <= tm
    const2 = lambda b, i: (0, 0)
    resident = lambda shape, imap: pl.BlockSpec(shape, imap, pipeline_mode=pl.Buffered(1))
    kern = functools.partial(_pool_layer_kernel, tm=tm, d=d, halo=halo)
    return pl.pallas_call(
        kern,
        grid=(batch, nt),
        in_specs=[
            pl.BlockSpec((tm, d), lambda b, i: (b * nt + i, 0)),
            resident((N_META, d), const2),
            pl.BlockSpec((1, d), const2),
            resident((d, 2 * d), const2),
            resident(w_grp.shape, lambda b, i: (0, 0, 0)),
            pl.BlockSpec((1, d), const2),
            resident((d, d), const2),
            pl.BlockSpec((1, d), const2),
        ],
        out_specs=pl.BlockSpec((tm, d), lambda b, i: (b * nt + i, 0)),
        out_shape=jax.ShapeDtypeStruct((m, d), F32),
        scratch_shapes=[pltpu.VMEM((halo + tm, d), F32), pltpu.VMEM((tm, d), BF16)],
        compiler_params=pltpu.CompilerParams(
            dimension_semantics=("arbitrary", "arbitrary"),
            vmem_limit_bytes=VMEM_LIMIT_BYTES),
        name="l1_pool_layer",
    )(h, h_meta, pre_g, w_in, w_grp, scale, w_out, post_g)


def _rope_tables(length, rope_dim):
    half = rope_dim // 2
    pos = jnp.arange(length, dtype=F32)
    inv = ROPE_THETA ** (-jnp.arange(0, rope_dim, 2, dtype=F32) / rope_dim)
    ang = pos[:, None] * inv[None, :]
    cos, sin = jnp.cos(ang), jnp.sin(ang)
    zeros = lambda n: jnp.zeros((length, n), F32)
    cos_t = jnp.concatenate([cos, cos, jnp.ones((length, LANES - rope_dim), F32)], axis=1)
    sa_t = jnp.concatenate([zeros(half), sin, zeros(LANES - rope_dim)], axis=1)
    sb_t = jnp.concatenate([-sin, zeros(LANES - half)], axis=1)
    return cos_t, sa_t, sb_t


def kernel(x, meta_tokens, pre_norm_g, post_norm_g, attn_w_in, attn_w_out, attn_lambda_q1,
           attn_lambda_k1, attn_lambda_q2, attn_lambda_k2, attn_subln_g, pool_w_in,
           pool_w_group, pool_scale, pool_w_out):
    batch, seq, d = x.shape
    dv = d // DA_HEADS
    dh = dv // 2
    assert dh == LANES and seq % CHUNK == 0
    xf = x.reshape(batch * seq, d)
    meta = meta_tokens.astype(x.dtype)

    cos_t, sa_t, sb_t = _rope_tables(N_META + seq, dh // 4)
    w_in0 = attn_w_in[0].astype(BF16)
    w_out0 = attn_w_out[0].astype(BF16)
    pre_g0, pre_g1 = pre_norm_g[0:1], pre_norm_g[1:2]
    post_g0, post_g1 = post_norm_g[0:1], post_norm_g[1:2]
    q_scale = float(dh) ** -0.5 * math.log2(math.e)
    lam = [p[0:1] for p in (attn_lambda_q1, attn_lambda_k1, attn_lambda_q2, attn_lambda_k2)]
    subg = attn_subln_g[0:1]

    proj = functools.partial(_norm_proj, tn=PROJ_TN, q_scale=q_scale)
    qkvg_meta = proj(meta, pre_g0, w_in0, cos_t[:N_META], sa_t[:N_META], sb_t[:N_META],
                     tm=N_META, rows_per_seq=N_META)
    qkvg = proj(xf, pre_g0, w_in0, cos_t[N_META:], sa_t[N_META:], sb_t[N_META:],
                tm=PROJ_TM, rows_per_seq=seq)
    a_meta = _meta_attention(qkvg_meta, *lam, subg, d_model=d)
    a = _attention(qkvg, qkvg_meta, *lam, subg, batch=batch, seq=seq, d_model=d)
    h1_meta = _out_proj(a_meta, w_out0, meta, post_g0, tm=N_META)
    h1 = _out_proj(a, w_out0, xf, post_g0, tm=OUT_TM)

    out = _pool_layer(h1, h1_meta, pre_g1, pool_w_in[0].astype(BF16),
                      pool_w_group[0].astype(BF16), pool_scale[0:1],
                      pool_w_out[0].astype(BF16), post_g1, batch=batch, seq=seq, tm=POOL_TM)
    return out.reshape(batch, seq, d)
```

```python
import functools
import math

import jax
import jax.numpy as jnp
from jax import lax
from jax.experimental import pallas as pl
from jax.experimental.pallas import tpu as pltpu

F32 = jnp.float32
BF16 = jnp.bfloat16

N_META = 16
CHUNK = 64
RMS_EPS = 1e-6
DA_HEADS = 8
ROPE_THETA = 500000.0
POOL_WINDOWS = (2, 4, 8, 16)
LAMBDA_INIT_L0 = 0.8 - 0.6 * math.exp(-0.3 * 0)

LANES = 128
VMEM_LIMIT_BYTES = 56 * 1024 * 1024

PROJ_TM = 1024
PROJ_TN = 1024
PROJ_SUB_ROWS = 512
PROJ_SUB_COLS = 256
ATTN_TQ = 1024
ATTN_TK = 256
ATTN_SUBTILES = 4
ATTN_KEY_UNROLL = 4
OUT_TM = 512
POOL_TM = 256

NEG_INF = float(jnp.finfo(jnp.float32).min)


def _rms(x, g):
    ms = jnp.mean(x * x, axis=-1, keepdims=True)
    return x * lax.rsqrt(ms + RMS_EPS) * g


def _silu(g):
    return g * (1.0 / (1.0 + jnp.exp(-g)))


def _dot(a, b):
    return jnp.dot(a, b, preferred_element_type=F32)


def _dot_nt(a, b):
    return lax.dot_general(a, b, (((1,), (1,)), ((), ())), preferred_element_type=F32)


def _norm_proj_kernel(x_ref, g_ref, w_ref, cos_ref, sa_ref, sb_ref, o_ref, hn_ref, *,
                      n_col_tiles_per_part, q_scale, sub_rows, sub_cols):
    j = pl.program_id(1)
    npt = n_col_tiles_per_part

    @pl.when(j == 0)
    def _():
        hn_ref[...] = _rms(x_ref[...], g_ref[...]).astype(BF16)

    tm, tn = o_ref.shape
    mc, nc = min(tm, sub_rows), min(tn, sub_cols)
    is_rope = j < 2 * npt
    is_gate = j >= 3 * npt
    scale = jnp.where(j < npt, q_scale, 1.0).astype(F32)
    half = 16
    for mi in range(tm // mc):
        rows = slice(mi * mc, (mi + 1) * mc)
        c = jnp.where(is_rope, cos_ref[rows, :] * scale, 1.0)
        sa = jnp.where(is_rope, sa_ref[rows, :] * scale, 0.0)
        sb = jnp.where(is_rope, sb_ref[rows, :] * scale, 0.0)
        for ni in range(tn // nc):
            acc = _dot(hn_ref[rows, :], w_ref[:, ni * nc:(ni + 1) * nc])
            for blk in range(nc // LANES):
                t = acc[:, blk * LANES:(blk + 1) * LANES]
                r = t * c + pltpu.roll(t, half, 1) * sa + pltpu.roll(t, LANES - half, 1) * sb
                f = jnp.where(is_gate, 1.0 / (1.0 + jnp.exp(-t)), 1.0)
                col = ni * nc + blk * LANES
                o_ref[rows, col:col + LANES] = (r * f).astype(o_ref.dtype)


def _norm_proj(x, g, w, cos_t, sa_t, sb_t, *, tm, tn, rows_per_seq, q_scale):
    m, d = x.shape
    n = w.shape[1]
    tiles_per_seq = rows_per_seq // tm
    kern = functools.partial(_norm_proj_kernel, n_col_tiles_per_part=(n // 4) // tn,
                             q_scale=q_scale, sub_rows=PROJ_SUB_ROWS, sub_cols=PROJ_SUB_COLS)
    tab_spec = pl.BlockSpec((tm, LANES), lambda i, j: (i % tiles_per_seq, 0))
    return pl.pallas_call(
        kern,
        grid=(m // tm, n // tn),
        in_specs=[
            pl.BlockSpec((tm, d), lambda i, j: (i, 0)),
            pl.BlockSpec((1, d), lambda i, j: (0, 0)),
            pl.BlockSpec((d, tn), lambda i, j: (0, j)),
            tab_spec, tab_spec, tab_spec,
        ],
        out_specs=pl.BlockSpec((tm, tn), lambda i, j: (i, j)),
        out_shape=jax.ShapeDtypeStruct((m, n), BF16),
        scratch_shapes=[pltpu.VMEM((tm, d), BF16)],
        compiler_params=pltpu.CompilerParams(
            dimension_semantics=("parallel", "arbitrary"),
            vmem_limit_bytes=VMEM_LIMIT_BYTES),
        name="l0_norm_proj",
    )(x, g, w, cos_t, sa_t, sb_t)


def _lambda_full(lq1_ref, lk1_ref, lq2_ref, lk2_ref):
    a = jnp.sum(lq1_ref[...] * lk1_ref[...], axis=-1, keepdims=True)
    b = jnp.sum(lq2_ref[...] * lk2_ref[...], axis=-1, keepdims=True)
    return jnp.exp(a) - jnp.exp(b) + LAMBDA_INIT_L0


def _diff_finalize(o1, o2, lam, subg, gate):
    o = o1 - lam * o2
    o = _rms(o, subg) * (1.0 - LAMBDA_INIT_L0)
    return (o * gate.astype(F32)).astype(BF16)


def _attn_kernel(q_ref, k_ref, v_ref, gate_ref, km_ref, vm_ref, lq1_ref, lk1_ref, lq2_ref,
                 lk2_ref, subg_ref, o_ref, m_ref, l_ref, acc_ref, *, tq, tk, dh, n_sub,
                 unroll):
    i = pl.program_id(2)
    ts = tq // n_sub
    dv = acc_ref.shape[2]

    def scores(c, kb):
        q = q_ref[c * ts:(c + 1) * ts, :]
        return jnp.concatenate([_dot_nt(q[:, :dh], kb[:, :dh]), _dot_nt(q[:, dh:], kb[:, dh:])],
                               axis=0)

    lane0 = lax.broadcasted_iota(jnp.int32, (2 * ts, LANES), 1) == 0
    for c in range(n_sub):
        s = scores(c, km_ref[...])
        m0 = jnp.max(s, axis=-1, keepdims=True)
        p = jnp.exp2(s - m0)
        m_ref[c] = jnp.broadcast_to(m0, (2 * ts, LANES))
        l_ref[c] = jnp.where(lane0, jnp.sum(p, axis=-1, keepdims=True), 0.0)
        acc_ref[c] = _dot(p.astype(BF16), vm_ref[...])

    def update(c, s, vb):
        m_prev = m_ref[c]
        m_new = jnp.maximum(m_prev, jnp.max(s, axis=-1, keepdims=True))
        alpha = jnp.exp2(m_prev - m_new)
        p = jnp.exp2(s - jnp.tile(m_new, (1, s.shape[1] // LANES)))
        psum = p[:, 0:LANES]
        for blk in range(1, s.shape[1] // LANES):
            psum = psum + p[:, blk * LANES:(blk + 1) * LANES]
        l_ref[c] = alpha * l_ref[c] + psum
        m_ref[c] = m_new
        acc_ref[c] = (acc_ref[c] * jnp.tile(alpha, (1, dv // LANES))
                      + _dot(p.astype(BF16), vb))

    def full_blocks(jb, carry):
        for u in range(unroll):
            off = pl.multiple_of((jb * unroll + u) * tk, tk)
            kb = k_ref[pl.ds(off, tk), :]
            vb = v_ref[pl.ds(off, tk), :]
            for c in range(n_sub):
                update(c, scores(c, kb), vb)
        return carry

    lax.fori_loop(0, i * (tq // tk // unroll), full_blocks, 0)

    row = lax.broadcasted_iota(jnp.int32, (2 * ts, tk), 0) % ts
    col = lax.broadcasted_iota(jnp.int32, (2 * ts, tk), 1)
    for d in range(tq // tk):
        off = pl.multiple_of(i * tq + d * tk, tk)
        kb = k_ref[pl.ds(off, tk), :]
        vb = v_ref[pl.ds(off, tk), :]
        for c in range(n_sub):
            if d * tk >= (c + 1) * ts:
                continue
            s = scores(c, kb)
            if (d + 1) * tk > c * ts:
                visible = (col + d * tk) // CHUNK <= (row + c * ts) // CHUNK
                s = jnp.where(visible, s, NEG_INF)
            update(c, s, vb)

    lam = _lambda_full(lq1_ref, lk1_ref, lq2_ref, lk2_ref)
    for c in range(n_sub):
        inv_l = 1.0 / jnp.sum(l_ref[c], axis=-1, keepdims=True)
        acc = acc_ref[c]
        o_ref[c * ts:(c + 1) * ts, :] = _diff_finalize(
            acc[:ts] * inv_l[:ts], acc[ts:] * inv_l[ts:], lam, subg_ref[...],
            gate_ref[c * ts:(c + 1) * ts, :])


def _attention(qkvg, qkvg_meta, lq1, lk1, lq2, lk2, subg, *, batch, seq, d_model):
    dv = d_model // DA_HEADS
    dh = dv // 2
    tq, tk, n_sub = ATTN_TQ, ATTN_TK, ATTN_SUBTILES
    ts = tq // n_sub
    unroll = ATTN_KEY_UNROLL
    assert ts % CHUNK == 0 and tk % CHUNK == 0 and tq % (tk * unroll) == 0
    nq = seq // tq
    h_ = DA_HEADS
    kern = functools.partial(_attn_kernel, tq=tq, tk=tk, dh=dh, n_sub=n_sub, unroll=unroll)
    vec = lambda n: pl.BlockSpec((1, n), lambda b, h, i: (0, 0))
    return pl.pallas_call(
        kern,
        grid=(batch, h_, nq),
        in_specs=[
            pl.BlockSpec((tq, dv), lambda b, h, i: (b * nq + i, h)),
            pl.BlockSpec((seq, dv), lambda b, h, i: (b, h_ + h)),
            pl.BlockSpec((seq, dv), lambda b, h, i: (b, 2 * h_ + h)),
            pl.BlockSpec((tq, dv), lambda b, h, i: (b * nq + i, 3 * h_ + h)),
            pl.BlockSpec((N_META, dv), lambda b, h, i: (0, h_ + h)),
            pl.BlockSpec((N_META, dv), lambda b, h, i: (0, 2 * h_ + h)),
            vec(dh), vec(dh), vec(dh), vec(dh), vec(dv),
        ],
        out_specs=pl.BlockSpec((tq, dv), lambda b, h, i: (b * nq + i, h)),
        out_shape=jax.ShapeDtypeStruct((batch * seq, d_model), BF16),
        scratch_shapes=[pltpu.VMEM((n_sub, 2 * ts, LANES), F32),
                        pltpu.VMEM((n_sub, 2 * ts, LANES), F32),
                        pltpu.VMEM((n_sub, 2 * ts, dv), F32)],
        compiler_params=pltpu.CompilerParams(
            dimension_semantics=("parallel", "parallel", "arbitrary"),
            vmem_limit_bytes=VMEM_LIMIT_BYTES),
        name="l0_diff_attn",
    )(qkvg, qkvg, qkvg, qkvg, qkvg_meta, qkvg_meta, lq1, lk1, lq2, lk2, subg)


def _meta_attn_kernel(q_ref, k_ref, v_ref, gate_ref, lq1_ref, lk1_ref, lq2_ref, lk2_ref,
                      subg_ref, o_ref, *, dh):
    q = q_ref[...]
    k = k_ref[...]
    v = v_ref[...]

    def attend(qm, km):
        s = _dot_nt(qm, km)
        p = jnp.exp2(s - jnp.max(s, axis=-1, keepdims=True))
        return _dot(p.astype(BF16), v) * (1.0 / jnp.sum(p, axis=-1, keepdims=True))

    lam = _lambda_full(lq1_ref, lk1_ref, lq2_ref, lk2_ref)
    o_ref[...] = _diff_finalize(attend(q[:, :dh], k[:, :dh]), attend(q[:, dh:], k[:, dh:]),
                                lam, subg_ref[...], gate_ref[...])


def _meta_attention(qkvg_meta, lq1, lk1, lq2, lk2, subg, *, d_model):
    dv = d_model // DA_HEADS
    dh = dv // 2
    h_ = DA_HEADS
    vec = lambda n: pl.BlockSpec((1, n), lambda h: (0, 0))
    blk = lambda part: pl.BlockSpec((N_META, dv), lambda h: (0, part * h_ + h))
    return pl.pallas_call(
        functools.partial(_meta_attn_kernel, dh=dh),
        grid=(h_,),
        in_specs=[blk(0), blk(1), blk(2), blk(3), vec(dh), vec(dh), vec(dh), vec(dh), vec(dv)],
        out_specs=pl.BlockSpec((N_META, dv), lambda h: (0, h)),
        out_shape=jax.ShapeDtypeStruct((N_META, d_model), BF16),
        compiler_params=pltpu.CompilerParams(dimension_semantics=("parallel",)),
        name="l0_meta_attn",
    )(qkvg_meta, qkvg_meta, qkvg_meta, qkvg_meta, lq1, lk1, lq2, lk2, subg)


def _out_proj_kernel(a_ref, w_ref, x_ref, g_ref, o_ref):
    y = _dot(a_ref[...], w_ref[...])
    o_ref[...] = x_ref[...] + _rms(y, g_ref[...])


def _out_proj(a, w, x, g, *, tm):
    m, d = x.shape
    return pl.pallas_call(
        _out_proj_kernel,
        grid=(m // tm,),
        in_specs=[
            pl.BlockSpec((tm, d), lambda i: (i, 0)),
            pl.BlockSpec((d, d), lambda i: (0, 0), pipeline_mode=pl.Buffered(1)),
            pl.BlockSpec((tm, d), lambda i: (i, 0)),
            pl.BlockSpec((1, d), lambda i: (0, 0)),
        ],
        out_specs=pl.BlockSpec((tm, d), lambda i: (i, 0)),
        out_shape=jax.ShapeDtypeStruct((m, d), F32),
        compiler_params=pltpu.CompilerParams(
            dimension_semantics=("parallel",), vmem_limit_bytes=VMEM_LIMIT_BYTES),
        name="l0_out_proj",
    )(a, w, x, g)


def _pool_layer_kernel(h_ref, hm_ref, pre_g_ref, w_in_ref, w_grp_ref, scale_ref, w_out_ref,
                       post_g_ref, o_ref, u_ext_ref, z_ref, *, tm, d, halo):
    i = pl.program_id(1)
    pre_g = pre_g_ref[...]

    @pl.when(i == 0)
    def _():
        hmn = _rms(hm_ref[...], pre_g).astype(BF16)
        u_ext_ref[0:halo, :] = _dot(hmn, w_in_ref[:, 0:d])

    @pl.when(i > 0)
    def _():
        u_ext_ref[0:halo, :] = u_ext_ref[tm:tm + halo, :]

    h = h_ref[...]
    hn = _rms(h, pre_g).astype(BF16)
    u_ext_ref[halo:halo + tm, :] = _dot(hn, w_in_ref[:, 0:d])

    gsz = d // len(POOL_WINDOWS)
    for gi, w in enumerate(POOL_WINDOWS):
        c0 = gi * gsz
        u = u_ext_ref[halo:halo + tm, c0:c0 + gsz]
        s = u
        for back in range(1, w):
            s = s + u_ext_ref[halo - back:halo - back + tm, c0:c0 + gsz]
        mixed = (s * (1.0 / w) - u).astype(BF16)
        mo = _dot(mixed, w_grp_ref[gi]) * scale_ref[:, c0:c0 + gsz]
        gate = _dot(hn, w_in_ref[:, d + c0:d + c0 + gsz])
        z_ref[:, c0:c0 + gsz] = (mo * _silu(gate)).astype(BF16)

    y = _dot(z_ref[...], w_out_ref[...])
    o_ref[...] = h + _rms(y, post_g_ref[...])


def _pool_layer(h, h_meta, pre_g, w_in, w_grp, scale, w_out, post_g, *, batch, seq, tm):
    m, d = h.shape
    nt = seq // tm
    halo = N_META
    assert max(POOL_WINDOWS) <= halo and halo <= tm
    const2 = lambda b, i: (0, 0)
    resident = lambda shape, imap: pl.BlockSpec(shape, imap, pipeline_mode=pl.Buffered(1))
    kern = functools.partial(_pool_layer_kernel, tm=tm, d=d, halo=halo)
    return pl.pallas_call(
        kern,
        grid=(batch, nt),
        in_specs=[
            pl.BlockSpec((tm, d), lambda b, i: (b * nt + i, 0)),
            resident((N_META, d), const2),
            pl.BlockSpec((1, d), const2),
            resident((d, 2 * d), const2),
            resident(w_grp.shape, lambda b, i: (0, 0, 0)),
            pl.BlockSpec((1, d), const2),
            resident((d, d), const2),
            pl.BlockSpec((1, d), const2),
        ],
        out_specs=pl.BlockSpec((tm, d), lambda b, i: (b * nt + i, 0)),
        out_shape=jax.ShapeDtypeStruct((m, d), F32),
        scratch_shapes=[pltpu.VMEM((halo + tm, d), F32), pltpu.VMEM((tm, d), BF16)],
        compiler_params=pltpu.CompilerParams(
            dimension_semantics=("arbitrary", "arbitrary"),
            vmem_limit_bytes=VMEM_LIMIT_BYTES),
        name="l1_pool_layer",
    )(h, h_meta, pre_g, w_in, w_grp, scale, w_out, post_g)


def _rope_tables(length, rope_dim):
    half = rope_dim // 2
    pos = jnp.arange(length, dtype=F32)
    inv = ROPE_THETA ** (-jnp.arange(0, rope_dim, 2, dtype=F32) / rope_dim)
    ang = pos[:, None] * inv[None, :]
    cos, sin = jnp.cos(ang), jnp.sin(ang)
    zeros = lambda n: jnp.zeros((length, n), F32)
    cos_t = jnp.concatenate([cos, cos, jnp.ones((length, LANES - rope_dim), F32)], axis=1)
    sa_t = jnp.concatenate([zeros(half), sin, zeros(LANES - rope_dim)], axis=1)
    sb_t = jnp.concatenate([-sin, zeros(LANES - half)], axis=1)
    return cos_t, sa_t, sb_t


def kernel(x, meta_tokens, pre_norm_g, post_norm_g, attn_w_in, attn_w_out, attn_lambda_q1,
           attn_lambda_k1, attn_lambda_q2, attn_lambda_k2, attn_subln_g, pool_w_in,
           pool_w_group, pool_scale, pool_w_out):
    batch, seq, d = x.shape
    dv = d // DA_HEADS
    dh = dv // 2
    assert dh == LANES and seq % CHUNK == 0
    xf = x.reshape(batch * seq, d)
    meta = meta_tokens.astype(x.dtype)

    cos_t, sa_t, sb_t = _rope_tables(N_META + seq, dh // 4)
    w_in0 = attn_w_in[0].astype(BF16)
    w_out0 = attn_w_out[0].astype(BF16)
    pre_g0, pre_g1 = pre_norm_g[0:1], pre_norm_g[1:2]
    post_g0, post_g1 = post_norm_g[0:1], post_norm_g[1:2]
    q_scale = float(dh) ** -0.5 * math.log2(math.e)
    lam = [p[0:1] for p in (attn_lambda_q1, attn_lambda_k1, attn_lambda_q2, attn_lambda_k2)]
    subg = attn_subln_g[0:1]

    proj = functools.partial(_norm_proj, tn=PROJ_TN, q_scale=q_scale)
    qkvg_meta = proj(meta, pre_g0, w_in0, cos_t[:N_META], sa_t[:N_META], sb_t[:N_META],
                     tm=N_META, rows_per_seq=N_META)
    qkvg = proj(xf, pre_g0, w_in0, cos_t[N_META:], sa_t[N_META:], sb_t[N_META:],
                tm=PROJ_TM, rows_per_seq=seq)
    a_meta = _meta_attention(qkvg_meta, *lam, subg, d_model=d)
    a = _attention(qkvg, qkvg_meta, *lam, subg, batch=batch, seq=seq, d_model=d)
    h1_meta = _out_proj(a_meta, w_out0, meta, post_g0, tm=N_META)
    h1 = _out_proj(a, w_out0, xf, post_g0, tm=OUT_TM)

    out = _pool_layer(h1, h1_meta, pre_g1, pool_w_in[0].astype(BF16),
                      pool_w_group[0].astype(BF16), pool_scale[0:1],
                      pool_w_out[0].astype(BF16), post_g1, batch=batch, seq=seq, tm=POOL_TM)
    return out.reshape(batch, seq, d)
```

```python
import functools
import math

import jax
import jax.numpy as jnp
from jax import lax
from jax.experimental import pallas as pl
from jax.experimental.pallas import tpu as pltpu

F32 = jnp.float32
BF16 = jnp.bfloat16

N_META = 16
META_PAD = 128
CHUNK = 64
RMS_EPS = 1e-6
DA_HEADS = 8
ROPE_THETA = 500000.0
POOL_WINDOWS = (2, 4, 8, 16)
LAMBDA_INIT_L0 = 0.8 - 0.6 * math.exp(-0.3 * 0)

LANES = 128
VMEM_LIMIT_BYTES = 56 * 1024 * 1024

PROJ_TM = 1024
PROJ_TN = 2048
PROJ_SUB_ROWS = 512
PROJ_SUB_COLS = 256
ATTN_TQ = 1024
ATTN_TK = 256
ATTN_SUBTILES = 4
ATTN_KEY_UNROLL = 4
OUT_TM = 1024
OUT_SUB_ROWS = 256
POOL_TM = 256

NEG_INF = float(jnp.finfo(jnp.float32).min)


def _rms(x, g):
    ms = jnp.mean(x * x, axis=-1, keepdims=True)
    return x * lax.rsqrt(ms + RMS_EPS) * g


def _silu(g):
    return g * (1.0 / (1.0 + jnp.exp(-g)))


def _dot(a, b):
    return jnp.dot(a, b, preferred_element_type=F32)


def _dot_nt(a, b):
    return lax.dot_general(a, b, (((1,), (1,)), ((), ())), preferred_element_type=F32)


def _norm_proj_kernel(x_ref, g_ref, w_ref, cos_ref, sa_ref, sb_ref, o_ref, hn_ref, *,
                      n_col_tiles_per_part, q_scale, sub_rows, sub_cols):
    j = pl.program_id(1)
    npt = n_col_tiles_per_part

    @pl.when(j == 0)
    def _():
        hn_ref[...] = _rms(x_ref[...], g_ref[...]).astype(BF16)

    tm, tn = o_ref.shape
    mc, nc = min(tm, sub_rows), min(tn, sub_cols)
    is_rope = j < 2 * npt
    is_gate = j >= 3 * npt
    scale = jnp.where(j < npt, q_scale, 1.0).astype(F32)
    half = 16
    for mi in range(tm // mc):
        rows = slice(mi * mc, (mi + 1) * mc)
        c = jnp.where(is_rope, cos_ref[rows, :] * scale, 1.0)
        sa = jnp.where(is_rope, sa_ref[rows, :] * scale, 0.0)
        sb = jnp.where(is_rope, sb_ref[rows, :] * scale, 0.0)
        for ni in range(tn // nc):
            acc = _dot(hn_ref[rows, :], w_ref[:, ni * nc:(ni + 1) * nc])
            for blk in range(nc // LANES):
                t = acc[:, blk * LANES:(blk + 1) * LANES]
                r = t * c + pltpu.roll(t, half, 1) * sa + pltpu.roll(t, LANES - half, 1) * sb
                f = jnp.where(is_gate, 1.0 / (1.0 + jnp.exp(-t)), 1.0)
                col = ni * nc + blk * LANES
                o_ref[rows, col:col + LANES] = (r * f).astype(o_ref.dtype)


def _norm_proj(x, g, w, cos_t, sa_t, sb_t, *, tm, tn, rows_per_seq, q_scale):
    m, d = x.shape
    n = w.shape[1]
    tiles_per_seq = rows_per_seq // tm
    kern = functools.partial(_norm_proj_kernel, n_col_tiles_per_part=(n // 4) // tn,
                             q_scale=q_scale, sub_rows=PROJ_SUB_ROWS, sub_cols=PROJ_SUB_COLS)
    tab_spec = pl.BlockSpec((tm, LANES), lambda i, j: (i % tiles_per_seq, 0))
    return pl.pallas_call(
        kern,
        grid=(m // tm, n // tn),
        in_specs=[
            pl.BlockSpec((tm, d), lambda i, j: (i, 0)),
            pl.BlockSpec((1, d), lambda i, j: (0, 0)),
            pl.BlockSpec((d, tn), lambda i, j: (0, j)),
            tab_spec, tab_spec, tab_spec,
        ],
        out_specs=pl.BlockSpec((tm, tn), lambda i, j: (i, j)),
        out_shape=jax.ShapeDtypeStruct((m, n), BF16),
        scratch_shapes=[pltpu.VMEM((tm, d), BF16)],
        compiler_params=pltpu.CompilerParams(
            dimension_semantics=("parallel", "arbitrary"),
            vmem_limit_bytes=VMEM_LIMIT_BYTES),
        name="l0_norm_proj",
    )(x, g, w, cos_t, sa_t, sb_t)


def _lambda_full(lq1_ref, lk1_ref, lq2_ref, lk2_ref):
    a = jnp.sum(lq1_ref[...] * lk1_ref[...], axis=-1, keepdims=True)
    b = jnp.sum(lq2_ref[...] * lk2_ref[...], axis=-1, keepdims=True)
    return jnp.exp(a) - jnp.exp(b) + LAMBDA_INIT_L0


def _diff_finalize(o1, o2, lam, subg, gate):
    o = o1 - lam * o2
    o = _rms(o, subg) * (1.0 - LAMBDA_INIT_L0)
    return (o * gate.astype(F32)).astype(BF16)


def _attn_kernel(q_ref, k_ref, v_ref, gate_ref, km_ref, vm_ref, bias_ref, lq1_ref, lk1_ref,
                 lq2_ref, lk2_ref, subg_ref, o_ref, m_ref, l_ref, acc_ref, *, tq, tk, dh,
                 n_sub, unroll):
    i = pl.program_id(2)
    ts = tq // n_sub
    dv = acc_ref.shape[2]

    def scores(c, kb):
        q = q_ref[c * ts:(c + 1) * ts, :]
        return jnp.concatenate([_dot_nt(q[:, :dh], kb[:, :dh]), _dot_nt(q[:, dh:], kb[:, dh:])],
                               axis=0)

    def lane_sum(p):
        psum = p[:, 0:LANES]
        for blk in range(1, p.shape[1] // LANES):
            psum = psum + p[:, blk * LANES:(blk + 1) * LANES]
        return psum

    for c in range(n_sub):
        off = pl.multiple_of(i * tq + c * ts, ts)
        s = jnp.concatenate([scores(c, k_ref[pl.ds(off, ts), :]), scores(c, km_ref[...])],
                            axis=1) + bias_ref[...]
        m0 = jnp.max(s, axis=-1, keepdims=True)
        p = jnp.exp2(s - m0)
        m_ref[c] = jnp.broadcast_to(m0, (2 * ts, LANES))
        l_ref[c] = lane_sum(p)
        pb = p.astype(BF16)
        acc_ref[c] = _dot(pb[:, :ts], v_ref[pl.ds(off, ts), :]) + _dot(pb[:, ts:], vm_ref[...])

    def update(c, s, vb):
        m_prev = m_ref[c]
        m_new = jnp.maximum(m_prev, jnp.max(s, axis=-1, keepdims=True))
        alpha = jnp.exp2(m_prev - m_new)
        p = jnp.exp2(s - jnp.tile(m_new, (1, s.shape[1] // LANES)))
        l_ref[c] = alpha * l_ref[c] + lane_sum(p)
        m_ref[c] = m_new
        acc_ref[c] = (acc_ref[c] * jnp.tile(alpha, (1, dv // LANES))
                      + _dot(p.astype(BF16), vb))

    def full_blocks(jb, carry):
        for u in range(unroll):
            off = pl.multiple_of((jb * unroll + u) * tk, tk)
            kb = k_ref[pl.ds(off, tk), :]
            vb = v_ref[pl.ds(off, tk), :]
            for c in range(n_sub):
                update(c, scores(c, kb), vb)
        return carry

    for d in range(n_sub - 1):
        off = pl.multiple_of(i * tq + d * tk, tk)
        kb = k_ref[pl.ds(off, tk), :]
        vb = v_ref[pl.ds(off, tk), :]
        for c in range(d + 1, n_sub):
            update(c, scores(c, kb), vb)

    lax.fori_loop(0, i * (tq // tk // unroll), full_blocks, 0)

    lam = _lambda_full(lq1_ref, lk1_ref, lq2_ref, lk2_ref)
    for c in range(n_sub):
        inv_l = 1.0 / jnp.sum(l_ref[c], axis=-1, keepdims=True)
        acc = acc_ref[c]
        o_ref[c * ts:(c + 1) * ts, :] = _diff_finalize(
            acc[:ts] * inv_l[:ts], acc[ts:] * inv_l[ts:], lam, subg_ref[...],
            gate_ref[c * ts:(c + 1) * ts, :])


def _attention(qkvg, qkvg_meta, lq1, lk1, lq2, lk2, subg, *, batch, seq, d_model):
    dv = d_model // DA_HEADS
    dh = dv // 2
    tq, tk, n_sub = ATTN_TQ, ATTN_TK, ATTN_SUBTILES
    ts = tq // n_sub
    unroll = ATTN_KEY_UNROLL
    assert ts % CHUNK == 0 and tk == ts and tq % (tk * unroll) == 0
    nq = seq // tq
    h_ = DA_HEADS
    kern = functools.partial(_attn_kernel, tq=tq, tk=tk, dh=dh, n_sub=n_sub, unroll=unroll)
    vec = lambda n: pl.BlockSpec((1, n), lambda b, h, i: (0, 0))

    row_chunk = (jnp.arange(2 * ts)[:, None] % ts) // CHUNK
    key = jnp.arange(ts + META_PAD)[None, :]
    visible = jnp.where(key < ts, key // CHUNK <= row_chunk, key - ts < N_META)
    bias = jnp.where(visible, 0.0, NEG_INF).astype(F32)
    return pl.pallas_call(
        kern,
        grid=(batch, h_, nq),
        in_specs=[
            pl.BlockSpec((tq, dv), lambda b, h, i: (b * nq + i, h)),
            pl.BlockSpec((seq, dv), lambda b, h, i: (b, h_ + h)),
            pl.BlockSpec((seq, dv), lambda b, h, i: (b, 2 * h_ + h)),
            pl.BlockSpec((tq, dv), lambda b, h, i: (b * nq + i, 3 * h_ + h)),
            pl.BlockSpec((META_PAD, dv), lambda b, h, i: (0, h_ + h)),
            pl.BlockSpec((META_PAD, dv), lambda b, h, i: (0, 2 * h_ + h)),
            pl.BlockSpec(bias.shape, lambda b, h, i: (0, 0)),
            vec(dh), vec(dh), vec(dh), vec(dh), vec(dv),
        ],
        out_specs=pl.BlockSpec((tq, dv), lambda b, h, i: (b * nq + i, h)),
        out_shape=jax.ShapeDtypeStruct((batch * seq, d_model), BF16),
        scratch_shapes=[pltpu.VMEM((n_sub, 2 * ts, LANES), F32),
                        pltpu.VMEM((n_sub, 2 * ts, LANES), F32),
                        pltpu.VMEM((n_sub, 2 * ts, dv), F32)],
        compiler_params=pltpu.CompilerParams(
            dimension_semantics=("parallel", "parallel", "arbitrary"),
            vmem_limit_bytes=VMEM_LIMIT_BYTES),
        name="l0_diff_attn",
    )(qkvg, qkvg, qkvg, qkvg, qkvg_meta, qkvg_meta, bias, lq1, lk1, lq2, lk2, subg)


def _meta_attn_kernel(q_ref, k_ref, v_ref, gate_ref, lq1_ref, lk1_ref, lq2_ref, lk2_ref,
                      subg_ref, o_ref, *, dh):
    q = q_ref[...]
    k = k_ref[...]
    v = v_ref[...]

    def attend(qm, km):
        s = _dot_nt(qm, km)
        p = jnp.exp2(s - jnp.max(s, axis=-1, keepdims=True))
        return _dot(p.astype(BF16), v) * (1.0 / jnp.sum(p, axis=-1, keepdims=True))

    lam = _lambda_full(lq1_ref, lk1_ref, lq2_ref, lk2_ref)
    o_ref[...] = _diff_finalize(attend(q[:, :dh], k[:, :dh]), attend(q[:, dh:], k[:, dh:]),
                                lam, subg_ref[...], gate_ref[...])


def _meta_attention(qkvg_meta, lq1, lk1, lq2, lk2, subg, *, d_model):
    dv = d_model // DA_HEADS
    dh = dv // 2
    h_ = DA_HEADS
    vec = lambda n: pl.BlockSpec((1, n), lambda h: (0, 0))
    blk = lambda part: pl.BlockSpec((N_META, dv), lambda h: (0, part * h_ + h))
    return pl.pallas_call(
        functools.partial(_meta_attn_kernel, dh=dh),
        grid=(h_,),
        in_specs=[blk(0), blk(1), blk(2), blk(3), vec(dh), vec(dh), vec(dh), vec(dh), vec(dv)],
        out_specs=pl.BlockSpec((N_META, dv), lambda h: (0, h)),
        out_shape=jax.ShapeDtypeStruct((N_META, d_model), BF16),
        compiler_params=pltpu.CompilerParams(dimension_semantics=("parallel",)),
        name="l0_meta_attn",
    )(qkvg_meta, qkvg_meta, qkvg_meta, qkvg_meta, lq1, lk1, lq2, lk2, subg)


def _out_proj_kernel(a_ref, w_ref, x_ref, g_ref, o_ref, *, sub_rows):
    tm = o_ref.shape[0]
    mc = min(tm, sub_rows)
    for mi in range(tm // mc):
        rows = slice(mi * mc, (mi + 1) * mc)
        y = _dot(a_ref[rows, :], w_ref[...])
        o_ref[rows, :] = x_ref[rows, :] + _rms(y, g_ref[...])


def _out_proj(a, w, x, g, *, tm):
    m, d = x.shape
    return pl.pallas_call(
        functools.partial(_out_proj_kernel, sub_rows=OUT_SUB_ROWS),
        grid=(m // tm,),
        in_specs=[
            pl.BlockSpec((tm, d), lambda i: (i, 0)),
            pl.BlockSpec((d, d), lambda i: (0, 0), pipeline_mode=pl.Buffered(1)),
            pl.BlockSpec((tm, d), lambda i: (i, 0)),
            pl.BlockSpec((1, d), lambda i: (0, 0)),
        ],
        out_specs=pl.BlockSpec((tm, d), lambda i: (i, 0)),
        out_shape=jax.ShapeDtypeStruct((m, d), F32),
        compiler_params=pltpu.CompilerParams(
            dimension_semantics=("parallel",), vmem_limit_bytes=VMEM_LIMIT_BYTES),
        name="l0_out_proj",
    )(a, w, x, g)


def _pool_layer_kernel(h_ref, hm_ref, pre_g_ref, w_in_ref, w_grp_ref, scale_ref, w_out_ref,
                       post_g_ref, o_ref, u_ext_ref, z_ref, *, tm, d, halo):
    i = pl.program_id(1)
    pre_g = pre_g_ref[...]

    @pl.when(i == 0)
    def _():
        hmn = _rms(hm_ref[...], pre_g).astype(BF16)
        u_ext_ref[0:halo, :] = _dot(hmn, w_in_ref[:, 0:d])

    @pl.when(i > 0)
    def _():
        u_ext_ref[0:halo, :] = u_ext_ref[tm:tm + halo, :]

    h = h_ref[...]
    hn = _rms(h, pre_g).astype(BF16)
    u_ext_ref[halo:halo + tm, :] = _dot(hn, w_in_ref[:, 0:d])

    gsz = d // len(POOL_WINDOWS)
    for gi, w in enumerate(POOL_WINDOWS):
        c0 = gi * gsz
        u = u_ext_ref[halo:halo + tm, c0:c0 + gsz]
        s = u
        for back in range(1, w):
            s = s + u_ext_ref[halo - back:halo - back + tm, c0:c0 + gsz]
        mixed = (s * (1.0 / w) - u).astype(BF16)
        mo = _dot(mixed, w_grp_ref[gi]) * scale_ref[:, c0:c0 + gsz]
        gate = _dot(hn, w_in_ref[:, d + c0:d + c0 + gsz])
        z_ref[:, c0:c0 + gsz] = (mo * _silu(gate)).astype(BF16)

    y = _dot(z_ref[...], w_out_ref[...])
    o_ref[...] = h + _rms(y, post_g_ref[...])


def _pool_layer(h, h_meta, pre_g, w_in, w_grp, scale, w_out, post_g, *, batch, seq, tm):
    m, d = h.shape
    nt = seq // tm
    halo = N_META
    assert max(POOL_WINDOWS) <= halo and halo <= tm
    const2 = lambda b, i: (0, 0)
    resident = lambda shape, imap: pl.BlockSpec(shape, imap, pipeline_mode=pl.Buffered(1))
    kern = functools.partial(_pool_layer_kernel, tm=tm, d=d, halo=halo)
    return pl.pallas_call(
        kern,
        grid=(batch, nt),
        in_specs=[
            pl.BlockSpec((tm, d), lambda b, i: (b * nt + i, 0)),
            resident((N_META, d), const2),
            pl.BlockSpec((1, d), const2),
            resident((d, 2 * d), const2),
            resident(w_grp.shape, lambda b, i: (0, 0, 0)),
            pl.BlockSpec((1, d), const2),
            resident((d, d), const2),
            pl.BlockSpec((1, d), const2),
        ],
        out_specs=pl.BlockSpec((tm, d), lambda b, i: (b * nt + i, 0)),
        out_shape=jax.ShapeDtypeStruct((m, d), F32),
        scratch_shapes=[pltpu.VMEM((halo + tm, d), F32), pltpu.VMEM((tm, d), BF16)],
        compiler_params=pltpu.CompilerParams(
            dimension_semantics=("arbitrary", "arbitrary"),
            vmem_limit_bytes=VMEM_LIMIT_BYTES),
        name="l1_pool_layer",
    )(h, h_meta, pre_g, w_in, w_grp, scale, w_out, post_g)


def _rope_tables(length, rope_dim):
    half = rope_dim // 2
    pos = jnp.arange(length, dtype=F32)
    inv = ROPE_THETA ** (-jnp.arange(0, rope_dim, 2, dtype=F32) / rope_dim)
    ang = pos[:, None] * inv[None, :]
    cos, sin = jnp.cos(ang), jnp.sin(ang)
    zeros = lambda n: jnp.zeros((length, n), F32)
    cos_t = jnp.concatenate([cos, cos, jnp.ones((length, LANES - rope_dim), F32)], axis=1)
    sa_t = jnp.concatenate([zeros(half), sin, zeros(LANES - rope_dim)], axis=1)
    sb_t = jnp.concatenate([-sin, zeros(LANES - half)], axis=1)
    return cos_t, sa_t, sb_t


def kernel(x, meta_tokens, pre_norm_g, post_norm_g, attn_w_in, attn_w_out, attn_lambda_q1,
           attn_lambda_k1, attn_lambda_q2, attn_lambda_k2, attn_subln_g, pool_w_in,
           pool_w_group, pool_scale, pool_w_out):
    batch, seq, d = x.shape
    dv = d // DA_HEADS
    dh = dv // 2
    assert dh == LANES and seq % CHUNK == 0
    xf = x.reshape(batch * seq, d)
    meta = meta_tokens.astype(x.dtype)

    cos_t, sa_t, sb_t = _rope_tables(N_META + seq, dh // 4)
    w_in0 = attn_w_in[0].astype(BF16)
    w_out0 = attn_w_out[0].astype(BF16)
    pre_g0, pre_g1 = pre_norm_g[0:1], pre_norm_g[1:2]
    post_g0, post_g1 = post_norm_g[0:1], post_norm_g[1:2]
    q_scale = float(dh) ** -0.5 * math.log2(math.e)
    lam = [p[0:1] for p in (attn_lambda_q1, attn_lambda_k1, attn_lambda_q2, attn_lambda_k2)]
    subg = attn_subln_g[0:1]

    proj = functools.partial(_norm_proj, tn=PROJ_TN, q_scale=q_scale)
    meta_pad = jnp.pad(meta, ((0, META_PAD - N_META), (0, 0)))
    qkvg_meta = proj(meta_pad, pre_g0, w_in0, cos_t[:META_PAD], sa_t[:META_PAD],
                     sb_t[:META_PAD], tm=META_PAD, rows_per_seq=META_PAD)
    qkvg = proj(xf, pre_g0, w_in0, cos_t[N_META:], sa_t[N_META:], sb_t[N_META:],
                tm=PROJ_TM, rows_per_seq=seq)
    a_meta = _meta_attention(qkvg_meta, *lam, subg, d_model=d)
    a = _attention(qkvg, qkvg_meta, *lam, subg, batch=batch, seq=seq, d_model=d)
    h1_meta = _out_proj(a_meta, w_out0, meta, post_g0, tm=N_META)
    h1 = _out_proj(a, w_out0, xf, post_g0, tm=OUT_TM)

    out = _pool_layer(h1, h1_meta, pre_g1, pool_w_in[0].astype(BF16),
                      pool_w_group[0].astype(BF16), pool_scale[0:1],
                      pool_w_out[0].astype(BF16), post_g1, batch=batch, seq=seq, tm=POOL_TM)
    return out.reshape(batch, seq, d)
```

```python
import functools
import math

import jax
import jax.numpy as jnp
from jax import lax
from jax.experimental import pallas as pl
from jax.experimental.pallas import tpu as pltpu

F32 = jnp.float32
BF16 = jnp.bfloat16

N_META = 16
CHUNK = 64
RMS_EPS = 1e-6
DA_HEADS = 8
ROPE_THETA = 500000.0
POOL_WINDOWS = (2, 4, 8, 16)
LAMBDA_INIT_L0 = 0.8 - 0.6 * math.exp(-0.3 * 0)

LANES = 128
VMEM_LIMIT_BYTES = 56 * 1024 * 1024

PROJ_TM = 1024
PROJ_TN = 2048
PROJ_SUB_ROWS = 512
PROJ_SUB_COLS = 256
ATTN_TQ = 1024
ATTN_TK = 256
ATTN_SUBTILES = 4
ATTN_KEY_UNROLL = 4
OUT_SUB_ROWS = 256
POOL_TM = 256

NEG_INF = float(jnp.finfo(jnp.float32).min)


def _rms(x, g):
    ms = jnp.mean(x * x, axis=-1, keepdims=True)
    return x * lax.rsqrt(ms + RMS_EPS) * g


def _silu(g):
    return g * (1.0 / (1.0 + jnp.exp(-g)))


def _dot(a, b):
    return jnp.dot(a, b, preferred_element_type=F32)


def _dot_nt(a, b):
    return lax.dot_general(a, b, (((1,), (1,)), ((), ())), preferred_element_type=F32)


def _norm_proj_kernel(x_ref, g_ref, w_ref, cos_ref, sa_ref, sb_ref, o_ref, hn_ref, *,
                      n_col_tiles_per_part, q_scale, sub_rows, sub_cols):
    j = pl.program_id(1)
    npt = n_col_tiles_per_part

    @pl.when(j == 0)
    def _():
        hn_ref[...] = _rms(x_ref[...], g_ref[...]).astype(BF16)

    tm, tn = o_ref.shape
    mc, nc = min(tm, sub_rows), min(tn, sub_cols)
    is_rope = j < 2 * npt
    is_gate = j >= 3 * npt
    scale = jnp.where(j < npt, q_scale, 1.0).astype(F32)
    half = 16
    for mi in range(tm // mc):
        rows = slice(mi * mc, (mi + 1) * mc)
        c = jnp.where(is_rope, cos_ref[rows, :] * scale, 1.0)
        sa = jnp.where(is_rope, sa_ref[rows, :] * scale, 0.0)
        sb = jnp.where(is_rope, sb_ref[rows, :] * scale, 0.0)
        for ni in range(tn // nc):
            acc = _dot(hn_ref[rows, :], w_ref[:, ni * nc:(ni + 1) * nc])
            for blk in range(nc // LANES):
                t = acc[:, blk * LANES:(blk + 1) * LANES]
                r = t * c + pltpu.roll(t, half, 1) * sa + pltpu.roll(t, LANES - half, 1) * sb
                f = jnp.where(is_gate, 1.0 / (1.0 + jnp.exp(-t)), 1.0)
                col = ni * nc + blk * LANES
                o_ref[rows, col:col + LANES] = (r * f).astype(o_ref.dtype)


def _norm_proj(x, g, w, cos_t, sa_t, sb_t, *, tm, tn, rows_per_seq, q_scale):
    m, d = x.shape
    n = w.shape[1]
    tiles_per_seq = rows_per_seq // tm
    kern = functools.partial(_norm_proj_kernel, n_col_tiles_per_part=(n // 4) // tn,
                             q_scale=q_scale, sub_rows=PROJ_SUB_ROWS, sub_cols=PROJ_SUB_COLS)
    tab_spec = pl.BlockSpec((tm, LANES), lambda i, j: (i % tiles_per_seq, 0))
    return pl.pallas_call(
        kern,
        grid=(m // tm, n // tn),
        in_specs=[
            pl.BlockSpec((tm, d), lambda i, j: (i, 0)),
            pl.BlockSpec((1, d), lambda i, j: (0, 0)),
            pl.BlockSpec((d, tn), lambda i, j: (0, j)),
            tab_spec, tab_spec, tab_spec,
        ],
        out_specs=pl.BlockSpec((tm, tn), lambda i, j: (i, j)),
        out_shape=jax.ShapeDtypeStruct((m, n), BF16),
        scratch_shapes=[pltpu.VMEM((tm, d), BF16)],
        compiler_params=pltpu.CompilerParams(
            dimension_semantics=("parallel", "arbitrary"),
            vmem_limit_bytes=VMEM_LIMIT_BYTES),
        name="l0_norm_proj",
    )(x, g, w, cos_t, sa_t, sb_t)


def _lambda_full(lq1_ref, lk1_ref, lq2_ref, lk2_ref):
    a = jnp.sum(lq1_ref[...] * lk1_ref[...], axis=-1, keepdims=True)
    b = jnp.sum(lq2_ref[...] * lk2_ref[...], axis=-1, keepdims=True)
    return jnp.exp(a) - jnp.exp(b) + LAMBDA_INIT_L0


def _diff_finalize(o1, o2, lam, subg, gate):
    o = o1 - lam * o2
    o = _rms(o, subg) * (1.0 - LAMBDA_INIT_L0)
    return (o * gate.astype(F32)).astype(BF16)


def _attn_kernel(q_ref, k_ref, v_ref, gate_ref, km_ref, vm_ref, lq1_ref, lk1_ref, lq2_ref,
                 lk2_ref, subg_ref, o_ref, m_ref, l_ref, acc_ref, *, tq, tk, dh, n_sub,
                 unroll):
    i = pl.program_id(2)
    ts = tq // n_sub
    dv = acc_ref.shape[2]

    def scores(c, kb):
        q = q_ref[c * ts:(c + 1) * ts, :]
        return jnp.concatenate([_dot_nt(q[:, :dh], kb[:, :dh]), _dot_nt(q[:, dh:], kb[:, dh:])],
                               axis=0)

    lane0 = lax.broadcasted_iota(jnp.int32, (2 * ts, LANES), 1) == 0
    for c in range(n_sub):
        s = scores(c, km_ref[...])
        m0 = jnp.max(s, axis=-1, keepdims=True)
        p = jnp.exp2(s - m0)
        m_ref[c] = jnp.broadcast_to(m0, (2 * ts, LANES))
        l_ref[c] = jnp.where(lane0, jnp.sum(p, axis=-1, keepdims=True), 0.0)
        acc_ref[c] = _dot(p.astype(BF16), vm_ref[...])

    def update(c, s, vb):
        m_prev = m_ref[c]
        m_new = jnp.maximum(m_prev, jnp.max(s, axis=-1, keepdims=True))
        alpha = jnp.exp2(m_prev - m_new)
        p = jnp.exp2(s - jnp.tile(m_new, (1, s.shape[1] // LANES)))
        psum = p[:, 0:LANES]
        for blk in range(1, s.shape[1] // LANES):
            psum = psum + p[:, blk * LANES:(blk + 1) * LANES]
        l_ref[c] = alpha * l_ref[c] + psum
        m_ref[c] = m_new
        acc_ref[c] = (acc_ref[c] * jnp.tile(alpha, (1, dv // LANES))
                      + _dot(p.astype(BF16), vb))

    def full_blocks(jb, carry):
        for u in range(unroll):
            off = pl.multiple_of((jb * unroll + u) * tk, tk)
            kb = k_ref[pl.ds(off, tk), :]
            vb = v_ref[pl.ds(off, tk), :]
            for c in range(n_sub):
                update(c, scores(c, kb), vb)
        return carry

    lax.fori_loop(0, i * (tq // tk // unroll), full_blocks, 0)

    row = lax.broadcasted_iota(jnp.int32, (2 * ts, tk), 0) % ts
    col = lax.broadcasted_iota(jnp.int32, (2 * ts, tk), 1)
    for d in range(tq // tk):
        off = pl.multiple_of(i * tq + d * tk, tk)
        kb = k_ref[pl.ds(off, tk), :]
        vb = v_ref[pl.ds(off, tk), :]
        for c in range(n_sub):
            if d * tk >= (c + 1) * ts:
                continue
            s = scores(c, kb)
            if (d + 1) * tk > c * ts:
                visible = (col + d * tk) // CHUNK <= (row + c * ts) // CHUNK
                s = jnp.where(visible, s, NEG_INF)
            update(c, s, vb)

    lam = _lambda_full(lq1_ref, lk1_ref, lq2_ref, lk2_ref)
    for c in range(n_sub):
        inv_l = 1.0 / jnp.sum(l_ref[c], axis=-1, keepdims=True)
        acc = acc_ref[c]
        o_ref[c * ts:(c + 1) * ts, :] = _diff_finalize(
            acc[:ts] * inv_l[:ts], acc[ts:] * inv_l[ts:], lam, subg_ref[...],
            gate_ref[c * ts:(c + 1) * ts, :])


def _attention(qkvg, qkvg_meta, lq1, lk1, lq2, lk2, subg, *, batch, seq, d_model):
    dv = d_model // DA_HEADS
    dh = dv // 2
    tq, tk, n_sub = ATTN_TQ, ATTN_TK, ATTN_SUBTILES
    ts = tq // n_sub
    unroll = ATTN_KEY_UNROLL
    assert ts % CHUNK == 0 and tk % CHUNK == 0 and tq % (tk * unroll) == 0
    nq = seq // tq
    h_ = DA_HEADS
    kern = functools.partial(_attn_kernel, tq=tq, tk=tk, dh=dh, n_sub=n_sub, unroll=unroll)
    vec = lambda n: pl.BlockSpec((1, n), lambda b, h, i: (0, 0))
    return pl.pallas_call(
        kern,
        grid=(batch, h_, nq),
        in_specs=[
            pl.BlockSpec((tq, dv), lambda b, h, i: (b * nq + i, h)),
            pl.BlockSpec((seq, dv), lambda b, h, i: (b, h_ + h)),
            pl.BlockSpec((seq, dv), lambda b, h, i: (b, 2 * h_ + h)),
            pl.BlockSpec((tq, dv), lambda b, h, i: (b * nq + i, 3 * h_ + h)),
            pl.BlockSpec((N_META, dv), lambda b, h, i: (0, h_ + h)),
            pl.BlockSpec((N_META, dv), lambda b, h, i: (0, 2 * h_ + h)),
            vec(dh), vec(dh), vec(dh), vec(dh), vec(dv),
        ],
        out_specs=pl.BlockSpec((tq, dv), lambda b, h, i: (b * nq + i, h)),
        out_shape=jax.ShapeDtypeStruct((batch * seq, d_model), BF16),
        scratch_shapes=[pltpu.VMEM((n_sub, 2 * ts, LANES), F32),
                        pltpu.VMEM((n_sub, 2 * ts, LANES), F32),
                        pltpu.VMEM((n_sub, 2 * ts, dv), F32)],
        compiler_params=pltpu.CompilerParams(
            dimension_semantics=("parallel", "parallel", "arbitrary"),
            vmem_limit_bytes=VMEM_LIMIT_BYTES),
        name="l0_diff_attn",
    )(qkvg, qkvg, qkvg, qkvg, qkvg_meta, qkvg_meta, lq1, lk1, lq2, lk2, subg)


def _meta_attn_kernel(q_ref, k_ref, v_ref, gate_ref, lq1_ref, lk1_ref, lq2_ref, lk2_ref,
                      subg_ref, o_ref, *, dh):
    q = q_ref[...]
    k = k_ref[...]
    v = v_ref[...]

    def attend(qm, km):
        s = _dot_nt(qm, km)
        p = jnp.exp2(s - jnp.max(s, axis=-1, keepdims=True))
        return _dot(p.astype(BF16), v) * (1.0 / jnp.sum(p, axis=-1, keepdims=True))

    lam = _lambda_full(lq1_ref, lk1_ref, lq2_ref, lk2_ref)
    o_ref[...] = _diff_finalize(attend(q[:, :dh], k[:, :dh]), attend(q[:, dh:], k[:, dh:]),
                                lam, subg_ref[...], gate_ref[...])


def _meta_attention(qkvg_meta, lq1, lk1, lq2, lk2, subg, *, d_model):
    dv = d_model // DA_HEADS
    dh = dv // 2
    h_ = DA_HEADS
    vec = lambda n: pl.BlockSpec((1, n), lambda h: (0, 0))
    blk = lambda part: pl.BlockSpec((N_META, dv), lambda h: (0, part * h_ + h))
    return pl.pallas_call(
        functools.partial(_meta_attn_kernel, dh=dh),
        grid=(h_,),
        in_specs=[blk(0), blk(1), blk(2), blk(3), vec(dh), vec(dh), vec(dh), vec(dh), vec(dv)],
        out_specs=pl.BlockSpec((N_META, dv), lambda h: (0, h)),
        out_shape=jax.ShapeDtypeStruct((N_META, d_model), BF16),
        compiler_params=pltpu.CompilerParams(dimension_semantics=("parallel",)),
        name="l0_meta_attn",
    )(qkvg_meta, qkvg_meta, qkvg_meta, qkvg_meta, lq1, lk1, lq2, lk2, subg)


def _out_proj_kernel(a_ref, w_ref, x_ref, g_ref, o_ref, *, sub_rows):
    tm = o_ref.shape[0]
    mc = min(tm, sub_rows)
    for mi in range(tm // mc):
        rows = slice(mi * mc, (mi + 1) * mc)
        y = _dot(a_ref[rows, :], w_ref[...])
        o_ref[rows, :] = x_ref[rows, :] + _rms(y, g_ref[...])


def _out_proj(a, w, x, g, *, tm):
    m, d = x.shape
    return pl.pallas_call(
        functools.partial(_out_proj_kernel, sub_rows=OUT_SUB_ROWS),
        grid=(m // tm,),
        in_specs=[
            pl.BlockSpec((tm, d), lambda i: (i, 0)),
            pl.BlockSpec((d, d), lambda i: (0, 0), pipeline_mode=pl.Buffered(1)),
            pl.BlockSpec((tm, d), lambda i: (i, 0)),
            pl.BlockSpec((1, d), lambda i: (0, 0)),
        ],
        out_specs=pl.BlockSpec((tm, d), lambda i: (i, 0)),
        out_shape=jax.ShapeDtypeStruct((m, d), F32),
        compiler_params=pltpu.CompilerParams(
            dimension_semantics=("parallel",), vmem_limit_bytes=VMEM_LIMIT_BYTES),
        name="l0_out_proj",
    )(a, w, x, g)


def _tail_kernel(a_ref, x_ref, hm_ref, w_out0_ref, post_g0_ref, pre_g_ref, w_in_ref,
                 w_grp_ref, scale_ref, w_out_ref, post_g_ref, o_ref, u_ext_ref, z_ref, *,
                 tm, d, halo):
    i = pl.program_id(1)
    pre_g = pre_g_ref[...]

    @pl.when(i == 0)
    def _():
        hmn = _rms(hm_ref[...], pre_g).astype(BF16)
        u_ext_ref[0:halo, :] = _dot(hmn, w_in_ref[:, 0:d])

    @pl.when(i > 0)
    def _():
        u_ext_ref[0:halo, :] = u_ext_ref[tm:tm + halo, :]

    h = x_ref[...] + _rms(_dot(a_ref[...], w_out0_ref[...]), post_g0_ref[...])
    o_ref[...] = h
    hn = _rms(h, pre_g).astype(BF16)
    u_ext_ref[halo:halo + tm, :] = _dot(hn, w_in_ref[:, 0:d])

    gsz = d // len(POOL_WINDOWS)
    for gi, w in enumerate(POOL_WINDOWS):
        c0 = gi * gsz
        u = u_ext_ref[halo:halo + tm, c0:c0 + gsz]
        s = u
        for back in range(1, w):
            s = s + u_ext_ref[halo - back:halo - back + tm, c0:c0 + gsz]
        mixed = (s * (1.0 / w) - u).astype(BF16)
        mo = _dot(mixed, w_grp_ref[gi]) * scale_ref[:, c0:c0 + gsz]
        gate = _dot(hn, w_in_ref[:, d + c0:d + c0 + gsz])
        z_ref[:, c0:c0 + gsz] = (mo * _silu(gate)).astype(BF16)

    y = _dot(z_ref[...], w_out_ref[...])
    o_ref[...] = o_ref[...] + _rms(y, post_g_ref[...])


def _tail(a, x, h_meta, w_out0, post_g0, pre_g, w_in, w_grp, scale, w_out, post_g, *, batch,
          seq, tm):
    m, d = x.shape
    nt = seq // tm
    halo = N_META
    assert max(POOL_WINDOWS) <= halo and halo <= tm
    const2 = lambda b, i: (0, 0)
    row_tile = lambda b, i: (b * nt + i, 0)
    resident = lambda shape, imap: pl.BlockSpec(shape, imap, pipeline_mode=pl.Buffered(1))
    kern = functools.partial(_tail_kernel, tm=tm, d=d, halo=halo)
    return pl.pallas_call(
        kern,
        grid=(batch, nt),
        in_specs=[
            pl.BlockSpec((tm, d), row_tile),
            pl.BlockSpec((tm, d), row_tile),
            resident((N_META, d), const2),
            resident((d, d), const2),
            pl.BlockSpec((1, d), const2),
            pl.BlockSpec((1, d), const2),
            resident((d, 2 * d), const2),
            resident(w_grp.shape, lambda b, i: (0, 0, 0)),
            pl.BlockSpec((1, d), const2),
            resident((d, d), const2),
            pl.BlockSpec((1, d), const2),
        ],
        out_specs=pl.BlockSpec((tm, d), row_tile),
        out_shape=jax.ShapeDtypeStruct((m, d), F32),
        scratch_shapes=[pltpu.VMEM((halo + tm, d), F32), pltpu.VMEM((tm, d), BF16)],
        compiler_params=pltpu.CompilerParams(
            dimension_semantics=("arbitrary", "arbitrary"),
            vmem_limit_bytes=VMEM_LIMIT_BYTES),
        name="l0_out_l1_pool",
    )(a, x, h_meta, w_out0, post_g0, pre_g, w_in, w_grp, scale, w_out, post_g)


def _rope_tables(length, rope_dim):
    half = rope_dim // 2
    pos = jnp.arange(length, dtype=F32)
    inv = ROPE_THETA ** (-jnp.arange(0, rope_dim, 2, dtype=F32) / rope_dim)
    ang = pos[:, None] * inv[None, :]
    cos, sin = jnp.cos(ang), jnp.sin(ang)
    zeros = lambda n: jnp.zeros((length, n), F32)
    cos_t = jnp.concatenate([cos, cos, jnp.ones((length, LANES - rope_dim), F32)], axis=1)
    sa_t = jnp.concatenate([zeros(half), sin, zeros(LANES - rope_dim)], axis=1)
    sb_t = jnp.concatenate([-sin, zeros(LANES - half)], axis=1)
    return cos_t, sa_t, sb_t


def kernel(x, meta_tokens, pre_norm_g, post_norm_g, attn_w_in, attn_w_out, attn_lambda_q1,
           attn_lambda_k1, attn_lambda_q2, attn_lambda_k2, attn_subln_g, pool_w_in,
           pool_w_group, pool_scale, pool_w_out):
    batch, seq, d = x.shape
    dv = d // DA_HEADS
    dh = dv // 2
    assert dh == LANES and seq % CHUNK == 0
    xf = x.reshape(batch * seq, d)
    meta = meta_tokens.astype(x.dtype)

    cos_t, sa_t, sb_t = _rope_tables(N_META + seq, dh // 4)
    w_in0 = attn_w_in[0].astype(BF16)
    w_out0 = attn_w_out[0].astype(BF16)
    pre_g0, pre_g1 = pre_norm_g[0:1], pre_norm_g[1:2]
    post_g0, post_g1 = post_norm_g[0:1], post_norm_g[1:2]
    q_scale = float(dh) ** -0.5 * math.log2(math.e)
    lam = [p[0:1] for p in (attn_lambda_q1, attn_lambda_k1, attn_lambda_q2, attn_lambda_k2)]
    subg = attn_subln_g[0:1]

    proj = functools.partial(_norm_proj, tn=PROJ_TN, q_scale=q_scale)
    qkvg_meta = proj(meta, pre_g0, w_in0, cos_t[:N_META], sa_t[:N_META], sb_t[:N_META],
                     tm=N_META, rows_per_seq=N_META)
    qkvg = proj(xf, pre_g0, w_in0, cos_t[N_META:], sa_t[N_META:], sb_t[N_META:],
                tm=PROJ_TM, rows_per_seq=seq)
    a_meta = _meta_attention(qkvg_meta, *lam, subg, d_model=d)
    a = _attention(qkvg, qkvg_meta, *lam, subg, batch=batch, seq=seq, d_model=d)
    h1_meta = _out_proj(a_meta, w_out0, meta, post_g0, tm=N_META)

    out = _tail(a, xf, h1_meta, w_out0, post_g0, pre_g1, pool_w_in[0].astype(BF16),
                pool_w_group[0].astype(BF16), pool_scale[0:1], pool_w_out[0].astype(BF16),
                post_g1, batch=batch, seq=seq, tm=POOL_TM)
    return out.reshape(batch, seq, d)
```

```python
import functools
import math

import jax
import jax.numpy as jnp
from jax import lax
from jax.experimental import pallas as pl
from jax.experimental.pallas import tpu as pltpu

F32 = jnp.float32
BF16 = jnp.bfloat16

N_META = 16
CHUNK = 64
RMS_EPS = 1e-6
DA_HEADS = 8
ROPE_THETA = 500000.0
POOL_WINDOWS = (2, 4, 8, 16)
LAMBDA_INIT_L0 = 0.8 - 0.6 * math.exp(-0.3 * 0)

LANES = 128
VMEM_LIMIT_BYTES = 56 * 1024 * 1024

PROJ_TM = 1024
PROJ_TN = 2048
PROJ_SUB_ROWS = 256
PROJ_SUB_COLS = 256
ATTN_TQ = 2048
ATTN_TK = 256
ATTN_SUBTILES = 8
ATTN_KEY_UNROLL = 4
OUT_SUB_ROWS = 256
POOL_TM = 256

NEG_INF = float(jnp.finfo(jnp.float32).min)


def _rms(x, g):
    ms = jnp.mean(x * x, axis=-1, keepdims=True)
    return x * lax.rsqrt(ms + RMS_EPS) * g


def _silu(g):
    return g * (1.0 / (1.0 + jnp.exp(-g)))


def _dot(a, b):
    return jnp.dot(a, b, preferred_element_type=F32)


def _dot_nt(a, b):
    return lax.dot_general(a, b, (((1,), (1,)), ((), ())), preferred_element_type=F32)


def _norm_proj_kernel(x_ref, g_ref, w_ref, cos_ref, sa_ref, sb_ref, o_ref, hn_ref, *,
                      n_col_tiles_per_part, q_scale, sub_rows, sub_cols):
    j = pl.program_id(1)
    npt = n_col_tiles_per_part

    @pl.when(j == 0)
    def _():
        hn_ref[...] = _rms(x_ref[...], g_ref[...]).astype(BF16)

    tm, tn = o_ref.shape
    mc, nc = min(tm, sub_rows), min(tn, sub_cols)
    is_rope = j < 2 * npt
    is_gate = j >= 3 * npt
    scale = jnp.where(j < npt, q_scale, 1.0).astype(F32)
    half = 16
    for mi in range(tm // mc):
        rows = slice(mi * mc, (mi + 1) * mc)
        c = jnp.where(is_rope, cos_ref[rows, :] * scale, 1.0)
        sa = jnp.where(is_rope, sa_ref[rows, :] * scale, 0.0)
        sb = jnp.where(is_rope, sb_ref[rows, :] * scale, 0.0)
        for ni in range(tn // nc):
            acc = _dot(hn_ref[rows, :], w_ref[:, ni * nc:(ni + 1) * nc])
            for blk in range(nc // LANES):
                t = acc[:, blk * LANES:(blk + 1) * LANES]
                r = t * c + pltpu.roll(t, half, 1) * sa + pltpu.roll(t, LANES - half, 1) * sb
                f = jnp.where(is_gate, 1.0 / (1.0 + jnp.exp(-t)), 1.0)
                col = ni * nc + blk * LANES
                o_ref[rows, col:col + LANES] = (r * f).astype(o_ref.dtype)


def _norm_proj(x, g, w, cos_t, sa_t, sb_t, *, tm, tn, rows_per_seq, q_scale):
    m, d = x.shape
    n = w.shape[1]
    tiles_per_seq = rows_per_seq // tm
    kern = functools.partial(_norm_proj_kernel, n_col_tiles_per_part=(n // 4) // tn,
                             q_scale=q_scale, sub_rows=PROJ_SUB_ROWS, sub_cols=PROJ_SUB_COLS)
    tab_spec = pl.BlockSpec((tm, LANES), lambda i, j: (i % tiles_per_seq, 0))
    return pl.pallas_call(
        kern,
        grid=(m // tm, n // tn),
        in_specs=[
            pl.BlockSpec((tm, d), lambda i, j: (i, 0)),
            pl.BlockSpec((1, d), lambda i, j: (0, 0)),
            pl.BlockSpec((d, tn), lambda i, j: (0, j)),
            tab_spec, tab_spec, tab_spec,
        ],
        out_specs=pl.BlockSpec((tm, tn), lambda i, j: (i, j)),
        out_shape=jax.ShapeDtypeStruct((m, n), BF16),
        scratch_shapes=[pltpu.VMEM((tm, d), BF16)],
        compiler_params=pltpu.CompilerParams(
            dimension_semantics=("parallel", "arbitrary"),
            vmem_limit_bytes=VMEM_LIMIT_BYTES),
        name="l0_norm_proj",
    )(x, g, w, cos_t, sa_t, sb_t)


def _lambda_full(lq1_ref, lk1_ref, lq2_ref, lk2_ref):
    a = jnp.sum(lq1_ref[...] * lk1_ref[...], axis=-1, keepdims=True)
    b = jnp.sum(lq2_ref[...] * lk2_ref[...], axis=-1, keepdims=True)
    return jnp.exp(a) - jnp.exp(b) + LAMBDA_INIT_L0


def _diff_finalize(o1, o2, lam, subg, gate):
    o = o1 - lam * o2
    o = _rms(o, subg) * (1.0 - LAMBDA_INIT_L0)
    return (o * gate.astype(F32)).astype(BF16)


def _attn_kernel(q_ref, k_ref, v_ref, gate_ref, km_ref, vm_ref, lq1_ref, lk1_ref, lq2_ref,
                 lk2_ref, subg_ref, o_ref, m_ref, l_ref, acc_ref, *, tq, tk, dh, n_sub,
                 unroll):
    i = pl.program_id(2)
    ts = tq // n_sub
    dv = acc_ref.shape[2]

    def scores(c, kb):
        q = q_ref[c * ts:(c + 1) * ts, :]
        return jnp.concatenate([_dot_nt(q[:, :dh], kb[:, :dh]), _dot_nt(q[:, dh:], kb[:, dh:])],
                               axis=0)

    lane0 = lax.broadcasted_iota(jnp.int32, (2 * ts, LANES), 1) == 0
    for c in range(n_sub):
        s = scores(c, km_ref[...])
        m0 = jnp.max(s, axis=-1, keepdims=True)
        p = jnp.exp2(s - m0)
        m_ref[c] = jnp.broadcast_to(m0, (2 * ts, LANES))
        l_ref[c] = jnp.where(lane0, jnp.sum(p, axis=-1, keepdims=True), 0.0)
        acc_ref[c] = _dot(p.astype(BF16), vm_ref[...])

    def update(c, s, vb):
        m_prev = m_ref[c]
        m_new = jnp.maximum(m_prev, jnp.max(s, axis=-1, keepdims=True))
        alpha = jnp.exp2(m_prev - m_new)
        p = jnp.exp2(s - jnp.tile(m_new, (1, s.shape[1] // LANES)))
        psum = p[:, 0:LANES]
        for blk in range(1, s.shape[1] // LANES):
            psum = psum + p[:, blk * LANES:(blk + 1) * LANES]
        l_ref[c] = alpha * l_ref[c] + psum
        m_ref[c] = m_new
        acc_ref[c] = (acc_ref[c] * jnp.tile(alpha, (1, dv // LANES))
                      + _dot(p.astype(BF16), vb))

    def full_blocks(jb, carry):
        for u in range(unroll):
            off = pl.multiple_of((jb * unroll + u) * tk, tk)
            kb = k_ref[pl.ds(off, tk), :]
            vb = v_ref[pl.ds(off, tk), :]
            for c in range(n_sub):
                update(c, scores(c, kb), vb)
        return carry

    lax.fori_loop(0, i * (tq // tk // unroll), full_blocks, 0)

    row = lax.broadcasted_iota(jnp.int32, (2 * ts, tk), 0) % ts
    col = lax.broadcasted_iota(jnp.int32, (2 * ts, tk), 1)
    for d in range(tq // tk):
        off = pl.multiple_of(i * tq + d * tk, tk)
        kb = k_ref[pl.ds(off, tk), :]
        vb = v_ref[pl.ds(off, tk), :]
        for c in range(n_sub):
            if d * tk >= (c + 1) * ts:
                continue
            s = scores(c, kb)
            if (d + 1) * tk > c * ts:
                visible = (col + d * tk) // CHUNK <= (row + c * ts) // CHUNK
                s = jnp.where(visible, s, NEG_INF)
            update(c, s, vb)

    lam = _lambda_full(lq1_ref, lk1_ref, lq2_ref, lk2_ref)
    for c in range(n_sub):
        inv_l = 1.0 / jnp.sum(l_ref[c], axis=-1, keepdims=True)
        acc = acc_ref[c]
        o_ref[c * ts:(c + 1) * ts, :] = _diff_finalize(
            acc[:ts] * inv_l[:ts], acc[ts:] * inv_l[ts:], lam, subg_ref[...],
            gate_ref[c * ts:(c + 1) * ts, :])


def _attention(qkvg, qkvg_meta, lq1, lk1, lq2, lk2, subg, *, batch, seq, d_model):
    dv = d_model // DA_HEADS
    dh = dv // 2
    tq, tk, n_sub = ATTN_TQ, ATTN_TK, ATTN_SUBTILES
    ts = tq // n_sub
    unroll = ATTN_KEY_UNROLL
    assert ts % CHUNK == 0 and tk % CHUNK == 0 and tq % (tk * unroll) == 0
    nq = seq // tq
    h_ = DA_HEADS
    kern = functools.partial(_attn_kernel, tq=tq, tk=tk, dh=dh, n_sub=n_sub, unroll=unroll)
    vec = lambda n: pl.BlockSpec((1, n), lambda b, h, i: (0, 0))
    return pl.pallas_call(
        kern,
        grid=(batch, h_, nq),
        in_specs=[
            pl.BlockSpec((tq, dv), lambda b, h, i: (b * nq + i, h)),
            pl.BlockSpec((seq, dv), lambda b, h, i: (b, h_ + h)),
            pl.BlockSpec((seq, dv), lambda b, h, i: (b, 2 * h_ + h)),
            pl.BlockSpec((tq, dv), lambda b, h, i: (b * nq + i, 3 * h_ + h)),
            pl.BlockSpec((N_META, dv), lambda b, h, i: (0, h_ + h)),
            pl.BlockSpec((N_META, dv), lambda b, h, i: (0, 2 * h_ + h)),
            vec(dh), vec(dh), vec(dh), vec(dh), vec(dv),
        ],
        out_specs=pl.BlockSpec((tq, dv), lambda b, h, i: (b * nq + i, h)),
        out_shape=jax.ShapeDtypeStruct((batch * seq, d_model), BF16),
        scratch_shapes=[pltpu.VMEM((n_sub, 2 * ts, LANES), F32),
                        pltpu.VMEM((n_sub, 2 * ts, LANES), F32),
                        pltpu.VMEM((n_sub, 2 * ts, dv), F32)],
        compiler_params=pltpu.CompilerParams(
            dimension_semantics=("parallel", "parallel", "arbitrary"),
            vmem_limit_bytes=VMEM_LIMIT_BYTES),
        name="l0_diff_attn",
    )(qkvg, qkvg, qkvg, qkvg, qkvg_meta, qkvg_meta, lq1, lk1, lq2, lk2, subg)


def _meta_attn_kernel(q_ref, k_ref, v_ref, gate_ref, lq1_ref, lk1_ref, lq2_ref, lk2_ref,
                      subg_ref, o_ref, *, dh):
    q = q_ref[...]
    k = k_ref[...]
    v = v_ref[...]

    def attend(qm, km):
        s = _dot_nt(qm, km)
        p = jnp.exp2(s - jnp.max(s, axis=-1, keepdims=True))
        return _dot(p.astype(BF16), v) * (1.0 / jnp.sum(p, axis=-1, keepdims=True))

    lam = _lambda_full(lq1_ref, lk1_ref, lq2_ref, lk2_ref)
    o_ref[...] = _diff_finalize(attend(q[:, :dh], k[:, :dh]), attend(q[:, dh:], k[:, dh:]),
                                lam, subg_ref[...], gate_ref[...])


def _meta_attention(qkvg_meta, lq1, lk1, lq2, lk2, subg, *, d_model):
    dv = d_model // DA_HEADS
    dh = dv // 2
    h_ = DA_HEADS
    vec = lambda n: pl.BlockSpec((1, n), lambda h: (0, 0))
    blk = lambda part: pl.BlockSpec((N_META, dv), lambda h: (0, part * h_ + h))
    return pl.pallas_call(
        functools.partial(_meta_attn_kernel, dh=dh),
        grid=(h_,),
        in_specs=[blk(0), blk(1), blk(2), blk(3), vec(dh), vec(dh), vec(dh), vec(dh), vec(dv)],
        out_specs=pl.BlockSpec((N_META, dv), lambda h: (0, h)),
        out_shape=jax.ShapeDtypeStruct((N_META, d_model), BF16),
        compiler_params=pltpu.CompilerParams(dimension_semantics=("parallel",)),
        name="l0_meta_attn",
    )(qkvg_meta, qkvg_meta, qkvg_meta, qkvg_meta, lq1, lk1, lq2, lk2, subg)


def _out_proj_kernel(a_ref, w_ref, x_ref, g_ref, o_ref, *, sub_rows):
    tm = o_ref.shape[0]
    mc = min(tm, sub_rows)
    for mi in range(tm // mc):
        rows = slice(mi * mc, (mi + 1) * mc)
        y = _dot(a_ref[rows, :], w_ref[...])
        o_ref[rows, :] = x_ref[rows, :] + _rms(y, g_ref[...])


def _out_proj(a, w, x, g, *, tm):
    m, d = x.shape
    return pl.pallas_call(
        functools.partial(_out_proj_kernel, sub_rows=OUT_SUB_ROWS),
        grid=(m // tm,),
        in_specs=[
            pl.BlockSpec((tm, d), lambda i: (i, 0)),
            pl.BlockSpec((d, d), lambda i: (0, 0), pipeline_mode=pl.Buffered(1)),
            pl.BlockSpec((tm, d), lambda i: (i, 0)),
            pl.BlockSpec((1, d), lambda i: (0, 0)),
        ],
        out_specs=pl.BlockSpec((tm, d), lambda i: (i, 0)),
        out_shape=jax.ShapeDtypeStruct((m, d), F32),
        compiler_params=pltpu.CompilerParams(
            dimension_semantics=("parallel",), vmem_limit_bytes=VMEM_LIMIT_BYTES),
        name="l0_out_proj",
    )(a, w, x, g)


def _tail_kernel(a_ref, x_ref, hm_ref, w_out0_ref, post_g0_ref, pre_g_ref, w_in_ref,
                 w_grp_ref, scale_ref, w_out_ref, post_g_ref, o_ref, u_ext_ref, z_ref, *,
                 tm, d, halo):
    i = pl.program_id(1)
    pre_g = pre_g_ref[...]

    @pl.when(i == 0)
    def _():
        hmn = _rms(hm_ref[...], pre_g).astype(BF16)
        u_ext_ref[0:halo, :] = _dot(hmn, w_in_ref[:, 0:d])

    @pl.when(i > 0)
    def _():
        u_ext_ref[0:halo, :] = u_ext_ref[tm:tm + halo, :]

    h = x_ref[...] + _rms(_dot(a_ref[...], w_out0_ref[...]), post_g0_ref[...])
    o_ref[...] = h
    hn = _rms(h, pre_g).astype(BF16)
    u_ext_ref[halo:halo + tm, :] = _dot(hn, w_in_ref[:, 0:d])

    gsz = d // len(POOL_WINDOWS)
    for gi, w in enumerate(POOL_WINDOWS):
        c0 = gi * gsz
        u = u_ext_ref[halo:halo + tm, c0:c0 + gsz]
        s = u
        for back in range(1, w):
            s = s + u_ext_ref[halo - back:halo - back + tm, c0:c0 + gsz]
        mixed = (s * (1.0 / w) - u).astype(BF16)
        mo = _dot(mixed, w_grp_ref[gi]) * scale_ref[:, c0:c0 + gsz]
        gate = _dot(hn, w_in_ref[:, d + c0:d + c0 + gsz])
        z_ref[:, c0:c0 + gsz] = (mo * _silu(gate)).astype(BF16)

    y = _dot(z_ref[...], w_out_ref[...])
    o_ref[...] = o_ref[...] + _rms(y, post_g_ref[...])


def _tail(a, x, h_meta, w_out0, post_g0, pre_g, w_in, w_grp, scale, w_out, post_g, *, batch,
          seq, tm):
    m, d = x.shape
    nt = seq // tm
    halo = N_META
    assert max(POOL_WINDOWS) <= halo and halo <= tm
    const2 = lambda b, i: (0, 0)
    row_tile = lambda b, i: (b * nt + i, 0)
    resident = lambda shape, imap: pl.BlockSpec(shape, imap, pipeline_mode=pl.Buffered(1))
    kern = functools.partial(_tail_kernel, tm=tm, d=d, halo=halo)
    return pl.pallas_call(
        kern,
        grid=(batch, nt),
        in_specs=[
            pl.BlockSpec((tm, d), row_tile),
            pl.BlockSpec((tm, d), row_tile),
            resident((N_META, d), const2),
            resident((d, d), const2),
            pl.BlockSpec((1, d), const2),
            pl.BlockSpec((1, d), const2),
            resident((d, 2 * d), const2),
            resident(w_grp.shape, lambda b, i: (0, 0, 0)),
            pl.BlockSpec((1, d), const2),
            resident((d, d), const2),
            pl.BlockSpec((1, d), const2),
        ],
        out_specs=pl.BlockSpec((tm, d), row_tile),
        out_shape=jax.ShapeDtypeStruct((m, d), F32),
        scratch_shapes=[pltpu.VMEM((halo + tm, d), F32), pltpu.VMEM((tm, d), BF16)],
        compiler_params=pltpu.CompilerParams(
            dimension_semantics=("arbitrary", "arbitrary"),
            vmem_limit_bytes=VMEM_LIMIT_BYTES),
        name="l0_out_l1_pool",
    )(a, x, h_meta, w_out0, post_g0, pre_g, w_in, w_grp, scale, w_out, post_g)


def _rope_tables(length, rope_dim):
    half = rope_dim // 2
    pos = jnp.arange(length, dtype=F32)
    inv = ROPE_THETA ** (-jnp.arange(0, rope_dim, 2, dtype=F32) / rope_dim)
    ang = pos[:, None] * inv[None, :]
    cos, sin = jnp.cos(ang), jnp.sin(ang)
    zeros = lambda n: jnp.zeros((length, n), F32)
    cos_t = jnp.concatenate([cos, cos, jnp.ones((length, LANES - rope_dim), F32)], axis=1)
    sa_t = jnp.concatenate([zeros(half), sin, zeros(LANES - rope_dim)], axis=1)
    sb_t = jnp.concatenate([-sin, zeros(LANES - half)], axis=1)
    return cos_t, sa_t, sb_t


def kernel(x, meta_tokens, pre_norm_g, post_norm_g, attn_w_in, attn_w_out, attn_lambda_q1,
           attn_lambda_k1, attn_lambda_q2, attn_lambda_k2, attn_subln_g, pool_w_in,
           pool_w_group, pool_scale, pool_w_out):
    batch, seq, d = x.shape
    dv = d // DA_HEADS
    dh = dv // 2
    assert dh == LANES and seq % CHUNK == 0
    xf = x.reshape(batch * seq, d)
    meta = meta_tokens.astype(x.dtype)

    cos_t, sa_t, sb_t = _rope_tables(N_META + seq, dh // 4)
    w_in0 = attn_w_in[0].astype(BF16)
    w_out0 = attn_w_out[0].astype(BF16)
    pre_g0, pre_g1 = pre_norm_g[0:1], pre_norm_g[1:2]
    post_g0, post_g1 = post_norm_g[0:1], post_norm_g[1:2]
    q_scale = float(dh) ** -0.5 * math.log2(math.e)
    lam = [p[0:1] for p in (attn_lambda_q1, attn_lambda_k1, attn_lambda_q2, attn_lambda_k2)]
    subg = attn_subln_g[0:1]

    proj = functools.partial(_norm_proj, tn=PROJ_TN, q_scale=q_scale)
    qkvg_meta = proj(meta, pre_g0, w_in0, cos_t[:N_META], sa_t[:N_META], sb_t[:N_META],
                     tm=N_META, rows_per_seq=N_META)
    qkvg = proj(xf, pre_g0, w_in0, cos_t[N_META:], sa_t[N_META:], sb_t[N_META:],
                tm=PROJ_TM, rows_per_seq=seq)
    a_meta = _meta_attention(qkvg_meta, *lam, subg, d_model=d)
    a = _attention(qkvg, qkvg_meta, *lam, subg, batch=batch, seq=seq, d_model=d)
    h1_meta = _out_proj(a_meta, w_out0, meta, post_g0, tm=N_META)

    out = _tail(a, xf, h1_meta, w_out0, post_g0, pre_g1, pool_w_in[0].astype(BF16),
                pool_w_group[0].astype(BF16), pool_scale[0:1], pool_w_out[0].astype(BF16),
                post_g1, batch=batch, seq=seq, tm=POOL_TM)
    return out.reshape(batch, seq, d)
```

```python
import functools
import math

import jax
import jax.numpy as jnp
from jax import lax
from jax.experimental import pallas as pl
from jax.experimental.pallas import tpu as pltpu

F32 = jnp.float32
BF16 = jnp.bfloat16

N_META = 16
CHUNK = 64
RMS_EPS = 1e-6
DA_HEADS = 8
ROPE_THETA = 500000.0
POOL_WINDOWS = (2, 4, 8, 16)
LAMBDA_INIT_L0 = 0.8 - 0.6 * math.exp(-0.3 * 0)

LANES = 128
VMEM_LIMIT_BYTES = 56 * 1024 * 1024

PROJ_TM = 1024
PROJ_TN = 2048
PROJ_SUB_ROWS = 256
PROJ_SUB_COLS = 256
ATTN_TQ = 2048
ATTN_TK = 256
ATTN_SUBTILES = 8
ATTN_KEY_UNROLL = 4
OUT_SUB_ROWS = 256
POOL_TM = 256

NEG_INF = float(jnp.finfo(jnp.float32).min)


def _rms(x, g):
    ms = jnp.mean(x * x, axis=-1, keepdims=True)
    return x * lax.rsqrt(ms + RMS_EPS) * g


def _silu(g):
    return g * (1.0 / (1.0 + jnp.exp(-g)))


def _dot(a, b):
    return jnp.dot(a, b, preferred_element_type=F32)


def _dot_nt(a, b):
    return lax.dot_general(a, b, (((1,), (1,)), ((), ())), preferred_element_type=F32)


def _norm_proj_kernel(x_ref, g_ref, w_ref, cos_ref, sa_ref, sb_ref, o_ref, hn_ref, *,
                      n_col_tiles_per_part, q_scale, sub_rows, sub_cols):
    j = pl.program_id(1)
    npt = n_col_tiles_per_part

    @pl.when(j == 0)
    def _():
        hn_ref[...] = _rms(x_ref[...], g_ref[...]).astype(BF16)

    tm, tn = o_ref.shape
    mc, nc = min(tm, sub_rows), min(tn, sub_cols)
    is_rope = j < 2 * npt
    is_gate = j >= 3 * npt
    scale = jnp.where(j < npt, q_scale, 1.0).astype(F32)
    half = 16
    for mi in range(tm // mc):
        rows = slice(mi * mc, (mi + 1) * mc)
        c = jnp.where(is_rope, cos_ref[rows, :] * scale, 1.0)
        sa = jnp.where(is_rope, sa_ref[rows, :] * scale, 0.0)
        sb = jnp.where(is_rope, sb_ref[rows, :] * scale, 0.0)
        for ni in range(tn // nc):
            acc = _dot(hn_ref[rows, :], w_ref[:, ni * nc:(ni + 1) * nc])
            for blk in range(nc // LANES):
                t = acc[:, blk * LANES:(blk + 1) * LANES]
                r = t * c + pltpu.roll(t, half, 1) * sa + pltpu.roll(t, LANES - half, 1) * sb
                f = jnp.where(is_gate, 1.0 / (1.0 + jnp.exp(-t)), 1.0)
                col = ni * nc + blk * LANES
                o_ref[rows, col:col + LANES] = (r * f).astype(o_ref.dtype)


def _norm_proj(x, g, w, cos_t, sa_t, sb_t, *, tm, tn, rows_per_seq, q_scale):
    m, d = x.shape
    n = w.shape[1]
    tiles_per_seq = rows_per_seq // tm
    kern = functools.partial(_norm_proj_kernel, n_col_tiles_per_part=(n // 4) // tn,
                             q_scale=q_scale, sub_rows=PROJ_SUB_ROWS, sub_cols=PROJ_SUB_COLS)
    tab_spec = pl.BlockSpec((tm, LANES), lambda i, j: (i % tiles_per_seq, 0))
    return pl.pallas_call(
        kern,
        grid=(m // tm, n // tn),
        in_specs=[
            pl.BlockSpec((tm, d), lambda i, j: (i, 0)),
            pl.BlockSpec((1, d), lambda i, j: (0, 0)),
            pl.BlockSpec((d, tn), lambda i, j: (0, j)),
            tab_spec, tab_spec, tab_spec,
        ],
        out_specs=pl.BlockSpec((tm, tn), lambda i, j: (i, j)),
        out_shape=jax.ShapeDtypeStruct((m, n), BF16),
        scratch_shapes=[pltpu.VMEM((tm, d), BF16)],
        compiler_params=pltpu.CompilerParams(
            dimension_semantics=("parallel", "arbitrary"),
            vmem_limit_bytes=VMEM_LIMIT_BYTES),
        name="l0_norm_proj",
    )(x, g, w, cos_t, sa_t, sb_t)


def _lambda_full(lq1_ref, lk1_ref, lq2_ref, lk2_ref):
    a = jnp.sum(lq1_ref[...] * lk1_ref[...], axis=-1, keepdims=True)
    b = jnp.sum(lq2_ref[...] * lk2_ref[...], axis=-1, keepdims=True)
    return jnp.exp(a) - jnp.exp(b) + LAMBDA_INIT_L0


def _diff_finalize(acc1, acc2, inv_l1, inv_l2, lam, subg, gate):
    o = acc1 * inv_l1 - acc2 * (lam * inv_l2)
    o = _rms(o, subg * (1.0 - LAMBDA_INIT_L0))
    return (o * gate.astype(F32)).astype(BF16)


def _attn_kernel(q_ref, k_ref, v_ref, gate_ref, km_ref, vm_ref, lq1_ref, lk1_ref, lq2_ref,
                 lk2_ref, subg_ref, o_ref, m_ref, l_ref, acc_ref, *, tq, tk, dh, n_sub,
                 unroll):
    i = pl.program_id(2)
    ts = tq // n_sub
    dv = acc_ref.shape[2]

    def scores(c, kb):
        q = q_ref[c * ts:(c + 1) * ts, :]
        return jnp.concatenate([_dot_nt(q[:, :dh], kb[:, :dh]), _dot_nt(q[:, dh:], kb[:, dh:])],
                               axis=0)

    def lane_sum(p):
        psum = p[:, 0:LANES]
        for blk in range(1, p.shape[1] // LANES):
            psum = psum + p[:, blk * LANES:(blk + 1) * LANES]
        return psum

    row = lax.broadcasted_iota(jnp.int32, (2 * ts, tk), 0) % ts
    col = lax.broadcasted_iota(jnp.int32, (2 * ts, tk), 1)
    visible = col // CHUNK <= row // CHUNK
    for c in range(n_sub):
        off = pl.multiple_of(i * tq + c * ts, ts)
        s = jnp.where(visible, scores(c, k_ref[pl.ds(off, tk), :]), NEG_INF)
        m0 = jnp.max(s, axis=-1, keepdims=True)
        p = jnp.exp2(s - m0)
        m_ref[c] = jnp.broadcast_to(m0, (2 * ts, LANES))
        l_ref[c] = lane_sum(p)
        acc_ref[c] = _dot(p.astype(BF16), v_ref[pl.ds(off, tk), :])

    def update(c, s, vb):
        m_prev = m_ref[c]
        m_new = jnp.maximum(m_prev, jnp.max(s, axis=-1, keepdims=True))
        alpha = jnp.exp2(m_prev - m_new)
        p = jnp.exp2(s - jnp.tile(m_new, (1, s.shape[1] // LANES)))
        l_ref[c] = alpha * l_ref[c] + lane_sum(p)
        m_ref[c] = m_new
        acc_ref[c] = (acc_ref[c] * jnp.tile(alpha, (1, dv // LANES))
                      + _dot(p.astype(BF16), vb))

    def full_blocks(jb, carry):
        for u in range(unroll):
            off = pl.multiple_of((jb * unroll + u) * tk, tk)
            kb = k_ref[pl.ds(off, tk), :]
            vb = v_ref[pl.ds(off, tk), :]
            for c in range(n_sub):
                update(c, scores(c, kb), vb)
        return carry

    lax.fori_loop(0, i * (tq // tk // unroll), full_blocks, 0)

    for d in range(n_sub - 1):
        off = pl.multiple_of(i * tq + d * tk, tk)
        kb = k_ref[pl.ds(off, tk), :]
        vb = v_ref[pl.ds(off, tk), :]
        for c in range(d + 1, n_sub):
            update(c, scores(c, kb), vb)

    lam = _lambda_full(lq1_ref, lk1_ref, lq2_ref, lk2_ref)
    lane0 = lax.broadcasted_iota(jnp.int32, (2 * ts, LANES), 1) == 0
    for c in range(n_sub):
        s = scores(c, km_ref[...])
        m_prev = m_ref[c]
        m_new = jnp.maximum(m_prev, jnp.max(s, axis=-1, keepdims=True))
        alpha = jnp.exp2(m_prev - m_new)
        p = jnp.exp2(s - m_new[:, :s.shape[1]])
        l = alpha * l_ref[c] + jnp.where(lane0, jnp.sum(p, axis=-1, keepdims=True), 0.0)
        acc = (acc_ref[c] * jnp.tile(alpha, (1, dv // LANES)) + _dot(p.astype(BF16), vm_ref[...]))
        inv_l = 1.0 / jnp.sum(l, axis=-1, keepdims=True)
        o_ref[c * ts:(c + 1) * ts, :] = _diff_finalize(
            acc[:ts], acc[ts:], inv_l[:ts], inv_l[ts:], lam, subg_ref[...],
            gate_ref[c * ts:(c + 1) * ts, :])


def _attention(qkvg, qkvg_meta, lq1, lk1, lq2, lk2, subg, *, batch, seq, d_model):
    dv = d_model // DA_HEADS
    dh = dv // 2
    tq, tk, n_sub = ATTN_TQ, ATTN_TK, ATTN_SUBTILES
    ts = tq // n_sub
    unroll = ATTN_KEY_UNROLL
    assert ts % CHUNK == 0 and tk % CHUNK == 0 and tq % (tk * unroll) == 0
    nq = seq // tq
    h_ = DA_HEADS
    kern = functools.partial(_attn_kernel, tq=tq, tk=tk, dh=dh, n_sub=n_sub, unroll=unroll)
    vec = lambda n: pl.BlockSpec((1, n), lambda b, h, i: (0, 0))
    return pl.pallas_call(
        kern,
        grid=(batch, h_, nq),
        in_specs=[
            pl.BlockSpec((tq, dv), lambda b, h, i: (b * nq + i, h)),
            pl.BlockSpec((seq, dv), lambda b, h, i: (b, h_ + h)),
            pl.BlockSpec((seq, dv), lambda b, h, i: (b, 2 * h_ + h)),
            pl.BlockSpec((tq, dv), lambda b, h, i: (b * nq + i, 3 * h_ + h)),
            pl.BlockSpec((N_META, dv), lambda b, h, i: (0, h_ + h)),
            pl.BlockSpec((N_META, dv), lambda b, h, i: (0, 2 * h_ + h)),
            vec(dh), vec(dh), vec(dh), vec(dh), vec(dv),
        ],
        out_specs=pl.BlockSpec((tq, dv), lambda b, h, i: (b * nq + i, h)),
        out_shape=jax.ShapeDtypeStruct((batch * seq, d_model), BF16),
        scratch_shapes=[pltpu.VMEM((n_sub, 2 * ts, LANES), F32),
                        pltpu.VMEM((n_sub, 2 * ts, LANES), F32),
                        pltpu.VMEM((n_sub, 2 * ts, dv), F32)],
        compiler_params=pltpu.CompilerParams(
            dimension_semantics=("parallel", "parallel", "arbitrary"),
            vmem_limit_bytes=VMEM_LIMIT_BYTES),
        name="l0_diff_attn",
    )(qkvg, qkvg, qkvg, qkvg, qkvg_meta, qkvg_meta, lq1, lk1, lq2, lk2, subg)


def _meta_attn_kernel(q_ref, k_ref, v_ref, gate_ref, lq1_ref, lk1_ref, lq2_ref, lk2_ref,
                      subg_ref, o_ref, *, dh):
    q = q_ref[...]
    k = k_ref[...]
    v = v_ref[...]

    def attend(qm, km):
        s = _dot_nt(qm, km)
        p = jnp.exp2(s - jnp.max(s, axis=-1, keepdims=True))
        return _dot(p.astype(BF16), v), 1.0 / jnp.sum(p, axis=-1, keepdims=True)

    lam = _lambda_full(lq1_ref, lk1_ref, lq2_ref, lk2_ref)
    acc1, inv_l1 = attend(q[:, :dh], k[:, :dh])
    acc2, inv_l2 = attend(q[:, dh:], k[:, dh:])
    o_ref[...] = _diff_finalize(acc1, acc2, inv_l1, inv_l2, lam, subg_ref[...], gate_ref[...])


def _meta_attention(qkvg_meta, lq1, lk1, lq2, lk2, subg, *, d_model):
    dv = d_model // DA_HEADS
    dh = dv // 2
    h_ = DA_HEADS
    vec = lambda n: pl.BlockSpec((1, n), lambda h: (0, 0))
    blk = lambda part: pl.BlockSpec((N_META, dv), lambda h: (0, part * h_ + h))
    return pl.pallas_call(
        functools.partial(_meta_attn_kernel, dh=dh),
        grid=(h_,),
        in_specs=[blk(0), blk(1), blk(2), blk(3), vec(dh), vec(dh), vec(dh), vec(dh), vec(dv)],
        out_specs=pl.BlockSpec((N_META, dv), lambda h: (0, h)),
        out_shape=jax.ShapeDtypeStruct((N_META, d_model), BF16),
        compiler_params=pltpu.CompilerParams(dimension_semantics=("parallel",)),
        name="l0_meta_attn",
    )(qkvg_meta, qkvg_meta, qkvg_meta, qkvg_meta, lq1, lk1, lq2, lk2, subg)


def _out_proj_kernel(a_ref, w_ref, x_ref, g_ref, o_ref, *, sub_rows):
    tm = o_ref.shape[0]
    mc = min(tm, sub_rows)
    for mi in range(tm // mc):
        rows = slice(mi * mc, (mi + 1) * mc)
        y = _dot(a_ref[rows, :], w_ref[...])
        o_ref[rows, :] = x_ref[rows, :] + _rms(y, g_ref[...])


def _out_proj(a, w, x, g, *, tm):
    m, d = x.shape
    return pl.pallas_call(
        functools.partial(_out_proj_kernel, sub_rows=OUT_SUB_ROWS),
        grid=(m // tm,),
        in_specs=[
            pl.BlockSpec((tm, d), lambda i: (i, 0)),
            pl.BlockSpec((d, d), lambda i: (0, 0), pipeline_mode=pl.Buffered(1)),
            pl.BlockSpec((tm, d), lambda i: (i, 0)),
            pl.BlockSpec((1, d), lambda i: (0, 0)),
        ],
        out_specs=pl.BlockSpec((tm, d), lambda i: (i, 0)),
        out_shape=jax.ShapeDtypeStruct((m, d), F32),
        compiler_params=pltpu.CompilerParams(
            dimension_semantics=("parallel",), vmem_limit_bytes=VMEM_LIMIT_BYTES),
        name="l0_out_proj",
    )(a, w, x, g)


def _tail_kernel(a_ref, x_ref, hm_ref, w_out0_ref, post_g0_ref, pre_g_ref, w_in_ref,
                 w_grp_ref, scale_ref, w_out_ref, post_g_ref, o_ref, u_ext_ref, z_ref, *,
                 tm, d, halo):
    i = pl.program_id(1)
    pre_g = pre_g_ref[...]

    @pl.when(i == 0)
    def _():
        hmn = _rms(hm_ref[...], pre_g).astype(BF16)
        u_ext_ref[0:halo, :] = _dot(hmn, w_in_ref[:, 0:d])

    @pl.when(i > 0)
    def _():
        u_ext_ref[0:halo, :] = u_ext_ref[tm:tm + halo, :]

    h = x_ref[...] + _rms(_dot(a_ref[...], w_out0_ref[...]), post_g0_ref[...])
    o_ref[...] = h
    hn = _rms(h, pre_g).astype(BF16)
    u_ext_ref[halo:halo + tm, :] = _dot(hn, w_in_ref[:, 0:d])

    gsz = d // len(POOL_WINDOWS)
    for gi, w in enumerate(POOL_WINDOWS):
        c0 = gi * gsz
        u = u_ext_ref[halo:halo + tm, c0:c0 + gsz]
        s = u
        for back in range(1, w):
            s = s + u_ext_ref[halo - back:halo - back + tm, c0:c0 + gsz]
        mixed = (s * (1.0 / w) - u).astype(BF16)
        mo = _dot(mixed, w_grp_ref[gi]) * scale_ref[:, c0:c0 + gsz]
        gate = _dot(hn, w_in_ref[:, d + c0:d + c0 + gsz])
        z_ref[:, c0:c0 + gsz] = (mo * _silu(gate)).astype(BF16)

    y = _dot(z_ref[...], w_out_ref[...])
    o_ref[...] = o_ref[...] + _rms(y, post_g_ref[...])


def _tail(a, x, h_meta, w_out0, post_g0, pre_g, w_in, w_grp, scale, w_out, post_g, *, batch,
          seq, tm):
    m, d = x.shape
    nt = seq // tm
    halo = N_META
    assert max(POOL_WINDOWS) <= halo and halo <= tm
    const2 = lambda b, i: (0, 0)
    row_tile = lambda b, i: (b * nt + i, 0)
    resident = lambda shape, imap: pl.BlockSpec(shape, imap, pipeline_mode=pl.Buffered(1))
    kern = functools.partial(_tail_kernel, tm=tm, d=d, halo=halo)
    return pl.pallas_call(
        kern,
        grid=(batch, nt),
        in_specs=[
            pl.BlockSpec((tm, d), row_tile),
            pl.BlockSpec((tm, d), row_tile),
            resident((N_META, d), const2),
            resident((d, d), const2),
            pl.BlockSpec((1, d), const2),
            pl.BlockSpec((1, d), const2),
            resident((d, 2 * d), const2),
            resident(w_grp.shape, lambda b, i: (0, 0, 0)),
            pl.BlockSpec((1, d), const2),
            resident((d, d), const2),
            pl.BlockSpec((1, d), const2),
        ],
        out_specs=pl.BlockSpec((tm, d), row_tile),
        out_shape=jax.ShapeDtypeStruct((m, d), F32),
        scratch_shapes=[pltpu.VMEM((halo + tm, d), F32), pltpu.VMEM((tm, d), BF16)],
        compiler_params=pltpu.CompilerParams(
            dimension_semantics=("arbitrary", "arbitrary"),
            vmem_limit_bytes=VMEM_LIMIT_BYTES),
        name="l0_out_l1_pool",
    )(a, x, h_meta, w_out0, post_g0, pre_g, w_in, w_grp, scale, w_out, post_g)


def _rope_tables(length, rope_dim):
    half = rope_dim // 2
    pos = jnp.arange(length, dtype=F32)
    inv = ROPE_THETA ** (-jnp.arange(0, rope_dim, 2, dtype=F32) / rope_dim)
    ang = pos[:, None] * inv[None, :]
    cos, sin = jnp.cos(ang), jnp.sin(ang)
    zeros = lambda n: jnp.zeros((length, n), F32)
    cos_t = jnp.concatenate([cos, cos, jnp.ones((length, LANES - rope_dim), F32)], axis=1)
    sa_t = jnp.concatenate([zeros(half), sin, zeros(LANES - rope_dim)], axis=1)
    sb_t = jnp.concatenate([-sin, zeros(LANES - half)], axis=1)
    return cos_t, sa_t, sb_t


def kernel(x, meta_tokens, pre_norm_g, post_norm_g, attn_w_in, attn_w_out, attn_lambda_q1,
           attn_lambda_k1, attn_lambda_q2, attn_lambda_k2, attn_subln_g, pool_w_in,
           pool_w_group, pool_scale, pool_w_out):
    batch, seq, d = x.shape
    dv = d // DA_HEADS
    dh = dv // 2
    assert dh == LANES and seq % CHUNK == 0
    xf = x.reshape(batch * seq, d)
    meta = meta_tokens.astype(x.dtype)

    cos_t, sa_t, sb_t = _rope_tables(N_META + seq, dh // 4)
    w_in0 = attn_w_in[0].astype(BF16)
    w_out0 = attn_w_out[0].astype(BF16)
    pre_g0, pre_g1 = pre_norm_g[0:1], pre_norm_g[1:2]
    post_g0, post_g1 = post_norm_g[0:1], post_norm_g[1:2]
    q_scale = float(dh) ** -0.5 * math.log2(math.e)
    lam = [p[0:1] for p in (attn_lambda_q1, attn_lambda_k1, attn_lambda_q2, attn_lambda_k2)]
    subg = attn_subln_g[0:1]

    proj = functools.partial(_norm_proj, tn=PROJ_TN, q_scale=q_scale)
    qkvg_meta = proj(meta, pre_g0, w_in0, cos_t[:N_META], sa_t[:N_META], sb_t[:N_META],
                     tm=N_META, rows_per_seq=N_META)
    qkvg = proj(xf, pre_g0, w_in0, cos_t[N_META:], sa_t[N_META:], sb_t[N_META:],
                tm=PROJ_TM, rows_per_seq=seq)
    a_meta = _meta_attention(qkvg_meta, *lam, subg, d_model=d)
    a = _attention(qkvg, qkvg_meta, *lam, subg, batch=batch, seq=seq, d_model=d)
    h1_meta = _out_proj(a_meta, w_out0, meta, post_g0, tm=N_META)

    out = _tail(a, xf, h1_meta, w_out0, post_g0, pre_g1, pool_w_in[0].astype(BF16),
                pool_w_group[0].astype(BF16), pool_scale[0:1], pool_w_out[0].astype(BF16),
                post_g1, batch=batch, seq=seq, tm=POOL_TM)
    return out.reshape(batch, seq, d)
```

```python
import functools
import math

import jax
import jax.numpy as jnp
import numpy as np
from jax import lax
from jax.experimental import pallas as pl
from jax.experimental.pallas import tpu as pltpu

F32 = jnp.float32
BF16 = jnp.bfloat16

N_META = 16
CHUNK = 64
RMS_EPS = 1e-6
DA_HEADS = 8
ROPE_THETA = 500000.0
POOL_WINDOWS = (2, 4, 8, 16)
LAMBDA_INIT_L0 = 0.8 - 0.6 * math.exp(-0.3 * 0)

LANES = 128
VMEM_LIMIT_BYTES = 56 * 1024 * 1024

PROJ_TM = 1024
PROJ_TN = 2048
PROJ_SUB_ROWS = 256
PROJ_SUB_COLS = 256
ATTN_TQ = 2048
ATTN_TK = 256
ATTN_SUBTILES = 8
ATTN_KEY_UNROLL = 4
WEIGHT_STAGE_ROWS = 128
POOL_TM = 256

NEG_INF = float(jnp.finfo(jnp.float32).min)


def _rms(x, g):
    ms = jnp.mean(x * x, axis=-1, keepdims=True)
    return x * lax.rsqrt(ms + RMS_EPS) * g


def _silu(g):
    return g * (1.0 / (1.0 + jnp.exp(-g)))


def _dot(a, b):
    return jnp.dot(a, b, preferred_element_type=F32)


def _dot_nt(a, b):
    return lax.dot_general(a, b, (((1,), (1,)), ((), ())), preferred_element_type=F32)


def _norm_proj_kernel(x_ref, g_ref, w_ref, cos_ref, sa_ref, sb_ref, o_ref, hn_ref, *,
                      n_col_tiles_per_part, q_scale, sub_rows, sub_cols):
    j = pl.program_id(1)
    npt = n_col_tiles_per_part

    @pl.when(j == 0)
    def _():
        hn_ref[...] = _rms(x_ref[...], g_ref[...]).astype(BF16)

    tm, tn = o_ref.shape
    mc, nc = min(tm, sub_rows), min(tn, sub_cols)
    is_rope = j < 2 * npt
    is_gate = j >= 3 * npt
    scale = jnp.where(j < npt, q_scale, 1.0).astype(F32)
    half = 16
    for mi in range(tm // mc):
        rows = slice(mi * mc, (mi + 1) * mc)
        c = jnp.where(is_rope, cos_ref[rows, :] * scale, 1.0)
        sa = jnp.where(is_rope, sa_ref[rows, :] * scale, 0.0)
        sb = jnp.where(is_rope, sb_ref[rows, :] * scale, 0.0)
        for ni in range(tn // nc):
            acc = _dot(hn_ref[rows, :], w_ref[:, ni * nc:(ni + 1) * nc])
            for blk in range(nc // LANES):
                t = acc[:, blk * LANES:(blk + 1) * LANES]
                r = t * c + pltpu.roll(t, half, 1) * sa + pltpu.roll(t, LANES - half, 1) * sb
                f = jnp.where(is_gate, 1.0 / (1.0 + jnp.exp(-t)), 1.0)
                col = ni * nc + blk * LANES
                o_ref[rows, col:col + LANES] = (r * f).astype(o_ref.dtype)


def _norm_proj(x, g, w, cos_t, sa_t, sb_t, *, tm, tn, rows_per_seq, q_scale):
    m, d = x.shape
    n = w.shape[1]
    tiles_per_seq = rows_per_seq // tm
    kern = functools.partial(_norm_proj_kernel, n_col_tiles_per_part=(n // 4) // tn,
                             q_scale=q_scale, sub_rows=PROJ_SUB_ROWS, sub_cols=PROJ_SUB_COLS)
    tab_spec = pl.BlockSpec((tm, LANES), lambda i, j: (i % tiles_per_seq, 0))
    return pl.pallas_call(
        kern,
        grid=(m // tm, n // tn),
        in_specs=[
            pl.BlockSpec((tm, d), lambda i, j: (i, 0)),
            pl.BlockSpec((1, d), lambda i, j: (0, 0)),
            pl.BlockSpec((d, tn), lambda i, j: (0, j)),
            tab_spec, tab_spec, tab_spec,
        ],
        out_specs=pl.BlockSpec((tm, tn), lambda i, j: (i, j)),
        out_shape=jax.ShapeDtypeStruct((m, n), BF16),
        scratch_shapes=[pltpu.VMEM((tm, d), BF16)],
        compiler_params=pltpu.CompilerParams(
            dimension_semantics=("parallel", "arbitrary"),
            vmem_limit_bytes=VMEM_LIMIT_BYTES),
        name="l0_norm_proj",
    )(x, g, w, cos_t, sa_t, sb_t)


def _lambda_full(lq1_ref, lk1_ref, lq2_ref, lk2_ref):
    a = jnp.sum(lq1_ref[...] * lk1_ref[...], axis=-1, keepdims=True)
    b = jnp.sum(lq2_ref[...] * lk2_ref[...], axis=-1, keepdims=True)
    return jnp.exp(a) - jnp.exp(b) + LAMBDA_INIT_L0


def _diff_finalize(acc1, acc2, inv_l1, inv_l2, lam, subg, gate):
    o = acc1 * inv_l1 - acc2 * (lam * inv_l2)
    o = _rms(o, subg * (1.0 - LAMBDA_INIT_L0))
    return (o * gate.astype(F32)).astype(BF16)


def _attn_kernel(q_ref, k_ref, v_ref, gate_ref, km_ref, vm_ref, lq1_ref, lk1_ref, lq2_ref,
                 lk2_ref, subg_ref, o_ref, m_ref, l_ref, acc_ref, *, tq, tk, dh, n_sub,
                 unroll):
    i = pl.program_id(2)
    ts = tq // n_sub
    dv = acc_ref.shape[2]

    def scores(c, kb):
        q = q_ref[c * ts:(c + 1) * ts, :]
        return jnp.concatenate([_dot_nt(q[:, :dh], kb[:, :dh]), _dot_nt(q[:, dh:], kb[:, dh:])],
                               axis=0)

    def lane_sum(p):
        psum = p[:, 0:LANES]
        for blk in range(1, p.shape[1] // LANES):
            psum = psum + p[:, blk * LANES:(blk + 1) * LANES]
        return psum

    row = lax.broadcasted_iota(jnp.int32, (2 * ts, tk), 0) % ts
    col = lax.broadcasted_iota(jnp.int32, (2 * ts, tk), 1)
    visible = col // CHUNK <= row // CHUNK
    for c in range(n_sub):
        off = pl.multiple_of(i * tq + c * ts, ts)
        s = jnp.where(visible, scores(c, k_ref[pl.ds(off, tk), :]), NEG_INF)
        m0 = jnp.max(s, axis=-1, keepdims=True)
        p = jnp.exp2(s - m0)
        m_ref[c] = jnp.broadcast_to(m0, (2 * ts, LANES))
        l_ref[c] = lane_sum(p)
        acc_ref[c] = _dot(p.astype(BF16), v_ref[pl.ds(off, tk), :])

    def update(c, s, vb):
        m_prev = m_ref[c]
        m_new = jnp.maximum(m_prev, jnp.max(s, axis=-1, keepdims=True))
        alpha = jnp.exp2(m_prev - m_new)
        p = jnp.exp2(s - jnp.tile(m_new, (1, s.shape[1] // LANES)))
        l_ref[c] = alpha * l_ref[c] + lane_sum(p)
        m_ref[c] = m_new
        acc_ref[c] = (acc_ref[c] * jnp.tile(alpha, (1, dv // LANES))
                      + _dot(p.astype(BF16), vb))

    def full_blocks(jb, carry):
        for u in range(unroll):
            off = pl.multiple_of((jb * unroll + u) * tk, tk)
            kb = k_ref[pl.ds(off, tk), :]
            vb = v_ref[pl.ds(off, tk), :]
            for c in range(n_sub):
                update(c, scores(c, kb), vb)
        return carry

    lax.fori_loop(0, i * (tq // tk // unroll), full_blocks, 0)

    for d in range(n_sub - 1):
        off = pl.multiple_of(i * tq + d * tk, tk)
        kb = k_ref[pl.ds(off, tk), :]
        vb = v_ref[pl.ds(off, tk), :]
        for c in range(d + 1, n_sub):
            update(c, scores(c, kb), vb)

    lam = _lambda_full(lq1_ref, lk1_ref, lq2_ref, lk2_ref)
    lane0 = lax.broadcasted_iota(jnp.int32, (2 * ts, LANES), 1) == 0
    for c in range(n_sub):
        s = scores(c, km_ref[...])
        m_prev = m_ref[c]
        m_new = jnp.maximum(m_prev, jnp.max(s, axis=-1, keepdims=True))
        alpha = jnp.exp2(m_prev - m_new)
        p = jnp.exp2(s - m_new[:, :s.shape[1]])
        l = alpha * l_ref[c] + jnp.where(lane0, jnp.sum(p, axis=-1, keepdims=True), 0.0)
        acc = (acc_ref[c] * jnp.tile(alpha, (1, dv // LANES)) + _dot(p.astype(BF16), vm_ref[...]))
        inv_l = 1.0 / jnp.sum(l, axis=-1, keepdims=True)
        o_ref[c * ts:(c + 1) * ts, :] = _diff_finalize(
            acc[:ts], acc[ts:], inv_l[:ts], inv_l[ts:], lam, subg_ref[...],
            gate_ref[c * ts:(c + 1) * ts, :])


def _attention(qkvg, qkvg_meta, lq1, lk1, lq2, lk2, subg, *, batch, seq, d_model):
    dv = d_model // DA_HEADS
    dh = dv // 2
    tq, tk, n_sub = ATTN_TQ, ATTN_TK, ATTN_SUBTILES
    ts = tq // n_sub
    unroll = ATTN_KEY_UNROLL
    assert ts % CHUNK == 0 and tk % CHUNK == 0 and tq % (tk * unroll) == 0
    nq = seq // tq
    h_ = DA_HEADS
    kern = functools.partial(_attn_kernel, tq=tq, tk=tk, dh=dh, n_sub=n_sub, unroll=unroll)
    vec = lambda n: pl.BlockSpec((1, n), lambda b, h, i: (0, 0))
    return pl.pallas_call(
        kern,
        grid=(batch, h_, nq),
        in_specs=[
            pl.BlockSpec((tq, dv), lambda b, h, i: (b * nq + i, h)),
            pl.BlockSpec((seq, dv), lambda b, h, i: (b, h_ + h)),
            pl.BlockSpec((seq, dv), lambda b, h, i: (b, 2 * h_ + h)),
            pl.BlockSpec((tq, dv), lambda b, h, i: (b * nq + i, 3 * h_ + h)),
            pl.BlockSpec((N_META, dv), lambda b, h, i: (0, h_ + h)),
            pl.BlockSpec((N_META, dv), lambda b, h, i: (0, 2 * h_ + h)),
            vec(dh), vec(dh), vec(dh), vec(dh), vec(dv),
        ],
        out_specs=pl.BlockSpec((tq, dv), lambda b, h, i: (b * nq + i, h)),
        out_shape=jax.ShapeDtypeStruct((batch * seq, d_model), BF16),
        scratch_shapes=[pltpu.VMEM((n_sub, 2 * ts, LANES), F32),
                        pltpu.VMEM((n_sub, 2 * ts, LANES), F32),
                        pltpu.VMEM((n_sub, 2 * ts, dv), F32)],
        compiler_params=pltpu.CompilerParams(
            dimension_semantics=("parallel", "parallel", "arbitrary"),
            vmem_limit_bytes=VMEM_LIMIT_BYTES),
        name="l0_diff_attn",
    )(qkvg, qkvg, qkvg, qkvg, qkvg_meta, qkvg_meta, lq1, lk1, lq2, lk2, subg)


def _meta_attn_kernel(q_ref, k_ref, v_ref, gate_ref, lq1_ref, lk1_ref, lq2_ref, lk2_ref,
                      subg_ref, o_ref, *, dh):
    q = q_ref[...]
    k = k_ref[...]
    v = v_ref[...]

    def attend(qm, km):
        s = _dot_nt(qm, km)
        p = jnp.exp2(s - jnp.max(s, axis=-1, keepdims=True))
        return _dot(p.astype(BF16), v), 1.0 / jnp.sum(p, axis=-1, keepdims=True)

    lam = _lambda_full(lq1_ref, lk1_ref, lq2_ref, lk2_ref)
    acc1, inv_l1 = attend(q[:, :dh], k[:, :dh])
    acc2, inv_l2 = attend(q[:, dh:], k[:, dh:])
    o_ref[...] = _diff_finalize(acc1, acc2, inv_l1, inv_l2, lam, subg_ref[...], gate_ref[...])


def _meta_attention(qkvg_meta, lq1, lk1, lq2, lk2, subg, *, d_model):
    dv = d_model // DA_HEADS
    dh = dv // 2
    h_ = DA_HEADS
    vec = lambda n: pl.BlockSpec((1, n), lambda h: (0, 0))
    blk = lambda part: pl.BlockSpec((N_META, dv), lambda h: (0, part * h_ + h))
    return pl.pallas_call(
        functools.partial(_meta_attn_kernel, dh=dh),
        grid=(h_,),
        in_specs=[blk(0), blk(1), blk(2), blk(3), vec(dh), vec(dh), vec(dh), vec(dh), vec(dv)],
        out_specs=pl.BlockSpec((N_META, dv), lambda h: (0, h)),
        out_shape=jax.ShapeDtypeStruct((N_META, d_model), BF16),
        compiler_params=pltpu.CompilerParams(dimension_semantics=("parallel",)),
        name="l0_meta_attn",
    )(qkvg_meta, qkvg_meta, qkvg_meta, qkvg_meta, lq1, lk1, lq2, lk2, subg)


def _out_proj_kernel(a_ref, w_ref, x_ref, g_ref, o_ref):
    y = _dot(a_ref[...], w_ref[...].astype(BF16))
    o_ref[...] = x_ref[...] + _rms(y, g_ref[...])


def _out_proj(a, w, x, g, *, tm):
    m, d = x.shape
    return pl.pallas_call(
        _out_proj_kernel,
        grid=(m // tm,),
        in_specs=[
            pl.BlockSpec((tm, d), lambda i: (i, 0)),
            pl.BlockSpec((d, d), lambda i: (0, 0), pipeline_mode=pl.Buffered(1)),
            pl.BlockSpec((tm, d), lambda i: (i, 0)),
            pl.BlockSpec((1, d), lambda i: (0, 0)),
        ],
        out_specs=pl.BlockSpec((tm, d), lambda i: (i, 0)),
        out_shape=jax.ShapeDtypeStruct((m, d), F32),
        compiler_params=pltpu.CompilerParams(
            dimension_semantics=("parallel",), vmem_limit_bytes=VMEM_LIMIT_BYTES),
        name="l0_out_proj",
    )(a, w, x, g)


def _load_as_bf16(src_hbm, col0, dst_ref, stage_ref, sem):
    rows, cols = dst_ref.shape
    chunk = stage_ref.shape[1]
    n = rows // chunk

    def copy(k, slot):
        return pltpu.make_async_copy(
            src_hbm.at[pl.ds(k * chunk, chunk), pl.ds(col0, cols)], stage_ref.at[slot],
            sem.at[slot])

    copy(0, 0).start()

    def body(k, carry):
        slot = k % 2

        @pl.when(k + 1 < n)
        def _():
            copy(k + 1, 1 - slot).start()

        copy(k, slot).wait()
        r0 = pl.multiple_of(k * chunk, chunk)
        dst_ref[pl.ds(r0, chunk), :] = stage_ref[slot].astype(dst_ref.dtype)
        return carry

    lax.fori_loop(0, n, body, 0)


def _tail_kernel(a_ref, x_ref, hm_ref, w_out0_hbm, post_g0_ref, pre_g_ref, w_in_hbm,
                 w_grp_ref, scale_ref, w_out_hbm, post_g_ref, o_ref, w_out0_ref, w_in_u_ref,
                 w_in_g_ref, w_out_ref, stage_ref, sem, u_ext_ref, z_ref, *, tm, d, halo):
    i = pl.program_id(1)
    pre_g = pre_g_ref[...]

    @pl.when(jnp.logical_and(pl.program_id(0) == 0, i == 0))
    def _():
        _load_as_bf16(w_out0_hbm, 0, w_out0_ref, stage_ref, sem)
        _load_as_bf16(w_in_hbm, 0, w_in_u_ref, stage_ref, sem)
        _load_as_bf16(w_in_hbm, d, w_in_g_ref, stage_ref, sem)
        _load_as_bf16(w_out_hbm, 0, w_out_ref, stage_ref, sem)

    @pl.when(i == 0)
    def _():
        hmn = _rms(hm_ref[...], pre_g).astype(BF16)
        u_ext_ref[0:halo, :] = _dot(hmn, w_in_u_ref[...])

    @pl.when(i > 0)
    def _():
        u_ext_ref[0:halo, :] = u_ext_ref[tm:tm + halo, :]

    h = x_ref[...] + _rms(_dot(a_ref[...], w_out0_ref[...]), post_g0_ref[...])
    o_ref[...] = h
    hn = _rms(h, pre_g).astype(BF16)
    u_ext_ref[halo:halo + tm, :] = _dot(hn, w_in_u_ref[...])

    gsz = d // len(POOL_WINDOWS)
    for gi, w in enumerate(POOL_WINDOWS):
        c0 = gi * gsz
        u = u_ext_ref[halo:halo + tm, c0:c0 + gsz]
        s = u
        for back in range(1, w):
            s = s + u_ext_ref[halo - back:halo - back + tm, c0:c0 + gsz]
        mixed = (s * (1.0 / w) - u).astype(BF16)
        mo = _dot(mixed, w_grp_ref[gi]) * scale_ref[:, c0:c0 + gsz]
        gate = _dot(hn, w_in_g_ref[:, c0:c0 + gsz])
        z_ref[:, c0:c0 + gsz] = (mo * _silu(gate)).astype(BF16)

    y = _dot(z_ref[...], w_out_ref[...])
    o_ref[...] = o_ref[...] + _rms(y, post_g_ref[...])


def _tail(a, x, h_meta, w_out0, post_g0, pre_g, w_in, w_grp, scale, w_out, post_g, *, batch,
          seq, tm):
    m, d = x.shape
    nt = seq // tm
    halo = N_META
    assert max(POOL_WINDOWS) <= halo and halo <= tm
    const2 = lambda b, i: (0, 0)
    row_tile = lambda b, i: (b * nt + i, 0)
    resident = lambda shape, imap: pl.BlockSpec(shape, imap, pipeline_mode=pl.Buffered(1))
    in_hbm = pl.BlockSpec(memory_space=pl.ANY)
    kern = functools.partial(_tail_kernel, tm=tm, d=d, halo=halo)
    return pl.pallas_call(
        kern,
        grid=(batch, nt),
        in_specs=[
            pl.BlockSpec((tm, d), row_tile),
            pl.BlockSpec((tm, d), row_tile),
            resident((N_META, d), const2),
            in_hbm,
            pl.BlockSpec((1, d), const2),
            pl.BlockSpec((1, d), const2),
            in_hbm,
            resident(w_grp.shape, lambda b, i: (0, 0, 0)),
            pl.BlockSpec((1, d), const2),
            in_hbm,
            pl.BlockSpec((1, d), const2),
        ],
        out_specs=pl.BlockSpec((tm, d), row_tile),
        out_shape=jax.ShapeDtypeStruct((m, d), F32),
        scratch_shapes=[pltpu.VMEM((d, d), BF16), pltpu.VMEM((d, d), BF16),
                        pltpu.VMEM((d, d), BF16), pltpu.VMEM((d, d), BF16),
                        pltpu.VMEM((2, WEIGHT_STAGE_ROWS, d), F32),
                        pltpu.SemaphoreType.DMA((2,)),
                        pltpu.VMEM((halo + tm, d), F32), pltpu.VMEM((tm, d), BF16)],
        compiler_params=pltpu.CompilerParams(
            dimension_semantics=("arbitrary", "arbitrary"),
            vmem_limit_bytes=VMEM_LIMIT_BYTES),
        name="l0_out_l1_pool",
    )(a, x, h_meta, w_out0, post_g0, pre_g, w_in, w_grp, scale, w_out, post_g)


def _rope_tables(length, rope_dim):
    half = rope_dim // 2
    pos = np.arange(length, dtype=np.float32)
    inv = np.float32(ROPE_THETA) ** (-np.arange(0, rope_dim, 2, dtype=np.float32)
                                     / np.float32(rope_dim))
    ang = pos[:, None] * inv[None, :]
    cos, sin = np.cos(ang), np.sin(ang)
    zeros = lambda n: np.zeros((length, n), np.float32)
    cos_t = np.concatenate([cos, cos, np.ones((length, LANES - rope_dim), np.float32)], axis=1)
    sa_t = np.concatenate([zeros(half), sin, zeros(LANES - rope_dim)], axis=1)
    sb_t = np.concatenate([-sin, zeros(LANES - half)], axis=1)
    return cos_t, sa_t, sb_t


def kernel(x, meta_tokens, pre_norm_g, post_norm_g, attn_w_in, attn_w_out, attn_lambda_q1,
           attn_lambda_k1, attn_lambda_q2, attn_lambda_k2, attn_subln_g, pool_w_in,
           pool_w_group, pool_scale, pool_w_out):
    batch, seq, d = x.shape
    dv = d // DA_HEADS
    dh = dv // 2
    assert dh == LANES and seq % CHUNK == 0
    xf = x.reshape(batch * seq, d)
    meta = meta_tokens.astype(x.dtype)

    cos_t, sa_t, sb_t = _rope_tables(N_META + seq, dh // 4)
    w_in0 = attn_w_in[0].astype(BF16)
    w_out0 = attn_w_out[0]
    pre_g0, pre_g1 = pre_norm_g[0:1], pre_norm_g[1:2]
    post_g0, post_g1 = post_norm_g[0:1], post_norm_g[1:2]
    q_scale = float(dh) ** -0.5 * math.log2(math.e)
    lam = [p[0:1] for p in (attn_lambda_q1, attn_lambda_k1, attn_lambda_q2, attn_lambda_k2)]
    subg = attn_subln_g[0:1]

    proj = functools.partial(_norm_proj, tn=PROJ_TN, q_scale=q_scale)
    qkvg_meta = proj(meta, pre_g0, w_in0, cos_t[:N_META], sa_t[:N_META], sb_t[:N_META],
                     tm=N_META, rows_per_seq=N_META)
    qkvg = proj(xf, pre_g0, w_in0, cos_t[N_META:], sa_t[N_META:], sb_t[N_META:],
                tm=PROJ_TM, rows_per_seq=seq)
    a_meta = _meta_attention(qkvg_meta, *lam, subg, d_model=d)
    a = _attention(qkvg, qkvg_meta, *lam, subg, batch=batch, seq=seq, d_model=d)
    h1_meta = _out_proj(a_meta, w_out0, meta, post_g0, tm=N_META)

    out = _tail(a, xf, h1_meta, w_out0, post_g0, pre_g1, pool_w_in[0],
                pool_w_group[0].astype(BF16), pool_scale[0:1], pool_w_out[0],
                post_g1, batch=batch, seq=seq, tm=POOL_TM)
    return out.reshape(batch, seq, d)
```

```python
import functools
import math

import jax
import jax.numpy as jnp
import numpy as np
from jax import lax
from jax.experimental import pallas as pl
from jax.experimental.pallas import tpu as pltpu

F32 = jnp.float32
BF16 = jnp.bfloat16

N_META = 16
CHUNK = 64
RMS_EPS = 1e-6
DA_HEADS = 8
ROPE_THETA = 500000.0
POOL_WINDOWS = (2, 4, 8, 16)
LAMBDA_INIT_L0 = 0.8 - 0.6 * math.exp(-0.3 * 0)

LANES = 128
VMEM_LIMIT_BYTES = 56 * 1024 * 1024

PROJ_TM = 1024
PROJ_TN = 2048
PROJ_SUB_ROWS = 256
PROJ_SUB_COLS = 256
ATTN_TQ = 2048
ATTN_TK = 256
ATTN_SUBTILES = 8
ATTN_KEY_UNROLL = 4
WEIGHT_STAGE_ROWS = 32
WEIGHT_STAGE_SLOTS = 8
POOL_TM = 256

NEG_INF = float(jnp.finfo(jnp.float32).min)


def _rms(x, g):
    ms = jnp.mean(x * x, axis=-1, keepdims=True)
    return x * lax.rsqrt(ms + RMS_EPS) * g


def _silu(g):
    return g * (1.0 / (1.0 + jnp.exp(-g)))


def _dot(a, b):
    return jnp.dot(a, b, preferred_element_type=F32)


def _dot_nt(a, b):
    return lax.dot_general(a, b, (((1,), (1,)), ((), ())), preferred_element_type=F32)


def _norm_proj_kernel(x_ref, g_ref, w_ref, cos_ref, sa_ref, sb_ref, o_ref, hn_ref, *,
                      n_col_tiles_per_part, q_scale, sub_rows, sub_cols):
    j = pl.program_id(1)
    npt = n_col_tiles_per_part

    @pl.when(j == 0)
    def _():
        hn_ref[...] = _rms(x_ref[...], g_ref[...]).astype(BF16)

    tm, tn = o_ref.shape
    mc, nc = min(tm, sub_rows), min(tn, sub_cols)
    is_rope = j < 2 * npt
    is_gate = j >= 3 * npt
    scale = jnp.where(j < npt, q_scale, 1.0).astype(F32)
    half = 16
    for mi in range(tm // mc):
        rows = slice(mi * mc, (mi + 1) * mc)
        c = jnp.where(is_rope, cos_ref[rows, :] * scale, 1.0)
        sa = jnp.where(is_rope, sa_ref[rows, :] * scale, 0.0)
        sb = jnp.where(is_rope, sb_ref[rows, :] * scale, 0.0)
        for ni in range(tn // nc):
            acc = _dot(hn_ref[rows, :], w_ref[:, ni * nc:(ni + 1) * nc])
            for blk in range(nc // LANES):
                t = acc[:, blk * LANES:(blk + 1) * LANES]
                r = t * c + pltpu.roll(t, half, 1) * sa + pltpu.roll(t, LANES - half, 1) * sb
                f = jnp.where(is_gate, 1.0 / (1.0 + jnp.exp(-t)), 1.0)
                col = ni * nc + blk * LANES
                o_ref[rows, col:col + LANES] = (r * f).astype(o_ref.dtype)


def _norm_proj(x, g, w, cos_t, sa_t, sb_t, *, tm, tn, rows_per_seq, q_scale):
    m, d = x.shape
    n = w.shape[1]
    tiles_per_seq = rows_per_seq // tm
    kern = functools.partial(_norm_proj_kernel, n_col_tiles_per_part=(n // 4) // tn,
                             q_scale=q_scale, sub_rows=PROJ_SUB_ROWS, sub_cols=PROJ_SUB_COLS)
    tab_spec = pl.BlockSpec((tm, LANES), lambda i, j: (i % tiles_per_seq, 0))
    return pl.pallas_call(
        kern,
        grid=(m // tm, n // tn),
        in_specs=[
            pl.BlockSpec((tm, d), lambda i, j: (i, 0)),
            pl.BlockSpec((1, d), lambda i, j: (0, 0)),
            pl.BlockSpec((d, tn), lambda i, j: (0, j)),
            tab_spec, tab_spec, tab_spec,
        ],
        out_specs=pl.BlockSpec((tm, tn), lambda i, j: (i, j)),
        out_shape=jax.ShapeDtypeStruct((m, n), BF16),
        scratch_shapes=[pltpu.VMEM((tm, d), BF16)],
        compiler_params=pltpu.CompilerParams(
            dimension_semantics=("parallel", "arbitrary"),
            vmem_limit_bytes=VMEM_LIMIT_BYTES),
        name="l0_norm_proj",
    )(x, g, w, cos_t, sa_t, sb_t)


def _lambda_full(lq1_ref, lk1_ref, lq2_ref, lk2_ref):
    a = jnp.sum(lq1_ref[...] * lk1_ref[...], axis=-1, keepdims=True)
    b = jnp.sum(lq2_ref[...] * lk2_ref[...], axis=-1, keepdims=True)
    return jnp.exp(a) - jnp.exp(b) + LAMBDA_INIT_L0


def _diff_finalize(acc1, acc2, inv_l1, inv_l2, lam, subg, gate):
    o = acc1 * inv_l1 - acc2 * (lam * inv_l2)
    o = _rms(o, subg * (1.0 - LAMBDA_INIT_L0))
    return (o * gate.astype(F32)).astype(BF16)


def _attn_kernel(q_ref, k_ref, v_ref, gate_ref, km_ref, vm_ref, lq1_ref, lk1_ref, lq2_ref,
                 lk2_ref, subg_ref, o_ref, m_ref, l_ref, acc_ref, *, tq, tk, dh, n_sub,
                 unroll):
    i = pl.program_id(2)
    ts = tq // n_sub
    dv = acc_ref.shape[2]

    def scores(c, kb):
        q = q_ref[c * ts:(c + 1) * ts, :]
        return jnp.concatenate([_dot_nt(q[:, :dh], kb[:, :dh]), _dot_nt(q[:, dh:], kb[:, dh:])],
                               axis=0)

    def lane_sum(p):
        psum = p[:, 0:LANES]
        for blk in range(1, p.shape[1] // LANES):
            psum = psum + p[:, blk * LANES:(blk + 1) * LANES]
        return psum

    row = lax.broadcasted_iota(jnp.int32, (2 * ts, tk), 0) % ts
    col = lax.broadcasted_iota(jnp.int32, (2 * ts, tk), 1)
    visible = col // CHUNK <= row // CHUNK
    for c in range(n_sub):
        off = pl.multiple_of(i * tq + c * ts, ts)
        s = jnp.where(visible, scores(c, k_ref[pl.ds(off, tk), :]), NEG_INF)
        m0 = jnp.max(s, axis=-1, keepdims=True)
        p = jnp.exp2(s - m0)
        m_ref[c] = jnp.broadcast_to(m0, (2 * ts, LANES))
        l_ref[c] = lane_sum(p)
        acc_ref[c] = _dot(p.astype(BF16), v_ref[pl.ds(off, tk), :])

    def update(c, s, vb):
        m_prev = m_ref[c]
        m_new = jnp.maximum(m_prev, jnp.max(s, axis=-1, keepdims=True))
        alpha = jnp.exp2(m_prev - m_new)
        p = jnp.exp2(s - jnp.tile(m_new, (1, s.shape[1] // LANES)))
        l_ref[c] = alpha * l_ref[c] + lane_sum(p)
        m_ref[c] = m_new
        acc_ref[c] = (acc_ref[c] * jnp.tile(alpha, (1, dv // LANES))
                      + _dot(p.astype(BF16), vb))

    def full_blocks(jb, carry):
        for u in range(unroll):
            off = pl.multiple_of((jb * unroll + u) * tk, tk)
            kb = k_ref[pl.ds(off, tk), :]
            vb = v_ref[pl.ds(off, tk), :]
            for c in range(n_sub):
                update(c, scores(c, kb), vb)
        return carry

    lax.fori_loop(0, i * (tq // tk // unroll), full_blocks, 0)

    for d in range(n_sub - 1):
        off = pl.multiple_of(i * tq + d * tk, tk)
        kb = k_ref[pl.ds(off, tk), :]
        vb = v_ref[pl.ds(off, tk), :]
        for c in range(d + 1, n_sub):
            update(c, scores(c, kb), vb)

    lam = _lambda_full(lq1_ref, lk1_ref, lq2_ref, lk2_ref)
    lane0 = lax.broadcasted_iota(jnp.int32, (2 * ts, LANES), 1) == 0
    for c in range(n_sub):
        s = scores(c, km_ref[...])
        m_prev = m_ref[c]
        m_new = jnp.maximum(m_prev, jnp.max(s, axis=-1, keepdims=True))
        alpha = jnp.exp2(m_prev - m_new)
        p = jnp.exp2(s - m_new[:, :s.shape[1]])
        l = alpha * l_ref[c] + jnp.where(lane0, jnp.sum(p, axis=-1, keepdims=True), 0.0)
        acc = (acc_ref[c] * jnp.tile(alpha, (1, dv // LANES)) + _dot(p.astype(BF16), vm_ref[...]))
        inv_l = 1.0 / jnp.sum(l, axis=-1, keepdims=True)
        o_ref[c * ts:(c + 1) * ts, :] = _diff_finalize(
            acc[:ts], acc[ts:], inv_l[:ts], inv_l[ts:], lam, subg_ref[...],
            gate_ref[c * ts:(c + 1) * ts, :])


def _attention(qkvg, qkvg_meta, lq1, lk1, lq2, lk2, subg, *, batch, seq, d_model):
    dv = d_model // DA_HEADS
    dh = dv // 2
    tq, tk, n_sub = ATTN_TQ, ATTN_TK, ATTN_SUBTILES
    ts = tq // n_sub
    unroll = ATTN_KEY_UNROLL
    assert ts % CHUNK == 0 and tk % CHUNK == 0 and tq % (tk * unroll) == 0
    nq = seq // tq
    h_ = DA_HEADS
    kern = functools.partial(_attn_kernel, tq=tq, tk=tk, dh=dh, n_sub=n_sub, unroll=unroll)
    vec = lambda n: pl.BlockSpec((1, n), lambda b, h, i: (0, 0))
    return pl.pallas_call(
        kern,
        grid=(batch, h_, nq),
        in_specs=[
            pl.BlockSpec((tq, dv), lambda b, h, i: (b * nq + i, h)),
            pl.BlockSpec((seq, dv), lambda b, h, i: (b, h_ + h)),
            pl.BlockSpec((seq, dv), lambda b, h, i: (b, 2 * h_ + h)),
            pl.BlockSpec((tq, dv), lambda b, h, i: (b * nq + i, 3 * h_ + h)),
            pl.BlockSpec((N_META, dv), lambda b, h, i: (0, h_ + h)),
            pl.BlockSpec((N_META, dv), lambda b, h, i: (0, 2 * h_ + h)),
            vec(dh), vec(dh), vec(dh), vec(dh), vec(dv),
        ],
        out_specs=pl.BlockSpec((tq, dv), lambda b, h, i: (b * nq + i, h)),
        out_shape=jax.ShapeDtypeStruct((batch * seq, d_model), BF16),
        scratch_shapes=[pltpu.VMEM((n_sub, 2 * ts, LANES), F32),
                        pltpu.VMEM((n_sub, 2 * ts, LANES), F32),
                        pltpu.VMEM((n_sub, 2 * ts, dv), F32)],
        compiler_params=pltpu.CompilerParams(
            dimension_semantics=("parallel", "parallel", "arbitrary"),
            vmem_limit_bytes=VMEM_LIMIT_BYTES),
        name="l0_diff_attn",
    )(qkvg, qkvg, qkvg, qkvg, qkvg_meta, qkvg_meta, lq1, lk1, lq2, lk2, subg)


def _meta_attn_kernel(q_ref, k_ref, v_ref, gate_ref, lq1_ref, lk1_ref, lq2_ref, lk2_ref,
                      subg_ref, o_ref, *, dh):
    q = q_ref[...]
    k = k_ref[...]
    v = v_ref[...]

    def attend(qm, km):
        s = _dot_nt(qm, km)
        p = jnp.exp2(s - jnp.max(s, axis=-1, keepdims=True))
        return _dot(p.astype(BF16), v), 1.0 / jnp.sum(p, axis=-1, keepdims=True)

    lam = _lambda_full(lq1_ref, lk1_ref, lq2_ref, lk2_ref)
    acc1, inv_l1 = attend(q[:, :dh], k[:, :dh])
    acc2, inv_l2 = attend(q[:, dh:], k[:, dh:])
    o_ref[...] = _diff_finalize(acc1, acc2, inv_l1, inv_l2, lam, subg_ref[...], gate_ref[...])


def _meta_attention(qkvg_meta, lq1, lk1, lq2, lk2, subg, *, d_model):
    dv = d_model // DA_HEADS
    dh = dv // 2
    h_ = DA_HEADS
    vec = lambda n: pl.BlockSpec((1, n), lambda h: (0, 0))
    blk = lambda part: pl.BlockSpec((N_META, dv), lambda h: (0, part * h_ + h))
    return pl.pallas_call(
        functools.partial(_meta_attn_kernel, dh=dh),
        grid=(h_,),
        in_specs=[blk(0), blk(1), blk(2), blk(3), vec(dh), vec(dh), vec(dh), vec(dh), vec(dv)],
        out_specs=pl.BlockSpec((N_META, dv), lambda h: (0, h)),
        out_shape=jax.ShapeDtypeStruct((N_META, d_model), BF16),
        compiler_params=pltpu.CompilerParams(dimension_semantics=("parallel",)),
        name="l0_meta_attn",
    )(qkvg_meta, qkvg_meta, qkvg_meta, qkvg_meta, lq1, lk1, lq2, lk2, subg)


def _out_proj_kernel(a_ref, w_ref, x_ref, g_ref, o_ref):
    y = _dot(a_ref[...], w_ref[...].astype(BF16))
    o_ref[...] = x_ref[...] + _rms(y, g_ref[...])


def _out_proj(a, w, x, g, *, tm):
    m, d = x.shape
    return pl.pallas_call(
        _out_proj_kernel,
        grid=(m // tm,),
        in_specs=[
            pl.BlockSpec((tm, d), lambda i: (i, 0)),
            pl.BlockSpec((d, d), lambda i: (0, 0), pipeline_mode=pl.Buffered(1)),
            pl.BlockSpec((tm, d), lambda i: (i, 0)),
            pl.BlockSpec((1, d), lambda i: (0, 0)),
        ],
        out_specs=pl.BlockSpec((tm, d), lambda i: (i, 0)),
        out_shape=jax.ShapeDtypeStruct((m, d), F32),
        compiler_params=pltpu.CompilerParams(
            dimension_semantics=("parallel",), vmem_limit_bytes=VMEM_LIMIT_BYTES),
        name="l0_out_proj",
    )(a, w, x, g)


def _load_as_bf16(src_hbm, col0, dst_ref, stage_ref, sem):
    rows, cols = dst_ref.shape
    slots, chunk = stage_ref.shape[0], stage_ref.shape[1]
    n = rows // chunk
    assert rows % chunk == 0 and n >= slots

    def copy(k, slot):
        return pltpu.make_async_copy(
            src_hbm.at[pl.ds(k * chunk, chunk), pl.ds(col0, cols)], stage_ref.at[slot],
            sem.at[slot])

    for k in range(slots - 1):
        copy(k, k).start()

    def body(k, carry):
        slot = k % slots
        ahead = k + slots - 1

        @pl.when(ahead < n)
        def _():
            copy(ahead, ahead % slots).start()

        copy(k, slot).wait()
        r0 = pl.multiple_of(k * chunk, chunk)
        dst_ref[pl.ds(r0, chunk), :] = stage_ref[slot].astype(dst_ref.dtype)
        return carry

    lax.fori_loop(0, n, body, 0)


def _tail_kernel(a_ref, x_ref, hm_ref, w_out0_hbm, post_g0_ref, pre_g_ref, w_in_hbm,
                 w_grp_ref, scale_ref, w_out_hbm, post_g_ref, o_ref, w_out0_ref, w_in_u_ref,
                 w_in_g_ref, w_out_ref, stage_ref, sem, u_ext_ref, z_ref, *, tm, d, halo):
    i = pl.program_id(1)
    pre_g = pre_g_ref[...]

    @pl.when(jnp.logical_and(pl.program_id(0) == 0, i == 0))
    def _():
        _load_as_bf16(w_out0_hbm, 0, w_out0_ref, stage_ref, sem)
        _load_as_bf16(w_in_hbm, 0, w_in_u_ref, stage_ref, sem)
        _load_as_bf16(w_in_hbm, d, w_in_g_ref, stage_ref, sem)
        _load_as_bf16(w_out_hbm, 0, w_out_ref, stage_ref, sem)

    @pl.when(i == 0)
    def _():
        hmn = _rms(hm_ref[...], pre_g).astype(BF16)
        u_ext_ref[0:halo, :] = _dot(hmn, w_in_u_ref[...])

    @pl.when(i > 0)
    def _():
        u_ext_ref[0:halo, :] = u_ext_ref[tm:tm + halo, :]

    h = x_ref[...] + _rms(_dot(a_ref[...], w_out0_ref[...]), post_g0_ref[...])
    o_ref[...] = h
    hn = _rms(h, pre_g).astype(BF16)
    u_ext_ref[halo:halo + tm, :] = _dot(hn, w_in_u_ref[...])

    gsz = d // len(POOL_WINDOWS)
    for gi, w in enumerate(POOL_WINDOWS):
        c0 = gi * gsz
        u = u_ext_ref[halo:halo + tm, c0:c0 + gsz]
        s = u
        for back in range(1, w):
            s = s + u_ext_ref[halo - back:halo - back + tm, c0:c0 + gsz]
        mixed = (s * (1.0 / w) - u).astype(BF16)
        mo = _dot(mixed, w_grp_ref[gi]) * scale_ref[:, c0:c0 + gsz]
        gate = _dot(hn, w_in_g_ref[:, c0:c0 + gsz])
        z_ref[:, c0:c0 + gsz] = (mo * _silu(gate)).astype(BF16)

    y = _dot(z_ref[...], w_out_ref[...])
    o_ref[...] = o_ref[...] + _rms(y, post_g_ref[...])


def _tail(a, x, h_meta, w_out0, post_g0, pre_g, w_in, w_grp, scale, w_out, post_g, *, batch,
          seq, tm):
    m, d = x.shape
    nt = seq // tm
    halo = N_META
    assert max(POOL_WINDOWS) <= halo and halo <= tm
    const2 = lambda b, i: (0, 0)
    row_tile = lambda b, i: (b * nt + i, 0)
    resident = lambda shape, imap: pl.BlockSpec(shape, imap, pipeline_mode=pl.Buffered(1))
    in_hbm = pl.BlockSpec(memory_space=pl.ANY)
    kern = functools.partial(_tail_kernel, tm=tm, d=d, halo=halo)
    return pl.pallas_call(
        kern,
        grid=(batch, nt),
        in_specs=[
            pl.BlockSpec((tm, d), row_tile),
            pl.BlockSpec((tm, d), row_tile),
            resident((N_META, d), const2),
            in_hbm,
            pl.BlockSpec((1, d), const2),
            pl.BlockSpec((1, d), const2),
            in_hbm,
            resident(w_grp.shape, lambda b, i: (0, 0, 0)),
            pl.BlockSpec((1, d), const2),
            in_hbm,
            pl.BlockSpec((1, d), const2),
        ],
        out_specs=pl.BlockSpec((tm, d), row_tile),
        out_shape=jax.ShapeDtypeStruct((m, d), F32),
        scratch_shapes=[pltpu.VMEM((d, d), BF16), pltpu.VMEM((d, d), BF16),
                        pltpu.VMEM((d, d), BF16), pltpu.VMEM((d, d), BF16),
                        pltpu.VMEM((WEIGHT_STAGE_SLOTS, WEIGHT_STAGE_ROWS, d), F32),
                        pltpu.SemaphoreType.DMA((WEIGHT_STAGE_SLOTS,)),
                        pltpu.VMEM((halo + tm, d), F32), pltpu.VMEM((tm, d), BF16)],
        compiler_params=pltpu.CompilerParams(
            dimension_semantics=("arbitrary", "arbitrary"),
            vmem_limit_bytes=VMEM_LIMIT_BYTES),
        name="l0_out_l1_pool",
    )(a, x, h_meta, w_out0, post_g0, pre_g, w_in, w_grp, scale, w_out, post_g)


def _rope_tables(length, rope_dim):
    half = rope_dim // 2
    pos = np.arange(length, dtype=np.float32)
    inv = np.float32(ROPE_THETA) ** (-np.arange(0, rope_dim, 2, dtype=np.float32)
                                     / np.float32(rope_dim))
    ang = pos[:, None] * inv[None, :]
    cos, sin = np.cos(ang), np.sin(ang)
    zeros = lambda n: np.zeros((length, n), np.float32)
    cos_t = np.concatenate([cos, cos, np.ones((length, LANES - rope_dim), np.float32)], axis=1)
    sa_t = np.concatenate([zeros(half), sin, zeros(LANES - rope_dim)], axis=1)
    sb_t = np.concatenate([-sin, zeros(LANES - half)], axis=1)
    return cos_t, sa_t, sb_t


def kernel(x, meta_tokens, pre_norm_g, post_norm_g, attn_w_in, attn_w_out, attn_lambda_q1,
           attn_lambda_k1, attn_lambda_q2, attn_lambda_k2, attn_subln_g, pool_w_in,
           pool_w_group, pool_scale, pool_w_out):
    batch, seq, d = x.shape
    dv = d // DA_HEADS
    dh = dv // 2
    assert dh == LANES and seq % CHUNK == 0
    xf = x.reshape(batch * seq, d)
    meta = meta_tokens.astype(x.dtype)

    cos_t, sa_t, sb_t = _rope_tables(N_META + seq, dh // 4)
    w_in0 = attn_w_in[0].astype(BF16)
    w_out0 = attn_w_out[0]
    pre_g0, pre_g1 = pre_norm_g[0:1], pre_norm_g[1:2]
    post_g0, post_g1 = post_norm_g[0:1], post_norm_g[1:2]
    q_scale = float(dh) ** -0.5 * math.log2(math.e)
    lam = [p[0:1] for p in (attn_lambda_q1, attn_lambda_k1, attn_lambda_q2, attn_lambda_k2)]
    subg = attn_subln_g[0:1]

    proj = functools.partial(_norm_proj, tn=PROJ_TN, q_scale=q_scale)
    qkvg_meta = proj(meta, pre_g0, w_in0, cos_t[:N_META], sa_t[:N_META], sb_t[:N_META],
                     tm=N_META, rows_per_seq=N_META)
    qkvg = proj(xf, pre_g0, w_in0, cos_t[N_META:], sa_t[N_META:], sb_t[N_META:],
                tm=PROJ_TM, rows_per_seq=seq)
    a_meta = _meta_attention(qkvg_meta, *lam, subg, d_model=d)
    a = _attention(qkvg, qkvg_meta, *lam, subg, batch=batch, seq=seq, d_model=d)
    h1_meta = _out_proj(a_meta, w_out0, meta, post_g0, tm=N_META)

    out = _tail(a, xf, h1_meta, w_out0, post_g0, pre_g1, pool_w_in[0],
                pool_w_group[0].astype(BF16), pool_scale[0:1], pool_w_out[0],
                post_g1, batch=batch, seq=seq, tm=POOL_TM)
    return out.reshape(batch, seq, d)
```

```python
import functools
import math

import jax
import jax.numpy as jnp
import numpy as np
from jax import lax
from jax.experimental import pallas as pl
from jax.experimental.pallas import tpu as pltpu

F32 = jnp.float32
BF16 = jnp.bfloat16

N_META = 16
CHUNK = 64
RMS_EPS = 1e-6
DA_HEADS = 8
ROPE_THETA = 500000.0
POOL_WINDOWS = (2, 4, 8, 16)
LAMBDA_INIT_L0 = 0.8 - 0.6 * math.exp(-0.3 * 0)

LANES = 128
VMEM_LIMIT_BYTES = 56 * 1024 * 1024

PROJ_TM = 1024
PROJ_TN = 2048
PROJ_SUB_ROWS = 256
PROJ_SUB_COLS = 256
ATTN_TQ = 2048
ATTN_TK = 256
ATTN_SUBTILES = 8
ATTN_KEY_UNROLL = 4
POOL_TM = 256

NEG_INF = float(jnp.finfo(jnp.float32).min)


def _rms(x, g):
    ms = jnp.mean(x * x, axis=-1, keepdims=True)
    return x * lax.rsqrt(ms + RMS_EPS) * g


def _silu(g):
    return g * (1.0 / (1.0 + jnp.exp(-g)))


def _dot(a, b):
    return jnp.dot(a, b, preferred_element_type=F32)


def _dot_nt(a, b):
    return lax.dot_general(a, b, (((1,), (1,)), ((), ())), preferred_element_type=F32)


def _norm_proj_kernel(x_ref, g_ref, w_ref, cos_ref, sa_ref, sb_ref, o_ref, hn_ref, *,
                      n_col_tiles_per_part, q_scale, sub_rows, sub_cols):
    j = pl.program_id(1)
    npt = n_col_tiles_per_part

    @pl.when(j == 0)
    def _():
        hn_ref[...] = _rms(x_ref[...], g_ref[...]).astype(BF16)

    tm, tn = o_ref.shape
    mc, nc = min(tm, sub_rows), min(tn, sub_cols)
    is_rope = j < 2 * npt
    is_gate = j >= 3 * npt
    scale = jnp.where(j < npt, q_scale, 1.0).astype(F32)
    half = 16
    for mi in range(tm // mc):
        rows = slice(mi * mc, (mi + 1) * mc)
        c = jnp.where(is_rope, cos_ref[rows, :] * scale, 1.0)
        sa = jnp.where(is_rope, sa_ref[rows, :] * scale, 0.0)
        sb = jnp.where(is_rope, sb_ref[rows, :] * scale, 0.0)
        for ni in range(tn // nc):
            acc = _dot(hn_ref[rows, :], w_ref[:, ni * nc:(ni + 1) * nc])
            for blk in range(nc // LANES):
                t = acc[:, blk * LANES:(blk + 1) * LANES]
                r = t * c + pltpu.roll(t, half, 1) * sa + pltpu.roll(t, LANES - half, 1) * sb
                f = jnp.where(is_gate, 1.0 / (1.0 + jnp.exp(-t)), 1.0)
                col = ni * nc + blk * LANES
                o_ref[rows, col:col + LANES] = (r * f).astype(o_ref.dtype)


def _norm_proj(x, g, w, cos_t, sa_t, sb_t, *, tm, tn, rows_per_seq, q_scale):
    m, d = x.shape
    n = w.shape[1]
    tiles_per_seq = rows_per_seq // tm
    kern = functools.partial(_norm_proj_kernel, n_col_tiles_per_part=(n // 4) // tn,
                             q_scale=q_scale, sub_rows=PROJ_SUB_ROWS, sub_cols=PROJ_SUB_COLS)
    tab_spec = pl.BlockSpec((tm, LANES), lambda i, j: (i % tiles_per_seq, 0))
    return pl.pallas_call(
        kern,
        grid=(m // tm, n // tn),
        in_specs=[
            pl.BlockSpec((tm, d), lambda i, j: (i, 0)),
            pl.BlockSpec((1, d), lambda i, j: (0, 0)),
            pl.BlockSpec((d, tn), lambda i, j: (0, j)),
            tab_spec, tab_spec, tab_spec,
        ],
        out_specs=pl.BlockSpec((tm, tn), lambda i, j: (i, j)),
        out_shape=jax.ShapeDtypeStruct((m, n), BF16),
        scratch_shapes=[pltpu.VMEM((tm, d), BF16)],
        compiler_params=pltpu.CompilerParams(
            dimension_semantics=("parallel", "arbitrary"),
            vmem_limit_bytes=VMEM_LIMIT_BYTES),
        name="l0_norm_proj",
    )(x, g, w, cos_t, sa_t, sb_t)


def _lambda_full(lq1_ref, lk1_ref, lq2_ref, lk2_ref):
    a = jnp.sum(lq1_ref[...] * lk1_ref[...], axis=-1, keepdims=True)
    b = jnp.sum(lq2_ref[...] * lk2_ref[...], axis=-1, keepdims=True)
    return jnp.exp(a) - jnp.exp(b) + LAMBDA_INIT_L0


def _diff_finalize(acc1, acc2, inv_l1, inv_l2, lam, subg, gate):
    o = acc1 * inv_l1 - acc2 * (lam * inv_l2)
    o = _rms(o, subg * (1.0 - LAMBDA_INIT_L0))
    return (o * gate.astype(F32)).astype(BF16)


def _attn_kernel(q_ref, k_ref, v_ref, gate_ref, km_ref, vm_ref, lq1_ref, lk1_ref, lq2_ref,
                 lk2_ref, subg_ref, w0_ref, w1_ref, w2_ref, w3_ref, o_ref, wb0_ref, wb1_ref,
                 wb2_ref, wb3_ref, m_ref, l_ref, acc_ref, *, tq, tk, dh, n_sub, unroll,
                 conv_steps):
    i = pl.program_id(2)
    ts = tq // n_sub
    dv = acc_ref.shape[2]

    step = (pl.program_id(0) * pl.num_programs(1) + pl.program_id(1)) * pl.num_programs(2) + i
    for k, (w_ref, wb_ref) in enumerate(((w0_ref, wb0_ref), (w1_ref, wb1_ref),
                                         (w2_ref, wb2_ref), (w3_ref, wb3_ref))):
        @pl.when(jnp.logical_and(step >= k * conv_steps, step < (k + 1) * conv_steps))
        def _(w_ref=w_ref, wb_ref=wb_ref):
            wb_ref[...] = w_ref[...].astype(wb_ref.dtype)

    def scores(c, kb):
        q = q_ref[c * ts:(c + 1) * ts, :]
        return jnp.concatenate([_dot_nt(q[:, :dh], kb[:, :dh]), _dot_nt(q[:, dh:], kb[:, dh:])],
                               axis=0)

    def lane_sum(p):
        psum = p[:, 0:LANES]
        for blk in range(1, p.shape[1] // LANES):
            psum = psum + p[:, blk * LANES:(blk + 1) * LANES]
        return psum

    row = lax.broadcasted_iota(jnp.int32, (2 * ts, tk), 0) % ts
    col = lax.broadcasted_iota(jnp.int32, (2 * ts, tk), 1)
    visible = col // CHUNK <= row // CHUNK
    for c in range(n_sub):
        off = pl.multiple_of(i * tq + c * ts, ts)
        s = jnp.where(visible, scores(c, k_ref[pl.ds(off, tk), :]), NEG_INF)
        m0 = jnp.max(s, axis=-1, keepdims=True)
        p = jnp.exp2(s - m0)
        m_ref[c] = jnp.broadcast_to(m0, (2 * ts, LANES))
        l_ref[c] = lane_sum(p)
        acc_ref[c] = _dot(p.astype(BF16), v_ref[pl.ds(off, tk), :])

    def update(c, s, vb):
        m_prev = m_ref[c]
        m_new = jnp.maximum(m_prev, jnp.max(s, axis=-1, keepdims=True))
        alpha = jnp.exp2(m_prev - m_new)
        p = jnp.exp2(s - jnp.tile(m_new, (1, s.shape[1] // LANES)))
        l_ref[c] = alpha * l_ref[c] + lane_sum(p)
        m_ref[c] = m_new
        acc_ref[c] = (acc_ref[c] * jnp.tile(alpha, (1, dv // LANES))
                      + _dot(p.astype(BF16), vb))

    def full_blocks(jb, carry):
        for u in range(unroll):
            off = pl.multiple_of((jb * unroll + u) * tk, tk)
            kb = k_ref[pl.ds(off, tk), :]
            vb = v_ref[pl.ds(off, tk), :]
            for c in range(n_sub):
                update(c, scores(c, kb), vb)
        return carry

    lax.fori_loop(0, i * (tq // tk // unroll), full_blocks, 0)

    for d in range(n_sub - 1):
        off = pl.multiple_of(i * tq + d * tk, tk)
        kb = k_ref[pl.ds(off, tk), :]
        vb = v_ref[pl.ds(off, tk), :]
        for c in range(d + 1, n_sub):
            update(c, scores(c, kb), vb)

    lam = _lambda_full(lq1_ref, lk1_ref, lq2_ref, lk2_ref)
    lane0 = lax.broadcasted_iota(jnp.int32, (2 * ts, LANES), 1) == 0
    for c in range(n_sub):
        s = scores(c, km_ref[...])
        m_prev = m_ref[c]
        m_new = jnp.maximum(m_prev, jnp.max(s, axis=-1, keepdims=True))
        alpha = jnp.exp2(m_prev - m_new)
        p = jnp.exp2(s - m_new[:, :s.shape[1]])
        l = alpha * l_ref[c] + jnp.where(lane0, jnp.sum(p, axis=-1, keepdims=True), 0.0)
        acc = (acc_ref[c] * jnp.tile(alpha, (1, dv // LANES)) + _dot(p.astype(BF16), vm_ref[...]))
        inv_l = 1.0 / jnp.sum(l, axis=-1, keepdims=True)
        o_ref[c * ts:(c + 1) * ts, :] = _diff_finalize(
            acc[:ts], acc[ts:], inv_l[:ts], inv_l[ts:], lam, subg_ref[...],
            gate_ref[c * ts:(c + 1) * ts, :])


def _attention(qkvg, qkvg_meta, lq1, lk1, lq2, lk2, subg, side_weights, *, batch, seq,
               d_model):
    dv = d_model // DA_HEADS
    dh = dv // 2
    tq, tk, n_sub = ATTN_TQ, ATTN_TK, ATTN_SUBTILES
    ts = tq // n_sub
    unroll = ATTN_KEY_UNROLL
    assert ts % CHUNK == 0 and tk % CHUNK == 0 and tq % (tk * unroll) == 0
    nq = seq // tq
    h_ = DA_HEADS
    n_steps = batch * h_ * nq
    conv_steps = n_steps // len(side_weights)
    conv_rows = d_model // conv_steps
    assert len(side_weights) == 4 and conv_steps * 4 == n_steps and conv_rows * conv_steps == d_model
    kern = functools.partial(_attn_kernel, tq=tq, tk=tk, dh=dh, n_sub=n_sub, unroll=unroll,
                             conv_steps=conv_steps)
    vec = lambda n: pl.BlockSpec((1, n), lambda b, h, i: (0, 0))

    def conv_spec(k, col_block):
        def imap(b, h, i):
            step = (b * h_ + h) * nq + i
            return (jnp.clip(step - k * conv_steps, 0, conv_steps - 1), col_block)
        return pl.BlockSpec((conv_rows, d_model), imap)

    side_arrays = [w for w, _ in side_weights]
    out_specs = [pl.BlockSpec((tq, dv), lambda b, h, i: (b * nq + i, h))]
    out_specs += [conv_spec(k, 0) for k in range(len(side_weights))]
    out_shape = [jax.ShapeDtypeStruct((batch * seq, d_model), BF16)]
    out_shape += [jax.ShapeDtypeStruct((d_model, d_model), BF16) for _ in side_weights]
    return pl.pallas_call(
        kern,
        grid=(batch, h_, nq),
        in_specs=[
            pl.BlockSpec((tq, dv), lambda b, h, i: (b * nq + i, h)),
            pl.BlockSpec((seq, dv), lambda b, h, i: (b, h_ + h)),
            pl.BlockSpec((seq, dv), lambda b, h, i: (b, 2 * h_ + h)),
            pl.BlockSpec((tq, dv), lambda b, h, i: (b * nq + i, 3 * h_ + h)),
            pl.BlockSpec((N_META, dv), lambda b, h, i: (0, h_ + h)),
            pl.BlockSpec((N_META, dv), lambda b, h, i: (0, 2 * h_ + h)),
            vec(dh), vec(dh), vec(dh), vec(dh), vec(dv),
        ] + [conv_spec(k, cb) for k, (_, cb) in enumerate(side_weights)],
        out_specs=out_specs,
        out_shape=out_shape,
        scratch_shapes=[pltpu.VMEM((n_sub, 2 * ts, LANES), F32),
                        pltpu.VMEM((n_sub, 2 * ts, LANES), F32),
                        pltpu.VMEM((n_sub, 2 * ts, dv), F32)],
        compiler_params=pltpu.CompilerParams(
            dimension_semantics=("arbitrary", "arbitrary", "arbitrary"),
            vmem_limit_bytes=VMEM_LIMIT_BYTES),
        name="l0_diff_attn",
    )(qkvg, qkvg, qkvg, qkvg, qkvg_meta, qkvg_meta, lq1, lk1, lq2, lk2, subg, *side_arrays)


def _meta_attn_kernel(q_ref, k_ref, v_ref, gate_ref, lq1_ref, lk1_ref, lq2_ref, lk2_ref,
                      subg_ref, o_ref, *, dh):
    q = q_ref[...]
    k = k_ref[...]
    v = v_ref[...]

    def attend(qm, km):
        s = _dot_nt(qm, km)
        p = jnp.exp2(s - jnp.max(s, axis=-1, keepdims=True))
        return _dot(p.astype(BF16), v), 1.0 / jnp.sum(p, axis=-1, keepdims=True)

    lam = _lambda_full(lq1_ref, lk1_ref, lq2_ref, lk2_ref)
    acc1, inv_l1 = attend(q[:, :dh], k[:, :dh])
    acc2, inv_l2 = attend(q[:, dh:], k[:, dh:])
    o_ref[...] = _diff_finalize(acc1, acc2, inv_l1, inv_l2, lam, subg_ref[...], gate_ref[...])


def _meta_attention(qkvg_meta, lq1, lk1, lq2, lk2, subg, *, d_model):
    dv = d_model // DA_HEADS
    dh = dv // 2
    h_ = DA_HEADS
    vec = lambda n: pl.BlockSpec((1, n), lambda h: (0, 0))
    blk = lambda part: pl.BlockSpec((N_META, dv), lambda h: (0, part * h_ + h))
    return pl.pallas_call(
        functools.partial(_meta_attn_kernel, dh=dh),
        grid=(h_,),
        in_specs=[blk(0), blk(1), blk(2), blk(3), vec(dh), vec(dh), vec(dh), vec(dh), vec(dv)],
        out_specs=pl.BlockSpec((N_META, dv), lambda h: (0, h)),
        out_shape=jax.ShapeDtypeStruct((N_META, d_model), BF16),
        compiler_params=pltpu.CompilerParams(dimension_semantics=("parallel",)),
        name="l0_meta_attn",
    )(qkvg_meta, qkvg_meta, qkvg_meta, qkvg_meta, lq1, lk1, lq2, lk2, subg)


def _out_proj_kernel(a_ref, w_ref, x_ref, g_ref, o_ref):
    y = _dot(a_ref[...], w_ref[...])
    o_ref[...] = x_ref[...] + _rms(y, g_ref[...])


def _out_proj(a, w, x, g, *, tm):
    m, d = x.shape
    return pl.pallas_call(
        _out_proj_kernel,
        grid=(m // tm,),
        in_specs=[
            pl.BlockSpec((tm, d), lambda i: (i, 0)),
            pl.BlockSpec((d, d), lambda i: (0, 0), pipeline_mode=pl.Buffered(1)),
            pl.BlockSpec((tm, d), lambda i: (i, 0)),
            pl.BlockSpec((1, d), lambda i: (0, 0)),
        ],
        out_specs=pl.BlockSpec((tm, d), lambda i: (i, 0)),
        out_shape=jax.ShapeDtypeStruct((m, d), F32),
        compiler_params=pltpu.CompilerParams(
            dimension_semantics=("parallel",), vmem_limit_bytes=VMEM_LIMIT_BYTES),
        name="l0_out_proj",
    )(a, w, x, g)


def _tail_kernel(a_ref, x_ref, hm_ref, w_out0_ref, post_g0_ref, pre_g_ref, w_in_u_ref,
                 w_in_g_ref, w_grp_ref, scale_ref, w_out_ref, post_g_ref, o_ref, u_ext_ref,
                 z_ref, *, tm, d, halo):
    i = pl.program_id(1)
    pre_g = pre_g_ref[...]

    @pl.when(i == 0)
    def _():
        hmn = _rms(hm_ref[...], pre_g).astype(BF16)
        u_ext_ref[0:halo, :] = _dot(hmn, w_in_u_ref[...])

    @pl.when(i > 0)
    def _():
        u_ext_ref[0:halo, :] = u_ext_ref[tm:tm + halo, :]

    h = x_ref[...] + _rms(_dot(a_ref[...], w_out0_ref[...]), post_g0_ref[...])
    o_ref[...] = h
    hn = _rms(h, pre_g).astype(BF16)
    u_ext_ref[halo:halo + tm, :] = _dot(hn, w_in_u_ref[...])

    gsz = d // len(POOL_WINDOWS)
    for gi, w in enumerate(POOL_WINDOWS):
        c0 = gi * gsz
        u = u_ext_ref[halo:halo + tm, c0:c0 + gsz]
        s = u
        for back in range(1, w):
            s = s + u_ext_ref[halo - back:halo - back + tm, c0:c0 + gsz]
        mixed = (s * (1.0 / w) - u).astype(BF16)
        mo = _dot(mixed, w_grp_ref[gi]) * scale_ref[:, c0:c0 + gsz]
        gate = _dot(hn, w_in_g_ref[:, c0:c0 + gsz])
        z_ref[:, c0:c0 + gsz] = (mo * _silu(gate)).astype(BF16)

    y = _dot(z_ref[...], w_out_ref[...])
    o_ref[...] = o_ref[...] + _rms(y, post_g_ref[...])


def _tail(a, x, h_meta, w_out0, post_g0, pre_g, w_in_u, w_in_g, w_grp, scale, w_out, post_g, *,
          batch, seq, tm):
    m, d = x.shape
    nt = seq // tm
    halo = N_META
    assert max(POOL_WINDOWS) <= halo and halo <= tm
    const2 = lambda b, i: (0, 0)
    row_tile = lambda b, i: (b * nt + i, 0)
    resident = lambda shape, imap: pl.BlockSpec(shape, imap, pipeline_mode=pl.Buffered(1))
    kern = functools.partial(_tail_kernel, tm=tm, d=d, halo=halo)
    return pl.pallas_call(
        kern,
        grid=(batch, nt),
        in_specs=[
            pl.BlockSpec((tm, d), row_tile),
            pl.BlockSpec((tm, d), row_tile),
            resident((N_META, d), const2),
            resident((d, d), const2),
            pl.BlockSpec((1, d), const2),
            pl.BlockSpec((1, d), const2),
            resident((d, d), const2),
            resident((d, d), const2),
            resident(w_grp.shape, lambda b, i: (0, 0, 0)),
            pl.BlockSpec((1, d), const2),
            resident((d, d), const2),
            pl.BlockSpec((1, d), const2),
        ],
        out_specs=pl.BlockSpec((tm, d), row_tile),
        out_shape=jax.ShapeDtypeStruct((m, d), F32),
        scratch_shapes=[pltpu.VMEM((halo + tm, d), F32), pltpu.VMEM((tm, d), BF16)],
        compiler_params=pltpu.CompilerParams(
            dimension_semantics=("arbitrary", "arbitrary"),
            vmem_limit_bytes=VMEM_LIMIT_BYTES),
        name="l0_out_l1_pool",
    )(a, x, h_meta, w_out0, post_g0, pre_g, w_in_u, w_in_g, w_grp, scale, w_out, post_g)


def _rope_tables(length, rope_dim):
    half = rope_dim // 2
    pos = np.arange(length, dtype=np.float32)
    inv = np.float32(ROPE_THETA) ** (-np.arange(0, rope_dim, 2, dtype=np.float32)
                                     / np.float32(rope_dim))
    ang = pos[:, None] * inv[None, :]
    cos, sin = np.cos(ang), np.sin(ang)
    zeros = lambda n: np.zeros((length, n), np.float32)
    cos_t = np.concatenate([cos, cos, np.ones((length, LANES - rope_dim), np.float32)], axis=1)
    sa_t = np.concatenate([zeros(half), sin, zeros(LANES - rope_dim)], axis=1)
    sb_t = np.concatenate([-sin, zeros(LANES - half)], axis=1)
    return cos_t, sa_t, sb_t


def kernel(x, meta_tokens, pre_norm_g, post_norm_g, attn_w_in, attn_w_out, attn_lambda_q1,
           attn_lambda_k1, attn_lambda_q2, attn_lambda_k2, attn_subln_g, pool_w_in,
           pool_w_group, pool_scale, pool_w_out):
    batch, seq, d = x.shape
    dv = d // DA_HEADS
    dh = dv // 2
    assert dh == LANES and seq % CHUNK == 0
    xf = x.reshape(batch * seq, d)
    meta = meta_tokens.astype(x.dtype)

    cos_t, sa_t, sb_t = _rope_tables(N_META + seq, dh // 4)
    w_in0 = attn_w_in[0].astype(BF16)
    pre_g0, pre_g1 = pre_norm_g[0:1], pre_norm_g[1:2]
    post_g0, post_g1 = post_norm_g[0:1], post_norm_g[1:2]
    q_scale = float(dh) ** -0.5 * math.log2(math.e)
    lam = [p[0:1] for p in (attn_lambda_q1, attn_lambda_k1, attn_lambda_q2, attn_lambda_k2)]
    subg = attn_subln_g[0:1]

    proj = functools.partial(_norm_proj, tn=PROJ_TN, q_scale=q_scale)
    qkvg_meta = proj(meta, pre_g0, w_in0, cos_t[:N_META], sa_t[:N_META], sb_t[:N_META],
                     tm=N_META, rows_per_seq=N_META)
    qkvg = proj(xf, pre_g0, w_in0, cos_t[N_META:], sa_t[N_META:], sb_t[N_META:],
                tm=PROJ_TM, rows_per_seq=seq)
    a_meta = _meta_attention(qkvg_meta, *lam, subg, d_model=d)
    side_weights = [(attn_w_out[0], 0), (pool_w_in[0], 0), (pool_w_in[0], 1), (pool_w_out[0], 0)]
    a, w_out0, w_in1_u, w_in1_g, w_out1 = _attention(
        qkvg, qkvg_meta, *lam, subg, side_weights, batch=batch, seq=seq, d_model=d)
    h1_meta = _out_proj(a_meta, w_out0, meta, post_g0, tm=N_META)

    out = _tail(a, xf, h1_meta, w_out0, post_g0, pre_g1, w_in1_u, w_in1_g,
                pool_w_group[0].astype(BF16), pool_scale[0:1], w_out1, post_g1,
                batch=batch, seq=seq, tm=POOL_TM)
    return out.reshape(batch, seq, d)
```

```python
import functools
import math

import jax
import jax.numpy as jnp
import numpy as np
from jax import lax
from jax.experimental import pallas as pl
from jax.experimental.pallas import tpu as pltpu

F32 = jnp.float32
BF16 = jnp.bfloat16

N_META = 16
CHUNK = 64
RMS_EPS = 1e-6
DA_HEADS = 8
ROPE_THETA = 500000.0
POOL_WINDOWS = (2, 4, 8, 16)
LAMBDA_INIT_L0 = 0.8 - 0.6 * math.exp(-0.3 * 0)

LANES = 128
VMEM_LIMIT_BYTES = 56 * 1024 * 1024

PROJ_TM = 1024
PROJ_TN = 2048
PROJ_META_TN = 512
PROJ_SUB_ROWS = 256
PROJ_SUB_COLS = 256
ATTN_TQ = 2048
ATTN_TK = 256
ATTN_SUBTILES = 8
ATTN_KEY_UNROLL = 4
POOL_TM = 256

NEG_INF = float(jnp.finfo(jnp.float32).min)


def _rms(x, g):
    ms = jnp.mean(x * x, axis=-1, keepdims=True)
    return x * lax.rsqrt(ms + RMS_EPS) * g


def _silu(g):
    return g * (1.0 / (1.0 + jnp.exp(-g)))


def _dot(a, b):
    return jnp.dot(a, b, preferred_element_type=F32)


def _dot_nt(a, b):
    return lax.dot_general(a, b, (((1,), (1,)), ((), ())), preferred_element_type=F32)


def _norm_proj_kernel(x_ref, g_ref, w_ref, cos_ref, sa_ref, sb_ref, o_ref, *rest,
                      n_col_tiles_per_part, q_scale, sub_rows, sub_cols, emit_weight):
    wb_ref, hn_ref = rest if emit_weight else (None,) + rest
    j = pl.program_id(1)
    npt = n_col_tiles_per_part

    @pl.when(j == 0)
    def _():
        hn_ref[...] = _rms(x_ref[...], g_ref[...]).astype(BF16)

    tm, tn = o_ref.shape
    mc, nc = min(tm, sub_rows), min(tn, sub_cols)
    is_rope = j < 2 * npt
    is_gate = j >= 3 * npt
    scale = jnp.where(j < npt, q_scale, 1.0).astype(F32)
    half = 16
    for mi in range(tm // mc):
        rows = slice(mi * mc, (mi + 1) * mc)
        c = jnp.where(is_rope, cos_ref[rows, :] * scale, 1.0)
        sa = jnp.where(is_rope, sa_ref[rows, :] * scale, 0.0)
        sb = jnp.where(is_rope, sb_ref[rows, :] * scale, 0.0)
        for ni in range(tn // nc):
            w_sub = w_ref[:, ni * nc:(ni + 1) * nc]
            if emit_weight:
                w_sub = w_sub.astype(BF16)
                if mi == 0:
                    wb_ref[:, ni * nc:(ni + 1) * nc] = w_sub
            acc = _dot(hn_ref[rows, :], w_sub)
            for blk in range(nc // LANES):
                t = acc[:, blk * LANES:(blk + 1) * LANES]
                r = t * c + pltpu.roll(t, half, 1) * sa + pltpu.roll(t, LANES - half, 1) * sb
                f = jnp.where(is_gate, 1.0 / (1.0 + jnp.exp(-t)), 1.0)
                col = ni * nc + blk * LANES
                o_ref[rows, col:col + LANES] = (r * f).astype(o_ref.dtype)


def _norm_proj(x, g, w, cos_t, sa_t, sb_t, *, tm, tn, rows_per_seq, q_scale,
               emit_weight=False):
    m, d = x.shape
    n = w.shape[1]
    tiles_per_seq = rows_per_seq // tm
    assert not emit_weight or m == tm
    kern = functools.partial(_norm_proj_kernel, n_col_tiles_per_part=(n // 4) // tn,
                             q_scale=q_scale, sub_rows=PROJ_SUB_ROWS, sub_cols=PROJ_SUB_COLS,
                             emit_weight=emit_weight)
    tab_spec = pl.BlockSpec((tm, LANES), lambda i, j: (i % tiles_per_seq, 0))
    out_specs = pl.BlockSpec((tm, tn), lambda i, j: (i, j))
    out_shape = jax.ShapeDtypeStruct((m, n), BF16)
    if emit_weight:
        out_specs = [out_specs, pl.BlockSpec((d, tn), lambda i, j: (0, j))]
        out_shape = [out_shape, jax.ShapeDtypeStruct((d, n), BF16)]
    return pl.pallas_call(
        kern,
        grid=(m // tm, n // tn),
        in_specs=[
            pl.BlockSpec((tm, d), lambda i, j: (i, 0)),
            pl.BlockSpec((1, d), lambda i, j: (0, 0)),
            pl.BlockSpec((d, tn), lambda i, j: (0, j)),
            tab_spec, tab_spec, tab_spec,
        ],
        out_specs=out_specs,
        out_shape=out_shape,
        scratch_shapes=[pltpu.VMEM((tm, d), BF16)],
        compiler_params=pltpu.CompilerParams(
            dimension_semantics=("parallel", "arbitrary"),
            vmem_limit_bytes=VMEM_LIMIT_BYTES),
        name="l0_norm_proj",
    )(x, g, w, cos_t, sa_t, sb_t)


def _lambda_full(lq1_ref, lk1_ref, lq2_ref, lk2_ref):
    a = jnp.sum(lq1_ref[...] * lk1_ref[...], axis=-1, keepdims=True)
    b = jnp.sum(lq2_ref[...] * lk2_ref[...], axis=-1, keepdims=True)
    return jnp.exp(a) - jnp.exp(b) + LAMBDA_INIT_L0


def _diff_finalize(acc1, acc2, inv_l1, inv_l2, lam, subg, gate):
    o = acc1 * inv_l1 - acc2 * (lam * inv_l2)
    o = _rms(o, subg * (1.0 - LAMBDA_INIT_L0))
    return (o * gate.astype(F32)).astype(BF16)


def _attn_kernel(q_ref, k_ref, v_ref, gate_ref, km_ref, vm_ref, lq1_ref, lk1_ref, lq2_ref,
                 lk2_ref, subg_ref, w0_ref, w1_ref, w2_ref, w3_ref, o_ref, wb0_ref, wb1_ref,
                 wb2_ref, wb3_ref, m_ref, l_ref, acc_ref, *, tq, tk, dh, n_sub, unroll,
                 conv_steps):
    i = pl.program_id(2)
    ts = tq // n_sub
    dv = acc_ref.shape[2]

    step = (pl.program_id(0) * pl.num_programs(1) + pl.program_id(1)) * pl.num_programs(2) + i
    for k, (w_ref, wb_ref) in enumerate(((w0_ref, wb0_ref), (w1_ref, wb1_ref),
                                         (w2_ref, wb2_ref), (w3_ref, wb3_ref))):
        @pl.when(jnp.logical_and(step >= k * conv_steps, step < (k + 1) * conv_steps))
        def _(w_ref=w_ref, wb_ref=wb_ref):
            wb_ref[...] = w_ref[...].astype(wb_ref.dtype)

    def scores(c, kb):
        q = q_ref[c * ts:(c + 1) * ts, :]
        return jnp.concatenate([_dot_nt(q[:, :dh], kb[:, :dh]), _dot_nt(q[:, dh:], kb[:, dh:])],
                               axis=0)

    def lane_sum(p):
        psum = p[:, 0:LANES]
        for blk in range(1, p.shape[1] // LANES):
            psum = psum + p[:, blk * LANES:(blk + 1) * LANES]
        return psum

    row = lax.broadcasted_iota(jnp.int32, (2 * ts, tk), 0) % ts
    col = lax.broadcasted_iota(jnp.int32, (2 * ts, tk), 1)
    visible = col // CHUNK <= row // CHUNK
    for c in range(n_sub):
        off = pl.multiple_of(i * tq + c * ts, ts)
        s = jnp.where(visible, scores(c, k_ref[pl.ds(off, tk), :]), NEG_INF)
        m0 = jnp.max(s, axis=-1, keepdims=True)
        p = jnp.exp2(s - m0)
        m_ref[c] = jnp.broadcast_to(m0, (2 * ts, LANES))
        l_ref[c] = lane_sum(p)
        acc_ref[c] = _dot(p.astype(BF16), v_ref[pl.ds(off, tk), :])

    def update(c, s, vb):
        m_prev = m_ref[c]
        m_new = jnp.maximum(m_prev, jnp.max(s, axis=-1, keepdims=True))
        alpha = jnp.exp2(m_prev - m_new)
        p = jnp.exp2(s - jnp.tile(m_new, (1, s.shape[1] // LANES)))
        l_ref[c] = alpha * l_ref[c] + lane_sum(p)
        m_ref[c] = m_new
        acc_ref[c] = (acc_ref[c] * jnp.tile(alpha, (1, dv // LANES))
                      + _dot(p.astype(BF16), vb))

    def full_blocks(jb, carry):
        for u in range(unroll):
            off = pl.multiple_of((jb * unroll + u) * tk, tk)
            kb = k_ref[pl.ds(off, tk), :]
            vb = v_ref[pl.ds(off, tk), :]
            for c in range(n_sub):
                update(c, scores(c, kb), vb)
        return carry

    lax.fori_loop(0, i * (tq // tk // unroll), full_blocks, 0)

    for d in range(n_sub - 1):
        off = pl.multiple_of(i * tq + d * tk, tk)
        kb = k_ref[pl.ds(off, tk), :]
        vb = v_ref[pl.ds(off, tk), :]
        for c in range(d + 1, n_sub):
            update(c, scores(c, kb), vb)

    lam = _lambda_full(lq1_ref, lk1_ref, lq2_ref, lk2_ref)
    lane0 = lax.broadcasted_iota(jnp.int32, (2 * ts, LANES), 1) == 0
    for c in range(n_sub):
        s = scores(c, km_ref[...])
        m_prev = m_ref[c]
        m_new = jnp.maximum(m_prev, jnp.max(s, axis=-1, keepdims=True))
        alpha = jnp.exp2(m_prev - m_new)
        p = jnp.exp2(s - m_new[:, :s.shape[1]])
        l = alpha * l_ref[c] + jnp.where(lane0, jnp.sum(p, axis=-1, keepdims=True), 0.0)
        acc = (acc_ref[c] * jnp.tile(alpha, (1, dv // LANES)) + _dot(p.astype(BF16), vm_ref[...]))
        inv_l = 1.0 / jnp.sum(l, axis=-1, keepdims=True)
        o_ref[c * ts:(c + 1) * ts, :] = _diff_finalize(
            acc[:ts], acc[ts:], inv_l[:ts], inv_l[ts:], lam, subg_ref[...],
            gate_ref[c * ts:(c + 1) * ts, :])


def _attention(qkvg, qkvg_meta, lq1, lk1, lq2, lk2, subg, side_weights, *, batch, seq,
               d_model):
    dv = d_model // DA_HEADS
    dh = dv // 2
    tq, tk, n_sub = ATTN_TQ, ATTN_TK, ATTN_SUBTILES
    ts = tq // n_sub
    unroll = ATTN_KEY_UNROLL
    assert ts % CHUNK == 0 and tk % CHUNK == 0 and tq % (tk * unroll) == 0
    nq = seq // tq
    h_ = DA_HEADS
    n_steps = batch * h_ * nq
    conv_steps = n_steps // len(side_weights)
    conv_rows = d_model // conv_steps
    assert len(side_weights) == 4 and conv_steps * 4 == n_steps and conv_rows * conv_steps == d_model
    kern = functools.partial(_attn_kernel, tq=tq, tk=tk, dh=dh, n_sub=n_sub, unroll=unroll,
                             conv_steps=conv_steps)
    vec = lambda n: pl.BlockSpec((1, n), lambda b, h, i: (0, 0))

    def conv_spec(k, col_block):
        def imap(b, h, i):
            step = (b * h_ + h) * nq + i
            return (jnp.clip(step - k * conv_steps, 0, conv_steps - 1), col_block)
        return pl.BlockSpec((conv_rows, d_model), imap)

    side_arrays = [w for w, _ in side_weights]
    out_specs = [pl.BlockSpec((tq, dv), lambda b, h, i: (b * nq + i, h))]
    out_specs += [conv_spec(k, 0) for k in range(len(side_weights))]
    out_shape = [jax.ShapeDtypeStruct((batch * seq, d_model), BF16)]
    out_shape += [jax.ShapeDtypeStruct((d_model, d_model), BF16) for _ in side_weights]
    return pl.pallas_call(
        kern,
        grid=(batch, h_, nq),
        in_specs=[
            pl.BlockSpec((tq, dv), lambda b, h, i: (b * nq + i, h)),
            pl.BlockSpec((seq, dv), lambda b, h, i: (b, h_ + h)),
            pl.BlockSpec((seq, dv), lambda b, h, i: (b, 2 * h_ + h)),
            pl.BlockSpec((tq, dv), lambda b, h, i: (b * nq + i, 3 * h_ + h)),
            pl.BlockSpec((N_META, dv), lambda b, h, i: (0, h_ + h)),
            pl.BlockSpec((N_META, dv), lambda b, h, i: (0, 2 * h_ + h)),
            vec(dh), vec(dh), vec(dh), vec(dh), vec(dv),
        ] + [conv_spec(k, cb) for k, (_, cb) in enumerate(side_weights)],
        out_specs=out_specs,
        out_shape=out_shape,
        scratch_shapes=[pltpu.VMEM((n_sub, 2 * ts, LANES), F32),
                        pltpu.VMEM((n_sub, 2 * ts, LANES), F32),
                        pltpu.VMEM((n_sub, 2 * ts, dv), F32)],
        compiler_params=pltpu.CompilerParams(
            dimension_semantics=("arbitrary", "arbitrary", "arbitrary"),
            vmem_limit_bytes=VMEM_LIMIT_BYTES),
        name="l0_diff_attn",
    )(qkvg, qkvg, qkvg, qkvg, qkvg_meta, qkvg_meta, lq1, lk1, lq2, lk2, subg, *side_arrays)


def _meta_attn_kernel(q_ref, k_ref, v_ref, gate_ref, lq1_ref, lk1_ref, lq2_ref, lk2_ref,
                      subg_ref, o_ref, *, dh):
    q = q_ref[...]
    k = k_ref[...]
    v = v_ref[...]

    def attend(qm, km):
        s = _dot_nt(qm, km)
        p = jnp.exp2(s - jnp.max(s, axis=-1, keepdims=True))
        return _dot(p.astype(BF16), v), 1.0 / jnp.sum(p, axis=-1, keepdims=True)

    lam = _lambda_full(lq1_ref, lk1_ref, lq2_ref, lk2_ref)
    acc1, inv_l1 = attend(q[:, :dh], k[:, :dh])
    acc2, inv_l2 = attend(q[:, dh:], k[:, dh:])
    o_ref[...] = _diff_finalize(acc1, acc2, inv_l1, inv_l2, lam, subg_ref[...], gate_ref[...])


def _meta_attention(qkvg_meta, lq1, lk1, lq2, lk2, subg, *, d_model):
    dv = d_model // DA_HEADS
    dh = dv // 2
    h_ = DA_HEADS
    vec = lambda n: pl.BlockSpec((1, n), lambda h: (0, 0))
    blk = lambda part: pl.BlockSpec((N_META, dv), lambda h: (0, part * h_ + h))
    return pl.pallas_call(
        functools.partial(_meta_attn_kernel, dh=dh),
        grid=(h_,),
        in_specs=[blk(0), blk(1), blk(2), blk(3), vec(dh), vec(dh), vec(dh), vec(dh), vec(dv)],
        out_specs=pl.BlockSpec((N_META, dv), lambda h: (0, h)),
        out_shape=jax.ShapeDtypeStruct((N_META, d_model), BF16),
        compiler_params=pltpu.CompilerParams(dimension_semantics=("parallel",)),
        name="l0_meta_attn",
    )(qkvg_meta, qkvg_meta, qkvg_meta, qkvg_meta, lq1, lk1, lq2, lk2, subg)


def _out_proj_kernel(a_ref, w_ref, x_ref, g_ref, o_ref):
    y = _dot(a_ref[...], w_ref[...])
    o_ref[...] = x_ref[...] + _rms(y, g_ref[...])


def _out_proj(a, w, x, g, *, tm):
    m, d = x.shape
    return pl.pallas_call(
        _out_proj_kernel,
        grid=(m // tm,),
        in_specs=[
            pl.BlockSpec((tm, d), lambda i: (i, 0)),
            pl.BlockSpec((d, d), lambda i: (0, 0), pipeline_mode=pl.Buffered(1)),
            pl.BlockSpec((tm, d), lambda i: (i, 0)),
            pl.BlockSpec((1, d), lambda i: (0, 0)),
        ],
        out_specs=pl.BlockSpec((tm, d), lambda i: (i, 0)),
        out_shape=jax.ShapeDtypeStruct((m, d), F32),
        compiler_params=pltpu.CompilerParams(
            dimension_semantics=("parallel",), vmem_limit_bytes=VMEM_LIMIT_BYTES),
        name="l0_out_proj",
    )(a, w, x, g)


def _tail_kernel(a_ref, x_ref, hm_ref, w_out0_ref, post_g0_ref, pre_g_ref, w_in_u_ref,
                 w_in_g_ref, w_grp_ref, scale_ref, w_out_ref, post_g_ref, o_ref, u_ext_ref,
                 z_ref, *, tm, d, halo):
    i = pl.program_id(1)
    pre_g = pre_g_ref[...]

    @pl.when(i == 0)
    def _():
        hmn = _rms(hm_ref[...], pre_g).astype(BF16)
        u_ext_ref[0:halo, :] = _dot(hmn, w_in_u_ref[...])

    @pl.when(i > 0)
    def _():
        u_ext_ref[0:halo, :] = u_ext_ref[tm:tm + halo, :]

    h = x_ref[...] + _rms(_dot(a_ref[...], w_out0_ref[...]), post_g0_ref[...])
    o_ref[...] = h
    hn = _rms(h, pre_g).astype(BF16)
    u_ext_ref[halo:halo + tm, :] = _dot(hn, w_in_u_ref[...])

    gsz = d // len(POOL_WINDOWS)
    for gi, w in enumerate(POOL_WINDOWS):
        c0 = gi * gsz
        u = u_ext_ref[halo:halo + tm, c0:c0 + gsz]
        s = u
        for back in range(1, w):
            s = s + u_ext_ref[halo - back:halo - back + tm, c0:c0 + gsz]
        mixed = (s * (1.0 / w) - u).astype(BF16)
        mo = _dot(mixed, w_grp_ref[gi]) * scale_ref[:, c0:c0 + gsz]
        gate = _dot(hn, w_in_g_ref[:, c0:c0 + gsz])
        z_ref[:, c0:c0 + gsz] = (mo * _silu(gate)).astype(BF16)

    y = _dot(z_ref[...], w_out_ref[...])
    o_ref[...] = o_ref[...] + _rms(y, post_g_ref[...])


def _tail(a, x, h_meta, w_out0, post_g0, pre_g, w_in_u, w_in_g, w_grp, scale, w_out, post_g, *,
          batch, seq, tm):
    m, d = x.shape
    nt = seq // tm
    halo = N_META
    assert max(POOL_WINDOWS) <= halo and halo <= tm
    const2 = lambda b, i: (0, 0)
    row_tile = lambda b, i: (b * nt + i, 0)
    resident = lambda shape, imap: pl.BlockSpec(shape, imap, pipeline_mode=pl.Buffered(1))
    kern = functools.partial(_tail_kernel, tm=tm, d=d, halo=halo)
    return pl.pallas_call(
        kern,
        grid=(batch, nt),
        in_specs=[
            pl.BlockSpec((tm, d), row_tile),
            pl.BlockSpec((tm, d), row_tile),
            resident((N_META, d), const2),
            resident((d, d), const2),
            pl.BlockSpec((1, d), const2),
            pl.BlockSpec((1, d), const2),
            resident((d, d), const2),
            resident((d, d), const2),
            resident(w_grp.shape, lambda b, i: (0, 0, 0)),
            pl.BlockSpec((1, d), const2),
            resident((d, d), const2),
            pl.BlockSpec((1, d), const2),
        ],
        out_specs=pl.BlockSpec((tm, d), row_tile),
        out_shape=jax.ShapeDtypeStruct((m, d), F32),
        scratch_shapes=[pltpu.VMEM((halo + tm, d), F32), pltpu.VMEM((tm, d), BF16)],
        compiler_params=pltpu.CompilerParams(
            dimension_semantics=("arbitrary", "arbitrary"),
            vmem_limit_bytes=VMEM_LIMIT_BYTES),
        name="l0_out_l1_pool",
    )(a, x, h_meta, w_out0, post_g0, pre_g, w_in_u, w_in_g, w_grp, scale, w_out, post_g)


def _rope_tables(length, rope_dim):
    half = rope_dim // 2
    pos = np.arange(length, dtype=np.float32)
    inv = np.float32(ROPE_THETA) ** (-np.arange(0, rope_dim, 2, dtype=np.float32)
                                     / np.float32(rope_dim))
    ang = pos[:, None] * inv[None, :]
    cos, sin = np.cos(ang), np.sin(ang)
    zeros = lambda n: np.zeros((length, n), np.float32)
    cos_t = np.concatenate([cos, cos, np.ones((length, LANES - rope_dim), np.float32)], axis=1)
    sa_t = np.concatenate([zeros(half), sin, zeros(LANES - rope_dim)], axis=1)
    sb_t = np.concatenate([-sin, zeros(LANES - half)], axis=1)
    return cos_t, sa_t, sb_t


def kernel(x, meta_tokens, pre_norm_g, post_norm_g, attn_w_in, attn_w_out, attn_lambda_q1,
           attn_lambda_k1, attn_lambda_q2, attn_lambda_k2, attn_subln_g, pool_w_in,
           pool_w_group, pool_scale, pool_w_out):
    batch, seq, d = x.shape
    dv = d // DA_HEADS
    dh = dv // 2
    assert dh == LANES and seq % CHUNK == 0
    xf = x.reshape(batch * seq, d)
    meta = meta_tokens.astype(x.dtype)

    cos_t, sa_t, sb_t = _rope_tables(N_META + seq, dh // 4)
    pre_g0, pre_g1 = pre_norm_g[0:1], pre_norm_g[1:2]
    post_g0, post_g1 = post_norm_g[0:1], post_norm_g[1:2]
    q_scale = float(dh) ** -0.5 * math.log2(math.e)
    lam = [p[0:1] for p in (attn_lambda_q1, attn_lambda_k1, attn_lambda_q2, attn_lambda_k2)]
    subg = attn_subln_g[0:1]

    qkvg_meta, w_in0 = _norm_proj(
        meta, pre_g0, attn_w_in[0], cos_t[:N_META], sa_t[:N_META], sb_t[:N_META], tm=N_META,
        tn=PROJ_META_TN, rows_per_seq=N_META, q_scale=q_scale, emit_weight=True)
    qkvg = _norm_proj(xf, pre_g0, w_in0, cos_t[N_META:], sa_t[N_META:], sb_t[N_META:],
                      tm=PROJ_TM, tn=PROJ_TN, rows_per_seq=seq, q_scale=q_scale)
    a_meta = _meta_attention(qkvg_meta, *lam, subg, d_model=d)
    side_weights = [(attn_w_out[0], 0), (pool_w_in[0], 0), (pool_w_in[0], 1), (pool_w_out[0], 0)]
    a, w_out0, w_in1_u, w_in1_g, w_out1 = _attention(
        qkvg, qkvg_meta, *lam, subg, side_weights, batch=batch, seq=seq, d_model=d)
    h1_meta = _out_proj(a_meta, w_out0, meta, post_g0, tm=N_META)

    out = _tail(a, xf, h1_meta, w_out0, post_g0, pre_g1, w_in1_u, w_in1_g,
                pool_w_group[0].astype(BF16), pool_scale[0:1], w_out1, post_g1,
                batch=batch, seq=seq, tm=POOL_TM)
    return out.reshape(batch, seq, d)
```

```python
import functools
import math

import jax
import jax.numpy as jnp
import numpy as np
from jax import lax
from jax.experimental import pallas as pl
from jax.experimental.pallas import tpu as pltpu

F32 = jnp.float32
BF16 = jnp.bfloat16

N_META = 16
CHUNK = 64
RMS_EPS = 1e-6
DA_HEADS = 8
ROPE_THETA = 500000.0
POOL_WINDOWS = (2, 4, 8, 16)
LAMBDA_INIT_L0 = 0.8 - 0.6 * math.exp(-0.3 * 0)

LANES = 128
VMEM_LIMIT_BYTES = 56 * 1024 * 1024

PROJ_TM = 1024
PROJ_TN = 2048
PROJ_META_TN = 512
PROJ_SUB_ROWS = 256
PROJ_SUB_COLS = 256
ATTN_TQ = 2048
ATTN_TK = 256
ATTN_SUBTILES = 8
ATTN_KEY_UNROLL = 4
POOL_TM = 256

NEG_INF = float(jnp.finfo(jnp.float32).min)


def _rms(x, g):
    ms = jnp.mean(x * x, axis=-1, keepdims=True)
    return x * lax.rsqrt(ms + RMS_EPS) * g


def _silu(g):
    return g * (1.0 / (1.0 + jnp.exp(-g)))


def _dot(a, b):
    return jnp.dot(a, b, preferred_element_type=F32)


def _dot_nt(a, b):
    return lax.dot_general(a, b, (((1,), (1,)), ((), ())), preferred_element_type=F32)


def _norm_proj_kernel(x_ref, g_ref, w_ref, cos_ref, sa_ref, sb_ref, o_ref, *rest,
                      n_col_tiles_per_part, q_scale, sub_rows, sub_cols, emit_weight):
    wb_ref, hn_ref = rest if emit_weight else (None,) + rest
    j = pl.program_id(1)
    npt = n_col_tiles_per_part

    @pl.when(j == 0)
    def _():
        hn_ref[...] = _rms(x_ref[...], g_ref[...]).astype(BF16)

    tm, tn = o_ref.shape
    mc, nc = min(tm, sub_rows), min(tn, sub_cols)
    is_rope = j < 2 * npt
    is_gate = j >= 3 * npt
    scale = jnp.where(j < npt, q_scale, 1.0).astype(F32)
    half = 16
    for mi in range(tm // mc):
        rows = slice(mi * mc, (mi + 1) * mc)
        c = jnp.where(is_rope, cos_ref[rows, :] * scale, 1.0)
        sa = jnp.where(is_rope, sa_ref[rows, :] * scale, 0.0)
        sb = jnp.where(is_rope, sb_ref[rows, :] * scale, 0.0)
        for ni in range(tn // nc):
            w_sub = w_ref[:, ni * nc:(ni + 1) * nc]
            if emit_weight:
                w_sub = w_sub.astype(BF16)
                if mi == 0:
                    wb_ref[:, ni * nc:(ni + 1) * nc] = w_sub
            acc = _dot(hn_ref[rows, :], w_sub)
            for blk in range(nc // LANES):
                t = acc[:, blk * LANES:(blk + 1) * LANES]
                r = t * c + pltpu.roll(t, half, 1) * sa + pltpu.roll(t, LANES - half, 1) * sb
                f = jnp.where(is_gate, 1.0 / (1.0 + jnp.exp(-t)), 1.0)
                col = ni * nc + blk * LANES
                o_ref[rows, col:col + LANES] = (r * f).astype(o_ref.dtype)


def _norm_proj(x, g, w, cos_t, sa_t, sb_t, *, tm, tn, rows_per_seq, q_scale,
               emit_weight=False):
    m, d = x.shape
    n = w.shape[1]
    tiles_per_seq = rows_per_seq // tm
    assert not emit_weight or m == tm
    kern = functools.partial(_norm_proj_kernel, n_col_tiles_per_part=(n // 4) // tn,
                             q_scale=q_scale, sub_rows=PROJ_SUB_ROWS, sub_cols=PROJ_SUB_COLS,
                             emit_weight=emit_weight)
    tab_spec = pl.BlockSpec((tm, LANES), lambda i, j: (i % tiles_per_seq, 0))
    out_specs = pl.BlockSpec((tm, tn), lambda i, j: (i, j))
    out_shape = jax.ShapeDtypeStruct((m, n), BF16)
    if emit_weight:
        out_specs = [out_specs, pl.BlockSpec((d, tn), lambda i, j: (0, j))]
        out_shape = [out_shape, jax.ShapeDtypeStruct((d, n), BF16)]
    return pl.pallas_call(
        kern,
        grid=(m // tm, n // tn),
        in_specs=[
            pl.BlockSpec((tm, d), lambda i, j: (i, 0)),
            pl.BlockSpec((1, d), lambda i, j: (0, 0)),
            pl.BlockSpec((d, tn), lambda i, j: (0, j)),
            tab_spec, tab_spec, tab_spec,
        ],
        out_specs=out_specs,
        out_shape=out_shape,
        scratch_shapes=[pltpu.VMEM((tm, d), BF16)],
        compiler_params=pltpu.CompilerParams(
            dimension_semantics=("parallel", "arbitrary"),
            vmem_limit_bytes=VMEM_LIMIT_BYTES),
        name="l0_norm_proj",
    )(x, g, w, cos_t, sa_t, sb_t)


def _lambda_full(lq1_ref, lk1_ref, lq2_ref, lk2_ref):
    a = jnp.sum(lq1_ref[...] * lk1_ref[...], axis=-1, keepdims=True)
    b = jnp.sum(lq2_ref[...] * lk2_ref[...], axis=-1, keepdims=True)
    return jnp.exp(a) - jnp.exp(b) + LAMBDA_INIT_L0


def _diff_finalize(acc1, acc2, inv_l1, inv_l2, lam, subg, gate):
    o = acc1 * inv_l1 - acc2 * (lam * inv_l2)
    o = _rms(o, subg * (1.0 - LAMBDA_INIT_L0))
    return (o * gate.astype(F32)).astype(BF16)


def _attn_kernel(q_ref, k_ref, v_ref, gate_ref, km_ref, vm_ref, lq1_ref, lk1_ref, lq2_ref,
                 lk2_ref, subg_ref, w0_ref, w1_ref, w2_ref, w3_ref, o_ref, wb0_ref, wb1_ref,
                 wb2_ref, wb3_ref, m_ref, l_ref, acc_ref, *, tq, tk, dh, n_sub, unroll,
                 conv_steps):
    i = pl.program_id(2)
    ts = tq // n_sub
    dv = acc_ref.shape[2]

    step = (pl.program_id(0) * pl.num_programs(1) + pl.program_id(1)) * pl.num_programs(2) + i
    for k, (w_ref, wb_ref) in enumerate(((w0_ref, wb0_ref), (w1_ref, wb1_ref),
                                         (w2_ref, wb2_ref), (w3_ref, wb3_ref))):
        @pl.when(jnp.logical_and(step >= k * conv_steps, step < (k + 1) * conv_steps))
        def _(w_ref=w_ref, wb_ref=wb_ref):
            wb_ref[...] = w_ref[...].astype(wb_ref.dtype)

    def scores(c, kb):
        q = q_ref[c * ts:(c + 1) * ts, :]
        return jnp.concatenate([_dot_nt(q[:, :dh], kb[:, :dh]), _dot_nt(q[:, dh:], kb[:, dh:])],
                               axis=0)

    def lane_sum(p):
        psum = p[:, 0:LANES]
        for blk in range(1, p.shape[1] // LANES):
            psum = psum + p[:, blk * LANES:(blk + 1) * LANES]
        return psum

    row = lax.broadcasted_iota(jnp.int32, (2 * ts, tk), 0) % ts
    col = lax.broadcasted_iota(jnp.int32, (2 * ts, tk), 1)
    visible = col // CHUNK <= row // CHUNK
    for c in range(n_sub):
        off = pl.multiple_of(i * tq + c * ts, ts)
        s = jnp.where(visible, scores(c, k_ref[pl.ds(off, tk), :]), NEG_INF)
        m0 = jnp.max(s, axis=-1, keepdims=True)
        p = jnp.exp2(s - m0)
        m_ref[c] = jnp.broadcast_to(m0, (2 * ts, LANES))
        l_ref[c] = lane_sum(p)
        acc_ref[c] = _dot(p.astype(BF16), v_ref[pl.ds(off, tk), :])

    def update(c, s, vb):
        m_prev = m_ref[c]
        m_new = jnp.maximum(m_prev, jnp.max(s, axis=-1, keepdims=True))
        alpha = jnp.exp2(m_prev - m_new)
        p = jnp.exp2(s - jnp.tile(m_new, (1, s.shape[1] // LANES)))
        l_ref[c] = alpha * l_ref[c] + lane_sum(p)
        m_ref[c] = m_new
        acc_ref[c] = (acc_ref[c] * jnp.tile(alpha, (1, dv // LANES))
                      + _dot(p.astype(BF16), vb))

    def full_blocks(jb, carry):
        for u in range(unroll):
            off = pl.multiple_of((jb * unroll + u) * tk, tk)
            kb = k_ref[pl.ds(off, tk), :]
            vb = v_ref[pl.ds(off, tk), :]
            for c in range(n_sub):
                update(c, scores(c, kb), vb)
        return carry

    lax.fori_loop(0, i * (tq // tk // unroll), full_blocks, 0)

    for d in range(n_sub - 1):
        off = pl.multiple_of(i * tq + d * tk, tk)
        kb = k_ref[pl.ds(off, tk), :]
        vb = v_ref[pl.ds(off, tk), :]
        for c in range(d + 1, n_sub):
            update(c, scores(c, kb), vb)

    lam = _lambda_full(lq1_ref, lk1_ref, lq2_ref, lk2_ref)
    km = km_ref[...]
    n_meta = km.shape[0]
    zfeat = jnp.zeros((n_meta, dh), km.dtype)
    zpad = jnp.zeros((LANES - n_meta, 2 * dh), km.dtype)
    km_bd = jnp.concatenate([jnp.concatenate([km[:, :dh], zfeat], axis=1), zpad,
                             jnp.concatenate([zfeat, km[:, dh:]], axis=1), zpad], axis=0)
    vm_pad = jnp.concatenate([vm_ref[...], jnp.zeros((LANES - n_meta, dv), km.dtype)], axis=0)
    real_key = lax.broadcasted_iota(jnp.int32, (2 * ts, LANES), 1) < n_meta
    for c in range(n_sub):
        r = _dot_nt(q_ref[c * ts:(c + 1) * ts, :], km_bd)
        s = jnp.where(real_key, jnp.concatenate([r[:, :LANES], r[:, LANES:]], axis=0), NEG_INF)
        m_prev = m_ref[c]
        m_new = jnp.maximum(m_prev, jnp.max(s, axis=-1, keepdims=True))
        alpha = jnp.exp2(m_prev - m_new)
        p = jnp.exp2(s - m_new)
        l = alpha * l_ref[c] + p
        acc = (acc_ref[c] * jnp.tile(alpha, (1, dv // LANES)) + _dot(p.astype(BF16), vm_pad))
        inv_l = 1.0 / jnp.sum(l, axis=-1, keepdims=True)
        o_ref[c * ts:(c + 1) * ts, :] = _diff_finalize(
            acc[:ts], acc[ts:], inv_l[:ts], inv_l[ts:], lam, subg_ref[...],
            gate_ref[c * ts:(c + 1) * ts, :])


def _attention(qkvg, qkvg_meta, lq1, lk1, lq2, lk2, subg, side_weights, *, batch, seq,
               d_model):
    dv = d_model // DA_HEADS
    dh = dv // 2
    tq, tk, n_sub = ATTN_TQ, ATTN_TK, ATTN_SUBTILES
    ts = tq // n_sub
    unroll = ATTN_KEY_UNROLL
    assert ts % CHUNK == 0 and tk % CHUNK == 0 and tq % (tk * unroll) == 0
    nq = seq // tq
    h_ = DA_HEADS
    n_steps = batch * h_ * nq
    conv_steps = n_steps // len(side_weights)
    conv_rows = d_model // conv_steps
    assert len(side_weights) == 4 and conv_steps * 4 == n_steps and conv_rows * conv_steps == d_model
    kern = functools.partial(_attn_kernel, tq=tq, tk=tk, dh=dh, n_sub=n_sub, unroll=unroll,
                             conv_steps=conv_steps)
    vec = lambda n: pl.BlockSpec((1, n), lambda b, h, i: (0, 0))

    def conv_spec(k, col_block):
        def imap(b, h, i):
            step = (b * h_ + h) * nq + i
            return (jnp.clip(step - k * conv_steps, 0, conv_steps - 1), col_block)
        return pl.BlockSpec((conv_rows, d_model), imap)

    side_arrays = [w for w, _ in side_weights]
    out_specs = [pl.BlockSpec((tq, dv), lambda b, h, i: (b * nq + i, h))]
    out_specs += [conv_spec(k, 0) for k in range(len(side_weights))]
    out_shape = [jax.ShapeDtypeStruct((batch * seq, d_model), BF16)]
    out_shape += [jax.ShapeDtypeStruct((d_model, d_model), BF16) for _ in side_weights]
    return pl.pallas_call(
        kern,
        grid=(batch, h_, nq),
        in_specs=[
            pl.BlockSpec((tq, dv), lambda b, h, i: (b * nq + i, h)),
            pl.BlockSpec((seq, dv), lambda b, h, i: (b, h_ + h)),
            pl.BlockSpec((seq, dv), lambda b, h, i: (b, 2 * h_ + h)),
            pl.BlockSpec((tq, dv), lambda b, h, i: (b * nq + i, 3 * h_ + h)),
            pl.BlockSpec((N_META, dv), lambda b, h, i: (0, h_ + h)),
            pl.BlockSpec((N_META, dv), lambda b, h, i: (0, 2 * h_ + h)),
            vec(dh), vec(dh), vec(dh), vec(dh), vec(dv),
        ] + [conv_spec(k, cb) for k, (_, cb) in enumerate(side_weights)],
        out_specs=out_specs,
        out_shape=out_shape,
        scratch_shapes=[pltpu.VMEM((n_sub, 2 * ts, LANES), F32),
                        pltpu.VMEM((n_sub, 2 * ts, LANES), F32),
                        pltpu.VMEM((n_sub, 2 * ts, dv), F32)],
        compiler_params=pltpu.CompilerParams(
            dimension_semantics=("arbitrary", "arbitrary", "arbitrary"),
            vmem_limit_bytes=VMEM_LIMIT_BYTES),
        name="l0_diff_attn",
    )(qkvg, qkvg, qkvg, qkvg, qkvg_meta, qkvg_meta, lq1, lk1, lq2, lk2, subg, *side_arrays)


def _meta_layer0_kernel(q_ref, k_ref, v_ref, gate_ref, lq1_ref, lk1_ref, lq2_ref, lk2_ref,
                        subg_ref, w_ref, x_ref, g_ref, o_ref, y_ref, *, dh):
    h = pl.program_id(0)
    q = q_ref[...]
    k = k_ref[...]
    v = v_ref[...]

    def attend(qm, km):
        s = _dot_nt(qm, km)
        p = jnp.exp2(s - jnp.max(s, axis=-1, keepdims=True))
        return _dot(p.astype(BF16), v), 1.0 / jnp.sum(p, axis=-1, keepdims=True)

    lam = _lambda_full(lq1_ref, lk1_ref, lq2_ref, lk2_ref)
    acc1, inv_l1 = attend(q[:, :dh], k[:, :dh])
    acc2, inv_l2 = attend(q[:, dh:], k[:, dh:])
    a = _diff_finalize(acc1, acc2, inv_l1, inv_l2, lam, subg_ref[...], gate_ref[...])
    part = _dot(a, w_ref[...])

    @pl.when(h == 0)
    def _():
        y_ref[...] = part

    @pl.when(h > 0)
    def _():
        y_ref[...] += part

    @pl.when(h == pl.num_programs(0) - 1)
    def _():
        o_ref[...] = x_ref[...] + _rms(y_ref[...], g_ref[...])


def _meta_layer0(qkvg_meta, lq1, lk1, lq2, lk2, subg, w_out, x, post_g, *, d_model):
    dv = d_model // DA_HEADS
    dh = dv // 2
    h_ = DA_HEADS
    vec = lambda n: pl.BlockSpec((1, n), lambda h: (0, 0))
    blk = lambda part: pl.BlockSpec((N_META, dv), lambda h: (0, part * h_ + h))
    rows = pl.BlockSpec((N_META, d_model), lambda h: (0, 0))
    return pl.pallas_call(
        functools.partial(_meta_layer0_kernel, dh=dh),
        grid=(h_,),
        in_specs=[blk(0), blk(1), blk(2), blk(3), vec(dh), vec(dh), vec(dh), vec(dh), vec(dv),
                  pl.BlockSpec((dv, d_model), lambda h: (h, 0)), rows, vec(d_model)],
        out_specs=rows,
        out_shape=jax.ShapeDtypeStruct((N_META, d_model), F32),
        scratch_shapes=[pltpu.VMEM((N_META, d_model), F32)],
        compiler_params=pltpu.CompilerParams(dimension_semantics=("arbitrary",)),
        name="l0_meta_attn_out",
    )(qkvg_meta, qkvg_meta, qkvg_meta, qkvg_meta, lq1, lk1, lq2, lk2, subg, w_out, x, post_g)


def _tail_kernel(a_ref, x_ref, hm_ref, w_out0_ref, post_g0_ref, pre_g_ref, w_in_u_ref,
                 w_in_g_ref, w_grp_ref, scale_ref, w_out_ref, post_g_ref, o_ref, u_ext_ref,
                 z_ref, *, tm, d, halo):
    i = pl.program_id(1)
    pre_g = pre_g_ref[...]

    @pl.when(i == 0)
    def _():
        hmn = _rms(hm_ref[...], pre_g).astype(BF16)
        u_ext_ref[0:halo, :] = _dot(hmn, w_in_u_ref[...])

    @pl.when(i > 0)
    def _():
        u_ext_ref[0:halo, :] = u_ext_ref[tm:tm + halo, :]

    h = x_ref[...] + _rms(_dot(a_ref[...], w_out0_ref[...]), post_g0_ref[...])
    o_ref[...] = h
    hn = _rms(h, pre_g).astype(BF16)
    u_ext_ref[halo:halo + tm, :] = _dot(hn, w_in_u_ref[...])

    gsz = d // len(POOL_WINDOWS)
    for gi, w in enumerate(POOL_WINDOWS):
        c0 = gi * gsz
        u = u_ext_ref[halo:halo + tm, c0:c0 + gsz]
        s = u
        for back in range(1, w):
            s = s + u_ext_ref[halo - back:halo - back + tm, c0:c0 + gsz]
        mixed = (s * (1.0 / w) - u).astype(BF16)
        mo = _dot(mixed, w_grp_ref[gi]) * scale_ref[:, c0:c0 + gsz]
        gate = _dot(hn, w_in_g_ref[:, c0:c0 + gsz])
        z_ref[:, c0:c0 + gsz] = (mo * _silu(gate)).astype(BF16)

    y = _dot(z_ref[...], w_out_ref[...])
    o_ref[...] = o_ref[...] + _rms(y, post_g_ref[...])


def _tail(a, x, h_meta, w_out0, post_g0, pre_g, w_in_u, w_in_g, w_grp, scale, w_out, post_g, *,
          batch, seq, tm):
    m, d = x.shape
    nt = seq // tm
    halo = N_META
    assert max(POOL_WINDOWS) <= halo and halo <= tm
    const2 = lambda b, i: (0, 0)
    row_tile = lambda b, i: (b * nt + i, 0)
    resident = lambda shape, imap: pl.BlockSpec(shape, imap, pipeline_mode=pl.Buffered(1))
    kern = functools.partial(_tail_kernel, tm=tm, d=d, halo=halo)
    return pl.pallas_call(
        kern,
        grid=(batch, nt),
        in_specs=[
            pl.BlockSpec((tm, d), row_tile),
            pl.BlockSpec((tm, d), row_tile),
            resident((N_META, d), const2),
            resident((d, d), const2),
            pl.BlockSpec((1, d), const2),
            pl.BlockSpec((1, d), const2),
            resident((d, d), const2),
            resident((d, d), const2),
            resident(w_grp.shape, lambda b, i: (0, 0, 0)),
            pl.BlockSpec((1, d), const2),
            resident((d, d), const2),
            pl.BlockSpec((1, d), const2),
        ],
        out_specs=pl.BlockSpec((tm, d), row_tile),
        out_shape=jax.ShapeDtypeStruct((m, d), F32),
        scratch_shapes=[pltpu.VMEM((halo + tm, d), F32), pltpu.VMEM((tm, d), BF16)],
        compiler_params=pltpu.CompilerParams(
            dimension_semantics=("arbitrary", "arbitrary"),
            vmem_limit_bytes=VMEM_LIMIT_BYTES),
        name="l0_out_l1_pool",
    )(a, x, h_meta, w_out0, post_g0, pre_g, w_in_u, w_in_g, w_grp, scale, w_out, post_g)


def _rope_tables(length, rope_dim):
    half = rope_dim // 2
    pos = np.arange(length, dtype=np.float32)
    inv = np.float32(ROPE_THETA) ** (-np.arange(0, rope_dim, 2, dtype=np.float32)
                                     / np.float32(rope_dim))
    ang = pos[:, None] * inv[None, :]
    cos, sin = np.cos(ang), np.sin(ang)
    zeros = lambda n: np.zeros((length, n), np.float32)
    cos_t = np.concatenate([cos, cos, np.ones((length, LANES - rope_dim), np.float32)], axis=1)
    sa_t = np.concatenate([zeros(half), sin, zeros(LANES - rope_dim)], axis=1)
    sb_t = np.concatenate([-sin, zeros(LANES - half)], axis=1)
    return cos_t, sa_t, sb_t


def kernel(x, meta_tokens, pre_norm_g, post_norm_g, attn_w_in, attn_w_out, attn_lambda_q1,
           attn_lambda_k1, attn_lambda_q2, attn_lambda_k2, attn_subln_g, pool_w_in,
           pool_w_group, pool_scale, pool_w_out):
    batch, seq, d = x.shape
    dv = d // DA_HEADS
    dh = dv // 2
    assert dh == LANES and seq % CHUNK == 0
    xf = x.reshape(batch * seq, d)
    meta = meta_tokens.astype(x.dtype)

    cos_t, sa_t, sb_t = _rope_tables(N_META + seq, dh // 4)
    pre_g0, pre_g1 = pre_norm_g[0:1], pre_norm_g[1:2]
    post_g0, post_g1 = post_norm_g[0:1], post_norm_g[1:2]
    q_scale = float(dh) ** -0.5 * math.log2(math.e)
    lam = [p[0:1] for p in (attn_lambda_q1, attn_lambda_k1, attn_lambda_q2, attn_lambda_k2)]
    subg = attn_subln_g[0:1]

    qkvg_meta, w_in0 = _norm_proj(
        meta, pre_g0, attn_w_in[0], cos_t[:N_META], sa_t[:N_META], sb_t[:N_META], tm=N_META,
        tn=PROJ_META_TN, rows_per_seq=N_META, q_scale=q_scale, emit_weight=True)
    qkvg = _norm_proj(xf, pre_g0, w_in0, cos_t[N_META:], sa_t[N_META:], sb_t[N_META:],
                      tm=PROJ_TM, tn=PROJ_TN, rows_per_seq=seq, q_scale=q_scale)
    side_weights = [(attn_w_out[0], 0), (pool_w_in[0], 0), (pool_w_in[0], 1), (pool_w_out[0], 0)]
    a, w_out0, w_in1_u, w_in1_g, w_out1 = _attention(
        qkvg, qkvg_meta, *lam, subg, side_weights, batch=batch, seq=seq, d_model=d)
    h1_meta = _meta_layer0(qkvg_meta, *lam, subg, w_out0, meta, post_g0, d_model=d)

    out = _tail(a, xf, h1_meta, w_out0, post_g0, pre_g1, w_in1_u, w_in1_g,
                pool_w_group[0].astype(BF16), pool_scale[0:1], w_out1, post_g1,
                batch=batch, seq=seq, tm=POOL_TM)
    return out.reshape(batch, seq, d)
```

```python
import functools
import math

import jax
import jax.numpy as jnp
import numpy as np
from jax import lax
from jax.experimental import pallas as pl
from jax.experimental.pallas import tpu as pltpu

F32 = jnp.float32
BF16 = jnp.bfloat16

N_META = 16
CHUNK = 64
RMS_EPS = 1e-6
DA_HEADS = 8
ROPE_THETA = 500000.0
POOL_WINDOWS = (2, 4, 8, 16)
LAMBDA_INIT_L0 = 0.8 - 0.6 * math.exp(-0.3 * 0)

LANES = 128
VMEM_LIMIT_BYTES = 58 * 1024 * 1024

PROJ_TM = 1024
PROJ_TN = 2048
PROJ_META_TN = 512
PROJ_SUB_ROWS = 256
PROJ_SUB_COLS = 256
ATTN_TQ = 2048
ATTN_TK = 256
ATTN_SUBTILES = 8
ATTN_KEY_UNROLL = 4
POOL_TM = 256

NEG_INF = float(jnp.finfo(jnp.float32).min)


def _rms(x, g):
    ms = jnp.mean(x * x, axis=-1, keepdims=True)
    return x * lax.rsqrt(ms + RMS_EPS) * g


def _silu(g):
    return g * (1.0 / (1.0 + jnp.exp(-g)))


def _dot(a, b):
    return jnp.dot(a, b, preferred_element_type=F32)


def _dot_nt(a, b):
    return lax.dot_general(a, b, (((1,), (1,)), ((), ())), preferred_element_type=F32)


def _norm_proj_kernel(x_ref, g_ref, w_ref, cos_ref, sa_ref, sb_ref, o_ref, *rest,
                      n_col_tiles_per_part, q_scale, sub_rows, sub_cols, emit_weight):
    wb_ref, hn_ref = rest if emit_weight else (None,) + rest
    j = pl.program_id(1)
    npt = n_col_tiles_per_part

    @pl.when(j == 0)
    def _():
        hn_ref[...] = _rms(x_ref[...], g_ref[...]).astype(BF16)

    tm, tn = o_ref.shape
    mc, nc = min(tm, sub_rows), min(tn, sub_cols)
    is_rope = j < 2 * npt
    is_gate = j >= 3 * npt
    scale = jnp.where(j < npt, q_scale, 1.0).astype(F32)
    half = 16
    for mi in range(tm // mc):
        rows = slice(mi * mc, (mi + 1) * mc)
        c = jnp.where(is_rope, cos_ref[rows, :] * scale, 1.0)
        sa = jnp.where(is_rope, sa_ref[rows, :] * scale, 0.0)
        sb = jnp.where(is_rope, sb_ref[rows, :] * scale, 0.0)
        for ni in range(tn // nc):
            w_sub = w_ref[:, ni * nc:(ni + 1) * nc]
            if emit_weight:
                w_sub = w_sub.astype(BF16)
                if mi == 0:
                    wb_ref[:, ni * nc:(ni + 1) * nc] = w_sub
            acc = _dot(hn_ref[rows, :], w_sub)
            for blk in range(nc // LANES):
                t = acc[:, blk * LANES:(blk + 1) * LANES]
                r = t * c + pltpu.roll(t, half, 1) * sa + pltpu.roll(t, LANES - half, 1) * sb
                f = jnp.where(is_gate, 1.0 / (1.0 + jnp.exp(-t)), 1.0)
                col = ni * nc + blk * LANES
                o_ref[rows, col:col + LANES] = (r * f).astype(o_ref.dtype)


def _norm_proj(x, g, w, cos_t, sa_t, sb_t, *, tm, tn, rows_per_seq, q_scale,
               emit_weight=False):
    m, d = x.shape
    n = w.shape[1]
    tiles_per_seq = rows_per_seq // tm
    assert not emit_weight or m == tm
    kern = functools.partial(_norm_proj_kernel, n_col_tiles_per_part=(n // 4) // tn,
                             q_scale=q_scale, sub_rows=PROJ_SUB_ROWS, sub_cols=PROJ_SUB_COLS,
                             emit_weight=emit_weight)
    tab_spec = pl.BlockSpec((tm, LANES), lambda i, j: (i % tiles_per_seq, 0))
    out_specs = pl.BlockSpec((tm, tn), lambda i, j: (i, j))
    out_shape = jax.ShapeDtypeStruct((m, n), BF16)
    if emit_weight:
        out_specs = [out_specs, pl.BlockSpec((d, tn), lambda i, j: (0, j))]
        out_shape = [out_shape, jax.ShapeDtypeStruct((d, n), BF16)]
    return pl.pallas_call(
        kern,
        grid=(m // tm, n // tn),
        in_specs=[
            pl.BlockSpec((tm, d), lambda i, j: (i, 0)),
            pl.BlockSpec((1, d), lambda i, j: (0, 0)),
            pl.BlockSpec((d, tn), lambda i, j: (0, j)),
            tab_spec, tab_spec, tab_spec,
        ],
        out_specs=out_specs,
        out_shape=out_shape,
        scratch_shapes=[pltpu.VMEM((tm, d), BF16)],
        compiler_params=pltpu.CompilerParams(
            dimension_semantics=("parallel", "arbitrary"),
            vmem_limit_bytes=VMEM_LIMIT_BYTES),
        name="l0_norm_proj",
    )(x, g, w, cos_t, sa_t, sb_t)


def _lambda_full(lq1_ref, lk1_ref, lq2_ref, lk2_ref):
    a = jnp.sum(lq1_ref[...] * lk1_ref[...], axis=-1, keepdims=True)
    b = jnp.sum(lq2_ref[...] * lk2_ref[...], axis=-1, keepdims=True)
    return jnp.exp(a) - jnp.exp(b) + LAMBDA_INIT_L0


def _diff_finalize(acc1, acc2, inv_l1, inv_l2, lam, subg, gate):
    o = acc1 * inv_l1 - acc2 * (lam * inv_l2)
    o = _rms(o, subg * (1.0 - LAMBDA_INIT_L0))
    return (o * gate.astype(F32)).astype(BF16)


def _attn_kernel(q_ref, k_ref, v_ref, gate_ref, km_ref, vm_ref, lq1_ref, lk1_ref, lq2_ref,
                 lk2_ref, subg_ref, w0_ref, w1_ref, w2_ref, w3_ref, o_ref, wb0_ref, wb1_ref,
                 wb2_ref, wb3_ref, m_ref, l_ref, acc_ref, *, tq, tk, dh, n_sub, unroll,
                 conv_steps):
    i = pl.program_id(2)
    ts = tq // n_sub
    dv = acc_ref.shape[2]

    step = (pl.program_id(0) * pl.num_programs(1) + pl.program_id(1)) * pl.num_programs(2) + i
    for k, (w_ref, wb_ref) in enumerate(((w0_ref, wb0_ref), (w1_ref, wb1_ref),
                                         (w2_ref, wb2_ref), (w3_ref, wb3_ref))):
        @pl.when(jnp.logical_and(step >= k * conv_steps, step < (k + 1) * conv_steps))
        def _(w_ref=w_ref, wb_ref=wb_ref):
            wb_ref[...] = w_ref[...].astype(wb_ref.dtype)

    def scores(c, kb):
        q = q_ref[c * ts:(c + 1) * ts, :]
        return jnp.concatenate([_dot_nt(q[:, :dh], kb[:, :dh]), _dot_nt(q[:, dh:], kb[:, dh:])],
                               axis=0)

    def lane_sum(p):
        psum = p[:, 0:LANES]
        for blk in range(1, p.shape[1] // LANES):
            psum = psum + p[:, blk * LANES:(blk + 1) * LANES]
        return psum

    row = lax.broadcasted_iota(jnp.int32, (2 * ts, tk), 0) % ts
    col = lax.broadcasted_iota(jnp.int32, (2 * ts, tk), 1)
    visible = col // CHUNK <= row // CHUNK
    for c in range(n_sub):
        off = pl.multiple_of(i * tq + c * ts, ts)
        s = jnp.where(visible, scores(c, k_ref[pl.ds(off, tk), :]), NEG_INF)
        m0 = jnp.max(s, axis=-1, keepdims=True)
        p = jnp.exp2(s - m0)
        m_ref[c] = jnp.broadcast_to(m0, (2 * ts, LANES))
        l_ref[c] = lane_sum(p)
        acc_ref[c] = _dot(p.astype(BF16), v_ref[pl.ds(off, tk), :])

    def update(c, s, vb):
        m_prev = m_ref[c]
        m_new = jnp.maximum(m_prev, jnp.max(s, axis=-1, keepdims=True))
        alpha = jnp.exp2(m_prev - m_new)
        p = jnp.exp2(s - jnp.tile(m_new, (1, s.shape[1] // LANES)))
        l_ref[c] = alpha * l_ref[c] + lane_sum(p)
        m_ref[c] = m_new
        acc_ref[c] = (acc_ref[c] * jnp.tile(alpha, (1, dv // LANES))
                      + _dot(p.astype(BF16), vb))

    def full_blocks(jb, carry):
        for u in range(unroll):
            off = pl.multiple_of((jb * unroll + u) * tk, tk)
            kb = k_ref[pl.ds(off, tk), :]
            vb = v_ref[pl.ds(off, tk), :]
            for c in range(n_sub):
                update(c, scores(c, kb), vb)
        return carry

    lax.fori_loop(0, i * (tq // tk // unroll), full_blocks, 0)

    for d in range(n_sub - 1):
        off = pl.multiple_of(i * tq + d * tk, tk)
        kb = k_ref[pl.ds(off, tk), :]
        vb = v_ref[pl.ds(off, tk), :]
        for c in range(d + 1, n_sub):
            update(c, scores(c, kb), vb)

    lam = _lambda_full(lq1_ref, lk1_ref, lq2_ref, lk2_ref)
    km = km_ref[...]
    n_meta = km.shape[0]
    zfeat = jnp.zeros((n_meta, dh), km.dtype)
    zpad = jnp.zeros((LANES - n_meta, 2 * dh), km.dtype)
    km_bd = jnp.concatenate([jnp.concatenate([km[:, :dh], zfeat], axis=1), zpad,
                             jnp.concatenate([zfeat, km[:, dh:]], axis=1), zpad], axis=0)
    vm_pad = jnp.concatenate([vm_ref[...], jnp.zeros((LANES - n_meta, dv), km.dtype)], axis=0)
    real_key = lax.broadcasted_iota(jnp.int32, (2 * ts, LANES), 1) < n_meta
    for c in range(n_sub):
        r = _dot_nt(q_ref[c * ts:(c + 1) * ts, :], km_bd)
        s = jnp.where(real_key, jnp.concatenate([r[:, :LANES], r[:, LANES:]], axis=0), NEG_INF)
        m_prev = m_ref[c]
        m_new = jnp.maximum(m_prev, jnp.max(s, axis=-1, keepdims=True))
        alpha = jnp.exp2(m_prev - m_new)
        p = jnp.exp2(s - m_new)
        l = alpha * l_ref[c] + p
        acc = (acc_ref[c] * jnp.tile(alpha, (1, dv // LANES)) + _dot(p.astype(BF16), vm_pad))
        inv_l = 1.0 / jnp.sum(l, axis=-1, keepdims=True)
        o_ref[c * ts:(c + 1) * ts, :] = _diff_finalize(
            acc[:ts], acc[ts:], inv_l[:ts], inv_l[ts:], lam, subg_ref[...],
            gate_ref[c * ts:(c + 1) * ts, :])


def _attention(qkvg, qkvg_meta, lq1, lk1, lq2, lk2, subg, side_weights, *, batch, seq,
               d_model):
    dv = d_model // DA_HEADS
    dh = dv // 2
    tq, tk, n_sub = ATTN_TQ, ATTN_TK, ATTN_SUBTILES
    ts = tq // n_sub
    unroll = ATTN_KEY_UNROLL
    assert ts % CHUNK == 0 and tk % CHUNK == 0 and tq % (tk * unroll) == 0
    nq = seq // tq
    h_ = DA_HEADS
    n_steps = batch * h_ * nq
    conv_steps = n_steps // len(side_weights)
    conv_rows = d_model // conv_steps
    assert len(side_weights) == 4 and conv_steps * 4 == n_steps and conv_rows * conv_steps == d_model
    kern = functools.partial(_attn_kernel, tq=tq, tk=tk, dh=dh, n_sub=n_sub, unroll=unroll,
                             conv_steps=conv_steps)
    vec = lambda n: pl.BlockSpec((1, n), lambda b, h, i: (0, 0))

    def conv_spec(k, col_block):
        def imap(b, h, i):
            step = (b * h_ + h) * nq + i
            return (jnp.clip(step - k * conv_steps, 0, conv_steps - 1), col_block)
        return pl.BlockSpec((conv_rows, d_model), imap)

    side_arrays = [w for w, _ in side_weights]
    out_specs = [pl.BlockSpec((tq, dv), lambda b, h, i: (b * nq + i, h))]
    out_specs += [conv_spec(k, 0) for k in range(len(side_weights))]
    out_shape = [jax.ShapeDtypeStruct((batch * seq, d_model), BF16)]
    out_shape += [jax.ShapeDtypeStruct((d_model, d_model), BF16) for _ in side_weights]
    return pl.pallas_call(
        kern,
        grid=(batch, h_, nq),
        in_specs=[
            pl.BlockSpec((tq, dv), lambda b, h, i: (b * nq + i, h)),
            pl.BlockSpec((seq, dv), lambda b, h, i: (b, h_ + h)),
            pl.BlockSpec((seq, dv), lambda b, h, i: (b, 2 * h_ + h)),
            pl.BlockSpec((tq, dv), lambda b, h, i: (b * nq + i, 3 * h_ + h)),
            pl.BlockSpec((N_META, dv), lambda b, h, i: (0, h_ + h)),
            pl.BlockSpec((N_META, dv), lambda b, h, i: (0, 2 * h_ + h)),
            vec(dh), vec(dh), vec(dh), vec(dh), vec(dv),
        ] + [conv_spec(k, cb) for k, (_, cb) in enumerate(side_weights)],
        out_specs=out_specs,
        out_shape=out_shape,
        scratch_shapes=[pltpu.VMEM((n_sub, 2 * ts, LANES), F32),
                        pltpu.VMEM((n_sub, 2 * ts, LANES), F32),
                        pltpu.VMEM((n_sub, 2 * ts, dv), F32)],
        compiler_params=pltpu.CompilerParams(
            dimension_semantics=("arbitrary", "arbitrary", "arbitrary"),
            vmem_limit_bytes=VMEM_LIMIT_BYTES),
        name="l0_diff_attn",
    )(qkvg, qkvg, qkvg, qkvg, qkvg_meta, qkvg_meta, lq1, lk1, lq2, lk2, subg, *side_arrays)


def _meta_layer0_kernel(q_ref, k_ref, v_ref, gate_ref, lq1_ref, lk1_ref, lq2_ref, lk2_ref,
                        subg_ref, w_ref, x_ref, g_ref, o_ref, y_ref, *, dh):
    h = pl.program_id(0)
    q = q_ref[...]
    k = k_ref[...]
    v = v_ref[...]

    def attend(qm, km):
        s = _dot_nt(qm, km)
        p = jnp.exp2(s - jnp.max(s, axis=-1, keepdims=True))
        return _dot(p.astype(BF16), v), 1.0 / jnp.sum(p, axis=-1, keepdims=True)

    lam = _lambda_full(lq1_ref, lk1_ref, lq2_ref, lk2_ref)
    acc1, inv_l1 = attend(q[:, :dh], k[:, :dh])
    acc2, inv_l2 = attend(q[:, dh:], k[:, dh:])
    a = _diff_finalize(acc1, acc2, inv_l1, inv_l2, lam, subg_ref[...], gate_ref[...])
    part = _dot(a, w_ref[...])

    @pl.when(h == 0)
    def _():
        y_ref[...] = part

    @pl.when(h > 0)
    def _():
        y_ref[...] += part

    @pl.when(h == pl.num_programs(0) - 1)
    def _():
        o_ref[...] = x_ref[...] + _rms(y_ref[...], g_ref[...])


def _meta_layer0(qkvg_meta, lq1, lk1, lq2, lk2, subg, w_out, x, post_g, *, d_model):
    dv = d_model // DA_HEADS
    dh = dv // 2
    h_ = DA_HEADS
    vec = lambda n: pl.BlockSpec((1, n), lambda h: (0, 0))
    blk = lambda part: pl.BlockSpec((N_META, dv), lambda h: (0, part * h_ + h))
    rows = pl.BlockSpec((N_META, d_model), lambda h: (0, 0))
    return pl.pallas_call(
        functools.partial(_meta_layer0_kernel, dh=dh),
        grid=(h_,),
        in_specs=[blk(0), blk(1), blk(2), blk(3), vec(dh), vec(dh), vec(dh), vec(dh), vec(dv),
                  pl.BlockSpec((dv, d_model), lambda h: (h, 0)), rows, vec(d_model)],
        out_specs=rows,
        out_shape=jax.ShapeDtypeStruct((N_META, d_model), F32),
        scratch_shapes=[pltpu.VMEM((N_META, d_model), F32)],
        compiler_params=pltpu.CompilerParams(dimension_semantics=("arbitrary",)),
        name="l0_meta_attn_out",
    )(qkvg_meta, qkvg_meta, qkvg_meta, qkvg_meta, lq1, lk1, lq2, lk2, subg, w_out, x, post_g)


def _tail_kernel(a_first_ref, a_next_ref, x_ref, hm_ref, w_out0_ref, post_g0_ref, pre_g_ref,
                 w_in_u_ref, w_in_g_ref, w_grp_ref, scale_ref, w_out_ref, post_g_ref, o_ref,
                 y0_ref, u_ext_ref, z_ref, *, tm, d, halo):
    i = pl.program_id(1)
    pre_g = pre_g_ref[...]

    @pl.when(jnp.logical_and(pl.program_id(0) == 0, i == 0))
    def _():
        y0_ref[...] = _dot(a_first_ref[...], w_out0_ref[...])

    @pl.when(i == 0)
    def _():
        hmn = _rms(hm_ref[...], pre_g).astype(BF16)
        u_ext_ref[0:halo, :] = _dot(hmn, w_in_u_ref[...])

    @pl.when(i > 0)
    def _():
        u_ext_ref[0:halo, :] = u_ext_ref[tm:tm + halo, :]

    h = x_ref[...] + _rms(y0_ref[...], post_g0_ref[...])
    o_ref[...] = h
    y0_ref[...] = _dot(a_next_ref[...], w_out0_ref[...])
    hn = _rms(h, pre_g).astype(BF16)
    u_ext_ref[halo:halo + tm, :] = _dot(hn, w_in_u_ref[...])

    gsz = d // len(POOL_WINDOWS)
    for gi, w in enumerate(POOL_WINDOWS):
        c0 = gi * gsz
        u = u_ext_ref[halo:halo + tm, c0:c0 + gsz]
        s = u
        for back in range(1, w):
            s = s + u_ext_ref[halo - back:halo - back + tm, c0:c0 + gsz]
        mixed = (s * (1.0 / w) - u).astype(BF16)
        mo = _dot(mixed, w_grp_ref[gi]) * scale_ref[:, c0:c0 + gsz]
        gate = _dot(hn, w_in_g_ref[:, c0:c0 + gsz])
        z_ref[:, c0:c0 + gsz] = (mo * _silu(gate)).astype(BF16)

    y = _dot(z_ref[...], w_out_ref[...])
    o_ref[...] = o_ref[...] + _rms(y, post_g_ref[...])


def _tail(a, x, h_meta, w_out0, post_g0, pre_g, w_in_u, w_in_g, w_grp, scale, w_out, post_g, *,
          batch, seq, tm):
    m, d = x.shape
    nt = seq // tm
    halo = N_META
    assert max(POOL_WINDOWS) <= halo and halo <= tm
    const2 = lambda b, i: (0, 0)
    row_tile = lambda b, i: (b * nt + i, 0)
    next_row_tile = lambda b, i: (jnp.minimum(b * nt + i + 1, batch * nt - 1), 0)
    resident = lambda shape, imap: pl.BlockSpec(shape, imap, pipeline_mode=pl.Buffered(1))
    kern = functools.partial(_tail_kernel, tm=tm, d=d, halo=halo)
    return pl.pallas_call(
        kern,
        grid=(batch, nt),
        in_specs=[
            resident((tm, d), const2),
            pl.BlockSpec((tm, d), next_row_tile),
            pl.BlockSpec((tm, d), row_tile),
            resident((N_META, d), const2),
            resident((d, d), const2),
            pl.BlockSpec((1, d), const2),
            pl.BlockSpec((1, d), const2),
            resident((d, d), const2),
            resident((d, d), const2),
            resident(w_grp.shape, lambda b, i: (0, 0, 0)),
            pl.BlockSpec((1, d), const2),
            resident((d, d), const2),
            pl.BlockSpec((1, d), const2),
        ],
        out_specs=pl.BlockSpec((tm, d), row_tile),
        out_shape=jax.ShapeDtypeStruct((m, d), F32),
        scratch_shapes=[pltpu.VMEM((tm, d), F32), pltpu.VMEM((halo + tm, d), F32),
                        pltpu.VMEM((tm, d), BF16)],
        compiler_params=pltpu.CompilerParams(
            dimension_semantics=("arbitrary", "arbitrary"),
            vmem_limit_bytes=VMEM_LIMIT_BYTES),
        name="l0_out_l1_pool",
    )(a, a, x, h_meta, w_out0, post_g0, pre_g, w_in_u, w_in_g, w_grp, scale, w_out, post_g)


def _rope_tables(length, rope_dim):
    half = rope_dim // 2
    pos = np.arange(length, dtype=np.float32)
    inv = np.float32(ROPE_THETA) ** (-np.arange(0, rope_dim, 2, dtype=np.float32)
                                     / np.float32(rope_dim))
    ang = pos[:, None] * inv[None, :]
    cos, sin = np.cos(ang), np.sin(ang)
    zeros = lambda n: np.zeros((length, n), np.float32)
    cos_t = np.concatenate([cos, cos, np.ones((length, LANES - rope_dim), np.float32)], axis=1)
    sa_t = np.concatenate([zeros(half), sin, zeros(LANES - rope_dim)], axis=1)
    sb_t = np.concatenate([-sin, zeros(LANES - half)], axis=1)
    return cos_t, sa_t, sb_t


def kernel(x, meta_tokens, pre_norm_g, post_norm_g, attn_w_in, attn_w_out, attn_lambda_q1,
           attn_lambda_k1, attn_lambda_q2, attn_lambda_k2, attn_subln_g, pool_w_in,
           pool_w_group, pool_scale, pool_w_out):
    batch, seq, d = x.shape
    dv = d // DA_HEADS
    dh = dv // 2
    assert dh == LANES and seq % CHUNK == 0
    xf = x.reshape(batch * seq, d)
    meta = meta_tokens.astype(x.dtype)

    cos_t, sa_t, sb_t = _rope_tables(N_META + seq, dh // 4)
    pre_g0, pre_g1 = pre_norm_g[0:1], pre_norm_g[1:2]
    post_g0, post_g1 = post_norm_g[0:1], post_norm_g[1:2]
    q_scale = float(dh) ** -0.5 * math.log2(math.e)
    lam = [p[0:1] for p in (attn_lambda_q1, attn_lambda_k1, attn_lambda_q2, attn_lambda_k2)]
    subg = attn_subln_g[0:1]

    qkvg_meta, w_in0 = _norm_proj(
        meta, pre_g0, attn_w_in[0], cos_t[:N_META], sa_t[:N_META], sb_t[:N_META], tm=N_META,
        tn=PROJ_META_TN, rows_per_seq=N_META, q_scale=q_scale, emit_weight=True)
    qkvg = _norm_proj(xf, pre_g0, w_in0, cos_t[N_META:], sa_t[N_META:], sb_t[N_META:],
                      tm=PROJ_TM, tn=PROJ_TN, rows_per_seq=seq, q_scale=q_scale)
    side_weights = [(attn_w_out[0], 0), (pool_w_in[0], 0), (pool_w_in[0], 1), (pool_w_out[0], 0)]
    a, w_out0, w_in1_u, w_in1_g, w_out1 = _attention(
        qkvg, qkvg_meta, *lam, subg, side_weights, batch=batch, seq=seq, d_model=d)
    h1_meta = _meta_layer0(qkvg_meta, *lam, subg, w_out0, meta, post_g0, d_model=d)

    out = _tail(a, xf, h1_meta, w_out0, post_g0, pre_g1, w_in1_u, w_in1_g,
                pool_w_group[0].astype(BF16), pool_scale[0:1], w_out1, post_g1,
                batch=batch, seq=seq, tm=POOL_TM)
    return out.reshape(batch, seq, d)
```

```python
import functools
import math

import jax
import jax.numpy as jnp
import numpy as np
from jax import lax
from jax.experimental import pallas as pl
from jax.experimental.pallas import tpu as pltpu

F32 = jnp.float32
BF16 = jnp.bfloat16

N_META = 16
CHUNK = 64
RMS_EPS = 1e-6
DA_HEADS = 8
ROPE_THETA = 500000.0
POOL_WINDOWS = (2, 4, 8, 16)
LAMBDA_INIT_L0 = 0.8 - 0.6 * math.exp(-0.3 * 0)

LANES = 128
VMEM_LIMIT_BYTES = 56 * 1024 * 1024

PROJ_TM = 1024
PROJ_TN = 2048
PROJ_META_TN = 1024
PROJ_SUB_ROWS = 256
PROJ_SUB_COLS = 256
ATTN_TQ = 2048
ATTN_TK = 256
ATTN_SUBTILES = 8
ATTN_KEY_UNROLL = 4
POOL_TM = 256

NEG_INF = float(jnp.finfo(jnp.float32).min)


def _rms(x, g):
    ms = jnp.mean(x * x, axis=-1, keepdims=True)
    return x * lax.rsqrt(ms + RMS_EPS) * g


def _silu(g):
    return g * (1.0 / (1.0 + jnp.exp(-g)))


def _dot(a, b):
    return jnp.dot(a, b, preferred_element_type=F32)


def _dot_nt(a, b):
    return lax.dot_general(a, b, (((1,), (1,)), ((), ())), preferred_element_type=F32)


def _norm_proj_kernel(x_ref, g_ref, w_ref, cos_ref, sa_ref, sb_ref, o_ref, *rest,
                      n_col_tiles_per_part, q_scale, sub_rows, sub_cols, emit_weight):
    wb_ref, hn_ref = rest if emit_weight else (None,) + rest
    j = pl.program_id(1)
    npt = n_col_tiles_per_part

    @pl.when(j == 0)
    def _():
        hn_ref[...] = _rms(x_ref[...], g_ref[...]).astype(BF16)

    tm, tn = o_ref.shape
    mc, nc = min(tm, sub_rows), min(tn, sub_cols)
    is_rope = j < 2 * npt
    is_gate = j >= 3 * npt
    scale = jnp.where(j < npt, q_scale, 1.0).astype(F32)
    half = 16
    for mi in range(tm // mc):
        rows = slice(mi * mc, (mi + 1) * mc)
        c = jnp.where(is_rope, cos_ref[rows, :] * scale, 1.0)
        sa = jnp.where(is_rope, sa_ref[rows, :] * scale, 0.0)
        sb = jnp.where(is_rope, sb_ref[rows, :] * scale, 0.0)
        for ni in range(tn // nc):
            w_sub = w_ref[:, ni * nc:(ni + 1) * nc]
            if emit_weight:
                w_sub = w_sub.astype(BF16)
                if mi == 0:
                    wb_ref[:, ni * nc:(ni + 1) * nc] = w_sub
            acc = _dot(hn_ref[rows, :], w_sub)
            for blk in range(nc // LANES):
                t = acc[:, blk * LANES:(blk + 1) * LANES]
                r = t * c + pltpu.roll(t, half, 1) * sa + pltpu.roll(t, LANES - half, 1) * sb
                f = jnp.where(is_gate, 1.0 / (1.0 + jnp.exp(-t)), 1.0)
                col = ni * nc + blk * LANES
                o_ref[rows, col:col + LANES] = (r * f).astype(o_ref.dtype)


def _norm_proj(x, g, w, cos_t, sa_t, sb_t, *, tm, tn, rows_per_seq, q_scale,
               emit_weight=False):
    m, d = x.shape
    n = w.shape[1]
    tiles_per_seq = rows_per_seq // tm
    assert not emit_weight or m == tm
    kern = functools.partial(_norm_proj_kernel, n_col_tiles_per_part=(n // 4) // tn,
                             q_scale=q_scale, sub_rows=PROJ_SUB_ROWS, sub_cols=PROJ_SUB_COLS,
                             emit_weight=emit_weight)
    tab_spec = pl.BlockSpec((tm, LANES), lambda i, j: (i % tiles_per_seq, 0))
    out_specs = pl.BlockSpec((tm, tn), lambda i, j: (i, j))
    out_shape = jax.ShapeDtypeStruct((m, n), BF16)
    if emit_weight:
        out_specs = [out_specs, pl.BlockSpec((d, tn), lambda i, j: (0, j))]
        out_shape = [out_shape, jax.ShapeDtypeStruct((d, n), BF16)]
    return pl.pallas_call(
        kern,
        grid=(m // tm, n // tn),
        in_specs=[
            pl.BlockSpec((tm, d), lambda i, j: (i, 0)),
            pl.BlockSpec((1, d), lambda i, j: (0, 0)),
            pl.BlockSpec((d, tn), lambda i, j: (0, j)),
            tab_spec, tab_spec, tab_spec,
        ],
        out_specs=out_specs,
        out_shape=out_shape,
        scratch_shapes=[pltpu.VMEM((tm, d), BF16)],
        compiler_params=pltpu.CompilerParams(
            dimension_semantics=("parallel", "arbitrary"),
            vmem_limit_bytes=VMEM_LIMIT_BYTES),
        name="l0_norm_proj",
    )(x, g, w, cos_t, sa_t, sb_t)


def _lambda_full(lq1_ref, lk1_ref, lq2_ref, lk2_ref):
    a = jnp.sum(lq1_ref[...] * lk1_ref[...], axis=-1, keepdims=True)
    b = jnp.sum(lq2_ref[...] * lk2_ref[...], axis=-1, keepdims=True)
    return jnp.exp(a) - jnp.exp(b) + LAMBDA_INIT_L0


def _diff_finalize(acc1, acc2, inv_l1, inv_l2, lam, subg, gate):
    o = acc1 * inv_l1 - acc2 * (lam * inv_l2)
    o = _rms(o, subg * (1.0 - LAMBDA_INIT_L0))
    return (o * gate.astype(F32)).astype(BF16)


def _attn_kernel(q_ref, k_ref, v_ref, gate_ref, km_ref, vm_ref, lq1_ref, lk1_ref, lq2_ref,
                 lk2_ref, subg_ref, w0_ref, w1_ref, w2_ref, w3_ref, o_ref, wb0_ref, wb1_ref,
                 wb2_ref, wb3_ref, m_ref, l_ref, acc_ref, *, tq, tk, dh, n_sub, unroll,
                 conv_steps):
    i = pl.program_id(2)
    ts = tq // n_sub
    dv = acc_ref.shape[2]

    step = (pl.program_id(0) * pl.num_programs(1) + pl.program_id(1)) * pl.num_programs(2) + i
    for k, (w_ref, wb_ref) in enumerate(((w0_ref, wb0_ref), (w1_ref, wb1_ref),
                                         (w2_ref, wb2_ref), (w3_ref, wb3_ref))):
        @pl.when(jnp.logical_and(step >= k * conv_steps, step < (k + 1) * conv_steps))
        def _(w_ref=w_ref, wb_ref=wb_ref):
            wb_ref[...] = w_ref[...].astype(wb_ref.dtype)

    def scores(c, kb):
        q = q_ref[c * ts:(c + 1) * ts, :]
        return jnp.concatenate([_dot_nt(q[:, :dh], kb[:, :dh]), _dot_nt(q[:, dh:], kb[:, dh:])],
                               axis=0)

    def lane_sum(p):
        psum = p[:, 0:LANES]
        for blk in range(1, p.shape[1] // LANES):
            psum = psum + p[:, blk * LANES:(blk + 1) * LANES]
        return psum

    row = lax.broadcasted_iota(jnp.int32, (2 * ts, tk), 0) % ts
    col = lax.broadcasted_iota(jnp.int32, (2 * ts, tk), 1)
    visible = col // CHUNK <= row // CHUNK
    for c in range(n_sub):
        off = pl.multiple_of(i * tq + c * ts, ts)
        s = jnp.where(visible, scores(c, k_ref[pl.ds(off, tk), :]), NEG_INF)
        m0 = jnp.max(s, axis=-1, keepdims=True)
        p = jnp.exp2(s - m0)
        m_ref[c] = jnp.broadcast_to(m0, (2 * ts, LANES))
        l_ref[c] = lane_sum(p)
        acc_ref[c] = _dot(p.astype(BF16), v_ref[pl.ds(off, tk), :])

    def update(c, s, vb):
        m_prev = m_ref[c]
        m_new = jnp.maximum(m_prev, jnp.max(s, axis=-1, keepdims=True))
        alpha = jnp.exp2(m_prev - m_new)
        p = jnp.exp2(s - jnp.tile(m_new, (1, s.shape[1] // LANES)))
        l_ref[c] = alpha * l_ref[c] + lane_sum(p)
        m_ref[c] = m_new
        acc_ref[c] = (acc_ref[c] * jnp.tile(alpha, (1, dv // LANES))
                      + _dot(p.astype(BF16), vb))

    def full_blocks(jb, carry):
        for u in range(unroll):
            off = pl.multiple_of((jb * unroll + u) * tk, tk)
            kb = k_ref[pl.ds(off, tk), :]
            vb = v_ref[pl.ds(off, tk), :]
            for c in range(n_sub):
                update(c, scores(c, kb), vb)
        return carry

    lax.fori_loop(0, i * (tq // tk // unroll), full_blocks, 0)

    for d in range(n_sub - 1):
        off = pl.multiple_of(i * tq + d * tk, tk)
        kb = k_ref[pl.ds(off, tk), :]
        vb = v_ref[pl.ds(off, tk), :]
        for c in range(d + 1, n_sub):
            update(c, scores(c, kb), vb)

    lam = _lambda_full(lq1_ref, lk1_ref, lq2_ref, lk2_ref)
    km = km_ref[...]
    n_meta = km.shape[0]
    zfeat = jnp.zeros((n_meta, dh), km.dtype)
    zpad = jnp.zeros((LANES - n_meta, 2 * dh), km.dtype)
    km_bd = jnp.concatenate([jnp.concatenate([km[:, :dh], zfeat], axis=1), zpad,
                             jnp.concatenate([zfeat, km[:, dh:]], axis=1), zpad], axis=0)
    vm_pad = jnp.concatenate([vm_ref[...], jnp.zeros((LANES - n_meta, dv), km.dtype)], axis=0)
    real_key = lax.broadcasted_iota(jnp.int32, (2 * ts, LANES), 1) < n_meta
    for c in range(n_sub):
        r = _dot_nt(q_ref[c * ts:(c + 1) * ts, :], km_bd)
        s = jnp.where(real_key, jnp.concatenate([r[:, :LANES], r[:, LANES:]], axis=0), NEG_INF)
        m_prev = m_ref[c]
        m_new = jnp.maximum(m_prev, jnp.max(s, axis=-1, keepdims=True))
        alpha = jnp.exp2(m_prev - m_new)
        p = jnp.exp2(s - m_new)
        l = alpha * l_ref[c] + p
        acc = (acc_ref[c] * jnp.tile(alpha, (1, dv // LANES)) + _dot(p.astype(BF16), vm_pad))
        inv_l = 1.0 / jnp.sum(l, axis=-1, keepdims=True)
        o_ref[c * ts:(c + 1) * ts, :] = _diff_finalize(
            acc[:ts], acc[ts:], inv_l[:ts], inv_l[ts:], lam, subg_ref[...],
            gate_ref[c * ts:(c + 1) * ts, :])


def _attention(qkvg, qkvg_meta, lq1, lk1, lq2, lk2, subg, side_weights, *, batch, seq,
               d_model):
    dv = d_model // DA_HEADS
    dh = dv // 2
    tq, tk, n_sub = ATTN_TQ, ATTN_TK, ATTN_SUBTILES
    ts = tq // n_sub
    unroll = ATTN_KEY_UNROLL
    assert ts % CHUNK == 0 and tk % CHUNK == 0 and tq % (tk * unroll) == 0
    nq = seq // tq
    h_ = DA_HEADS
    n_steps = batch * h_ * nq
    conv_steps = n_steps // len(side_weights)
    conv_rows = d_model // conv_steps
    assert len(side_weights) == 4 and conv_steps * 4 == n_steps and conv_rows * conv_steps == d_model
    kern = functools.partial(_attn_kernel, tq=tq, tk=tk, dh=dh, n_sub=n_sub, unroll=unroll,
                             conv_steps=conv_steps)
    vec = lambda n: pl.BlockSpec((1, n), lambda b, h, i: (0, 0))

    def conv_spec(k, col_block):
        def imap(b, h, i):
            step = (b * h_ + h) * nq + i
            return (jnp.clip(step - k * conv_steps, 0, conv_steps - 1), col_block)
        return pl.BlockSpec((conv_rows, d_model), imap)

    side_arrays = [w for w, _ in side_weights]
    out_specs = [pl.BlockSpec((tq, dv), lambda b, h, i: (b * nq + i, h))]
    out_specs += [conv_spec(k, 0) for k in range(len(side_weights))]
    out_shape = [jax.ShapeDtypeStruct((batch * seq, d_model), BF16)]
    out_shape += [jax.ShapeDtypeStruct((d_model, d_model), BF16) for _ in side_weights]
    return pl.pallas_call(
        kern,
        grid=(batch, h_, nq),
        in_specs=[
            pl.BlockSpec((tq, dv), lambda b, h, i: (b * nq + i, h)),
            pl.BlockSpec((seq, dv), lambda b, h, i: (b, h_ + h)),
            pl.BlockSpec((seq, dv), lambda b, h, i: (b, 2 * h_ + h)),
            pl.BlockSpec((tq, dv), lambda b, h, i: (b * nq + i, 3 * h_ + h)),
            pl.BlockSpec((N_META, dv), lambda b, h, i: (0, h_ + h)),
            pl.BlockSpec((N_META, dv), lambda b, h, i: (0, 2 * h_ + h)),
            vec(dh), vec(dh), vec(dh), vec(dh), vec(dv),
        ] + [conv_spec(k, cb) for k, (_, cb) in enumerate(side_weights)],
        out_specs=out_specs,
        out_shape=out_shape,
        scratch_shapes=[pltpu.VMEM((n_sub, 2 * ts, LANES), F32),
                        pltpu.VMEM((n_sub, 2 * ts, LANES), F32),
                        pltpu.VMEM((n_sub, 2 * ts, dv), F32)],
        compiler_params=pltpu.CompilerParams(
            dimension_semantics=("arbitrary", "arbitrary", "arbitrary"),
            vmem_limit_bytes=VMEM_LIMIT_BYTES),
        name="l0_diff_attn",
    )(qkvg, qkvg, qkvg, qkvg, qkvg_meta, qkvg_meta, lq1, lk1, lq2, lk2, subg, *side_arrays)


def _meta_layer0_kernel(q_ref, k_ref, v_ref, gate_ref, lq1_ref, lk1_ref, lq2_ref, lk2_ref,
                        subg_ref, w_ref, x_ref, g_ref, o_ref, y_ref, *, dh):
    h = pl.program_id(0)
    q = q_ref[...]
    k = k_ref[...]
    v = v_ref[...]

    def attend(qm, km):
        s = _dot_nt(qm, km)
        p = jnp.exp2(s - jnp.max(s, axis=-1, keepdims=True))
        return _dot(p.astype(BF16), v), 1.0 / jnp.sum(p, axis=-1, keepdims=True)

    lam = _lambda_full(lq1_ref, lk1_ref, lq2_ref, lk2_ref)
    acc1, inv_l1 = attend(q[:, :dh], k[:, :dh])
    acc2, inv_l2 = attend(q[:, dh:], k[:, dh:])
    a = _diff_finalize(acc1, acc2, inv_l1, inv_l2, lam, subg_ref[...], gate_ref[...])
    part = _dot(a, w_ref[...])

    @pl.when(h == 0)
    def _():
        y_ref[...] = part

    @pl.when(h > 0)
    def _():
        y_ref[...] += part

    @pl.when(h == pl.num_programs(0) - 1)
    def _():
        o_ref[...] = x_ref[...] + _rms(y_ref[...], g_ref[...])


def _meta_layer0(qkvg_meta, lq1, lk1, lq2, lk2, subg, w_out, x, post_g, *, d_model):
    dv = d_model // DA_HEADS
    dh = dv // 2
    h_ = DA_HEADS
    vec = lambda n: pl.BlockSpec((1, n), lambda h: (0, 0))
    blk = lambda part: pl.BlockSpec((N_META, dv), lambda h: (0, part * h_ + h))
    rows = pl.BlockSpec((N_META, d_model), lambda h: (0, 0))
    return pl.pallas_call(
        functools.partial(_meta_layer0_kernel, dh=dh),
        grid=(h_,),
        in_specs=[blk(0), blk(1), blk(2), blk(3), vec(dh), vec(dh), vec(dh), vec(dh), vec(dv),
                  pl.BlockSpec((dv, d_model), lambda h: (h, 0)), rows, vec(d_model)],
        out_specs=rows,
        out_shape=jax.ShapeDtypeStruct((N_META, d_model), F32),
        scratch_shapes=[pltpu.VMEM((N_META, d_model), F32)],
        compiler_params=pltpu.CompilerParams(dimension_semantics=("arbitrary",)),
        name="l0_meta_attn_out",
    )(qkvg_meta, qkvg_meta, qkvg_meta, qkvg_meta, lq1, lk1, lq2, lk2, subg, w_out, x, post_g)


def _tail_kernel(a_ref, x_ref, hm_ref, w_out0_ref, post_g0_ref, pre_g_ref, w_in_u_ref,
                 w_in_g_ref, w_grp_ref, scale_ref, w_out_ref, post_g_ref, o_ref, u_ext_ref,
                 z_ref, *, tm, d, halo):
    i = pl.program_id(1)
    pre_g = pre_g_ref[...]

    @pl.when(i == 0)
    def _():
        hmn = _rms(hm_ref[...], pre_g).astype(BF16)
        u_ext_ref[0:halo, :] = _dot(hmn, w_in_u_ref[...])

    @pl.when(i > 0)
    def _():
        u_ext_ref[0:halo, :] = u_ext_ref[tm:tm + halo, :]

    h = x_ref[...] + _rms(_dot(a_ref[...], w_out0_ref[...]), post_g0_ref[...])
    o_ref[...] = h
    hn = _rms(h, pre_g).astype(BF16)
    u_ext_ref[halo:halo + tm, :] = _dot(hn, w_in_u_ref[...])

    gsz = d // len(POOL_WINDOWS)
    for gi, w in enumerate(POOL_WINDOWS):
        c0 = gi * gsz
        u = u_ext_ref[halo:halo + tm, c0:c0 + gsz]
        s = u
        for back in range(1, w):
            s = s + u_ext_ref[halo - back:halo - back + tm, c0:c0 + gsz]
        mixed = (s * (1.0 / w) - u).astype(BF16)
        mo = _dot(mixed, w_grp_ref[gi]) * scale_ref[:, c0:c0 + gsz]
        gate = _dot(hn, w_in_g_ref[:, c0:c0 + gsz])
        z_ref[:, c0:c0 + gsz] = (mo * _silu(gate)).astype(BF16)

    y = _dot(z_ref[...], w_out_ref[...])
    o_ref[...] = o_ref[...] + _rms(y, post_g_ref[...])


def _tail(a, x, h_meta, w_out0, post_g0, pre_g, w_in_u, w_in_g, w_grp, scale, w_out, post_g, *,
          batch, seq, tm):
    m, d = x.shape
    nt = seq // tm
    halo = N_META
    assert max(POOL_WINDOWS) <= halo and halo <= tm
    const2 = lambda b, i: (0, 0)
    row_tile = lambda b, i: (b * nt + i, 0)
    resident = lambda shape, imap: pl.BlockSpec(shape, imap, pipeline_mode=pl.Buffered(1))
    kern = functools.partial(_tail_kernel, tm=tm, d=d, halo=halo)
    return pl.pallas_call(
        kern,
        grid=(batch, nt),
        in_specs=[
            pl.BlockSpec((tm, d), row_tile),
            pl.BlockSpec((tm, d), row_tile),
            resident((N_META, d), const2),
            resident((d, d), const2),
            pl.BlockSpec((1, d), const2),
            pl.BlockSpec((1, d), const2),
            resident((d, d), const2),
            resident((d, d), const2),
            resident(w_grp.shape, lambda b, i: (0, 0, 0)),
            pl.BlockSpec((1, d), const2),
            resident((d, d), const2),
            pl.BlockSpec((1, d), const2),
        ],
        out_specs=pl.BlockSpec((tm, d), row_tile),
        out_shape=jax.ShapeDtypeStruct((m, d), F32),
        scratch_shapes=[pltpu.VMEM((halo + tm, d), F32), pltpu.VMEM((tm, d), BF16)],
        compiler_params=pltpu.CompilerParams(
            dimension_semantics=("arbitrary", "arbitrary"),
            vmem_limit_bytes=VMEM_LIMIT_BYTES),
        name="l0_out_l1_pool",
    )(a, x, h_meta, w_out0, post_g0, pre_g, w_in_u, w_in_g, w_grp, scale, w_out, post_g)


def _rope_tables(length, rope_dim):
    half = rope_dim // 2
    pos = np.arange(length, dtype=np.float32)
    inv = np.float32(ROPE_THETA) ** (-np.arange(0, rope_dim, 2, dtype=np.float32)
                                     / np.float32(rope_dim))
    ang = pos[:, None] * inv[None, :]
    cos, sin = np.cos(ang), np.sin(ang)
    zeros = lambda n: np.zeros((length, n), np.float32)
    cos_t = np.concatenate([cos, cos, np.ones((length, LANES - rope_dim), np.float32)], axis=1)
    sa_t = np.concatenate([zeros(half), sin, zeros(LANES - rope_dim)], axis=1)
    sb_t = np.concatenate([-sin, zeros(LANES - half)], axis=1)
    return cos_t, sa_t, sb_t


def kernel(x, meta_tokens, pre_norm_g, post_norm_g, attn_w_in, attn_w_out, attn_lambda_q1,
           attn_lambda_k1, attn_lambda_q2, attn_lambda_k2, attn_subln_g, pool_w_in,
           pool_w_group, pool_scale, pool_w_out):
    batch, seq, d = x.shape
    dv = d // DA_HEADS
    dh = dv // 2
    assert dh == LANES and seq % CHUNK == 0
    xf = x.reshape(batch * seq, d)
    meta = meta_tokens.astype(x.dtype)

    cos_t, sa_t, sb_t = _rope_tables(N_META + seq, dh // 4)
    pre_g0, pre_g1 = pre_norm_g[0:1], pre_norm_g[1:2]
    post_g0, post_g1 = post_norm_g[0:1], post_norm_g[1:2]
    q_scale = float(dh) ** -0.5 * math.log2(math.e)
    lam = [p[0:1] for p in (attn_lambda_q1, attn_lambda_k1, attn_lambda_q2, attn_lambda_k2)]
    subg = attn_subln_g[0:1]

    qkvg_meta, w_in0 = _norm_proj(
        meta, pre_g0, attn_w_in[0], cos_t[:N_META], sa_t[:N_META], sb_t[:N_META], tm=N_META,
        tn=PROJ_META_TN, rows_per_seq=N_META, q_scale=q_scale, emit_weight=True)
    qkvg = _norm_proj(xf, pre_g0, w_in0, cos_t[N_META:], sa_t[N_META:], sb_t[N_META:],
                      tm=PROJ_TM, tn=PROJ_TN, rows_per_seq=seq, q_scale=q_scale)
    side_weights = [(attn_w_out[0], 0), (pool_w_in[0], 0), (pool_w_in[0], 1), (pool_w_out[0], 0)]
    a, w_out0, w_in1_u, w_in1_g, w_out1 = _attention(
        qkvg, qkvg_meta, *lam, subg, side_weights, batch=batch, seq=seq, d_model=d)
    h1_meta = _meta_layer0(qkvg_meta, *lam, subg, w_out0, meta, post_g0, d_model=d)

    out = _tail(a, xf, h1_meta, w_out0, post_g0, pre_g1, w_in1_u, w_in1_g,
                pool_w_group[0].astype(BF16), pool_scale[0:1], w_out1, post_g1,
                batch=batch, seq=seq, tm=POOL_TM)
    return out.reshape(batch, seq, d)
```

```python
import functools
import math

import jax
import jax.numpy as jnp
import numpy as np
from jax import lax
from jax.experimental import pallas as pl
from jax.experimental.pallas import tpu as pltpu

F32 = jnp.float32
BF16 = jnp.bfloat16

N_META = 16
CHUNK = 64
RMS_EPS = 1e-6
DA_HEADS = 8
ROPE_THETA = 500000.0
POOL_WINDOWS = (2, 4, 8, 16)
LAMBDA_INIT_L0 = 0.8 - 0.6 * math.exp(-0.3 * 0)

LANES = 128
VMEM_LIMIT_BYTES = 56 * 1024 * 1024

PROJ_TM = 1024
PROJ_TN = 2048
PROJ_META_TN = 512
PROJ_SUB_ROWS = 256
PROJ_SUB_COLS = 256
ATTN_TQ = 2048
ATTN_TK = 256
ATTN_SUBTILES = 8
ATTN_KEY_UNROLL = 4
POOL_TM = 256

NEG_INF = float(jnp.finfo(jnp.float32).min)


def _rms(x, g):
    ms = jnp.mean(x * x, axis=-1, keepdims=True)
    return x * lax.rsqrt(ms + RMS_EPS) * g


def _silu(g):
    return g * (1.0 / (1.0 + jnp.exp(-g)))


def _dot(a, b):
    return jnp.dot(a, b, preferred_element_type=F32)


def _dot_nt(a, b):
    return lax.dot_general(a, b, (((1,), (1,)), ((), ())), preferred_element_type=F32)


def _norm_proj_kernel(x_ref, g_ref, w_ref, cos_ref, sa_ref, sb_ref, o_ref, *rest,
                      n_col_tiles_per_part, q_scale, sub_rows, sub_cols, emit_weight):
    wb_ref, hn_ref = rest if emit_weight else (None,) + rest
    j = pl.program_id(1)
    npt = n_col_tiles_per_part

    @pl.when(j == 0)
    def _():
        hn_ref[...] = _rms(x_ref[...], g_ref[...]).astype(BF16)

    tm, tn = o_ref.shape
    mc, nc = min(tm, sub_rows), min(tn, sub_cols)
    is_rope = j < 2 * npt
    is_gate = j >= 3 * npt
    scale = jnp.where(j < npt, q_scale, 1.0).astype(F32)
    half = 16
    for mi in range(tm // mc):
        rows = slice(mi * mc, (mi + 1) * mc)
        c = jnp.where(is_rope, cos_ref[rows, :] * scale, 1.0)
        sa = jnp.where(is_rope, sa_ref[rows, :] * scale, 0.0)
        sb = jnp.where(is_rope, sb_ref[rows, :] * scale, 0.0)
        for ni in range(tn // nc):
            w_sub = w_ref[:, ni * nc:(ni + 1) * nc]
            if emit_weight:
                w_sub = w_sub.astype(BF16)
                if mi == 0:
                    wb_ref[:, ni * nc:(ni + 1) * nc] = w_sub
            acc = _dot(hn_ref[rows, :], w_sub)
            for blk in range(nc // LANES):
                t = acc[:, blk * LANES:(blk + 1) * LANES]
                r = t * c + pltpu.roll(t, half, 1) * sa + pltpu.roll(t, LANES - half, 1) * sb
                f = jnp.where(is_gate, 1.0 / (1.0 + jnp.exp(-t)), 1.0)
                col = ni * nc + blk * LANES
                o_ref[rows, col:col + LANES] = (r * f).astype(o_ref.dtype)


def _norm_proj(x, g, w, cos_t, sa_t, sb_t, *, tm, tn, rows_per_seq, q_scale,
               emit_weight=False):
    m, d = x.shape
    n = w.shape[1]
    tiles_per_seq = rows_per_seq // tm
    assert not emit_weight or m == tm
    kern = functools.partial(_norm_proj_kernel, n_col_tiles_per_part=(n // 4) // tn,
                             q_scale=q_scale, sub_rows=PROJ_SUB_ROWS, sub_cols=PROJ_SUB_COLS,
                             emit_weight=emit_weight)
    tab_spec = pl.BlockSpec((tm, LANES), lambda i, j: (i % tiles_per_seq, 0))
    out_specs = pl.BlockSpec((tm, tn), lambda i, j: (i, j))
    out_shape = jax.ShapeDtypeStruct((m, n), BF16)
    if emit_weight:
        out_specs = [out_specs, pl.BlockSpec((d, tn), lambda i, j: (0, j))]
        out_shape = [out_shape, jax.ShapeDtypeStruct((d, n), BF16)]
    return pl.pallas_call(
        kern,
        grid=(m // tm, n // tn),
        in_specs=[
            pl.BlockSpec((tm, d), lambda i, j: (i, 0)),
            pl.BlockSpec((1, d), lambda i, j: (0, 0)),
            pl.BlockSpec((d, tn), lambda i, j: (0, j)),
            tab_spec, tab_spec, tab_spec,
        ],
        out_specs=out_specs,
        out_shape=out_shape,
        scratch_shapes=[pltpu.VMEM((tm, d), BF16)],
        compiler_params=pltpu.CompilerParams(
            dimension_semantics=("parallel", "arbitrary"),
            vmem_limit_bytes=VMEM_LIMIT_BYTES),
        name="l0_norm_proj",
    )(x, g, w, cos_t, sa_t, sb_t)


def _lambda_full(lq1_ref, lk1_ref, lq2_ref, lk2_ref):
    a = jnp.sum(lq1_ref[...] * lk1_ref[...], axis=-1, keepdims=True)
    b = jnp.sum(lq2_ref[...] * lk2_ref[...], axis=-1, keepdims=True)
    return jnp.exp(a) - jnp.exp(b) + LAMBDA_INIT_L0


def _diff_finalize(acc1, acc2, inv_l1, inv_l2, lam, subg, gate):
    o = acc1 * inv_l1 - acc2 * (lam * inv_l2)
    o = _rms(o, subg * (1.0 - LAMBDA_INIT_L0))
    return (o * gate.astype(F32)).astype(BF16)


def _attn_kernel(q_ref, k_ref, v_ref, gate_ref, km_ref, vm_ref, lq1_ref, lk1_ref, lq2_ref,
                 lk2_ref, subg_ref, w0_ref, w1_ref, w2_ref, w3_ref, o_ref, wb0_ref, wb1_ref,
                 wb2_ref, wb3_ref, m_ref, l_ref, acc_ref, *, tq, tk, dh, n_sub, unroll):
    ts = tq // n_sub
    dv = acc_ref.shape[2]

    for w_ref, wb_ref in ((w0_ref, wb0_ref), (w1_ref, wb1_ref), (w2_ref, wb2_ref),
                          (w3_ref, wb3_ref)):
        wb_ref[...] = w_ref[...].astype(wb_ref.dtype)

    for i in range(q_ref.shape[0] // tq):
        _attn_tile(i, q_ref, k_ref, v_ref, gate_ref, km_ref, vm_ref, lq1_ref, lk1_ref, lq2_ref,
                   lk2_ref, subg_ref, o_ref, m_ref, l_ref, acc_ref, tq=tq, tk=tk, dh=dh,
                   n_sub=n_sub, unroll=unroll)


def _attn_tile(i, q_ref, k_ref, v_ref, gate_ref, km_ref, vm_ref, lq1_ref, lk1_ref, lq2_ref,
               lk2_ref, subg_ref, o_ref, m_ref, l_ref, acc_ref, *, tq, tk, dh, n_sub, unroll):
    ts = tq // n_sub
    dv = acc_ref.shape[2]
    r0 = i * tq

    def scores(c, kb):
        q = q_ref[r0 + c * ts:r0 + (c + 1) * ts, :]
        return jnp.concatenate([_dot_nt(q[:, :dh], kb[:, :dh]), _dot_nt(q[:, dh:], kb[:, dh:])],
                               axis=0)

    def lane_sum(p):
        psum = p[:, 0:LANES]
        for blk in range(1, p.shape[1] // LANES):
            psum = psum + p[:, blk * LANES:(blk + 1) * LANES]
        return psum

    row = lax.broadcasted_iota(jnp.int32, (2 * ts, tk), 0) % ts
    col = lax.broadcasted_iota(jnp.int32, (2 * ts, tk), 1)
    visible = col // CHUNK <= row // CHUNK
    for c in range(n_sub):
        off = r0 + c * ts
        s = jnp.where(visible, scores(c, k_ref[pl.ds(off, tk), :]), NEG_INF)
        m0 = jnp.max(s, axis=-1, keepdims=True)
        p = jnp.exp2(s - m0)
        m_ref[c] = jnp.broadcast_to(m0, (2 * ts, LANES))
        l_ref[c] = lane_sum(p)
        acc_ref[c] = _dot(p.astype(BF16), v_ref[pl.ds(off, tk), :])

    def update(c, s, vb):
        m_prev = m_ref[c]
        m_new = jnp.maximum(m_prev, jnp.max(s, axis=-1, keepdims=True))
        alpha = jnp.exp2(m_prev - m_new)
        p = jnp.exp2(s - jnp.tile(m_new, (1, s.shape[1] // LANES)))
        l_ref[c] = alpha * l_ref[c] + lane_sum(p)
        m_ref[c] = m_new
        acc_ref[c] = (acc_ref[c] * jnp.tile(alpha, (1, dv // LANES))
                      + _dot(p.astype(BF16), vb))

    def full_blocks(jb, carry):
        for u in range(unroll):
            off = pl.multiple_of((jb * unroll + u) * tk, tk)
            kb = k_ref[pl.ds(off, tk), :]
            vb = v_ref[pl.ds(off, tk), :]
            for c in range(n_sub):
                update(c, scores(c, kb), vb)
        return carry

    if i > 0:
        lax.fori_loop(0, i * (tq // tk // unroll), full_blocks, 0)

    for d in range(n_sub - 1):
        off = r0 + d * tk
        kb = k_ref[pl.ds(off, tk), :]
        vb = v_ref[pl.ds(off, tk), :]
        for c in range(d + 1, n_sub):
            update(c, scores(c, kb), vb)

    lam = _lambda_full(lq1_ref, lk1_ref, lq2_ref, lk2_ref)
    km = km_ref[...]
    n_meta = km.shape[0]
    zfeat = jnp.zeros((n_meta, dh), km.dtype)
    zpad = jnp.zeros((LANES - n_meta, 2 * dh), km.dtype)
    km_bd = jnp.concatenate([jnp.concatenate([km[:, :dh], zfeat], axis=1), zpad,
                             jnp.concatenate([zfeat, km[:, dh:]], axis=1), zpad], axis=0)
    vm_pad = jnp.concatenate([vm_ref[...], jnp.zeros((LANES - n_meta, dv), km.dtype)], axis=0)
    real_key = lax.broadcasted_iota(jnp.int32, (2 * ts, LANES), 1) < n_meta
    for c in range(n_sub):
        r = _dot_nt(q_ref[r0 + c * ts:r0 + (c + 1) * ts, :], km_bd)
        s = jnp.where(real_key, jnp.concatenate([r[:, :LANES], r[:, LANES:]], axis=0), NEG_INF)
        m_prev = m_ref[c]
        m_new = jnp.maximum(m_prev, jnp.max(s, axis=-1, keepdims=True))
        alpha = jnp.exp2(m_prev - m_new)
        p = jnp.exp2(s - m_new)
        l = alpha * l_ref[c] + p
        acc = (acc_ref[c] * jnp.tile(alpha, (1, dv // LANES)) + _dot(p.astype(BF16), vm_pad))
        inv_l = 1.0 / jnp.sum(l, axis=-1, keepdims=True)
        o_ref[r0 + c * ts:r0 + (c + 1) * ts, :] = _diff_finalize(
            acc[:ts], acc[ts:], inv_l[:ts], inv_l[ts:], lam, subg_ref[...],
            gate_ref[r0 + c * ts:r0 + (c + 1) * ts, :])


def _attention(qkvg, qkvg_meta, lq1, lk1, lq2, lk2, subg, side_weights, *, batch, seq,
               d_model):
    dv = d_model // DA_HEADS
    dh = dv // 2
    tq, tk, n_sub = ATTN_TQ, ATTN_TK, ATTN_SUBTILES
    ts = tq // n_sub
    unroll = ATTN_KEY_UNROLL
    assert ts % CHUNK == 0 and tk % CHUNK == 0 and tq % (tk * unroll) == 0
    h_ = DA_HEADS
    n_steps = batch * h_
    conv_rows = d_model // n_steps
    assert len(side_weights) == 4 and seq % tq == 0 and conv_rows * n_steps == d_model
    kern = functools.partial(_attn_kernel, tq=tq, tk=tk, dh=dh, n_sub=n_sub, unroll=unroll)
    vec = lambda n: pl.BlockSpec((1, n), lambda b, h: (0, 0))

    def conv_spec(col_block):
        return pl.BlockSpec((conv_rows, d_model), lambda b, h: (b * h_ + h, col_block))

    side_arrays = [w for w, _ in side_weights]
    out_specs = [pl.BlockSpec((seq, dv), lambda b, h: (b, h))]
    out_specs += [conv_spec(0) for _ in side_weights]
    out_shape = [jax.ShapeDtypeStruct((batch * seq, d_model), BF16)]
    out_shape += [jax.ShapeDtypeStruct((d_model, d_model), BF16) for _ in side_weights]
    return pl.pallas_call(
        kern,
        grid=(batch, h_),
        in_specs=[
            pl.BlockSpec((seq, dv), lambda b, h: (b, h)),
            pl.BlockSpec((seq, dv), lambda b, h: (b, h_ + h)),
            pl.BlockSpec((seq, dv), lambda b, h: (b, 2 * h_ + h)),
            pl.BlockSpec((seq, dv), lambda b, h: (b, 3 * h_ + h)),
            pl.BlockSpec((N_META, dv), lambda b, h: (0, h_ + h)),
            pl.BlockSpec((N_META, dv), lambda b, h: (0, 2 * h_ + h)),
            vec(dh), vec(dh), vec(dh), vec(dh), vec(dv),
        ] + [conv_spec(cb) for _, cb in side_weights],
        out_specs=out_specs,
        out_shape=out_shape,
        scratch_shapes=[pltpu.VMEM((n_sub, 2 * ts, LANES), F32),
                        pltpu.VMEM((n_sub, 2 * ts, LANES), F32),
                        pltpu.VMEM((n_sub, 2 * ts, dv), F32)],
        compiler_params=pltpu.CompilerParams(
            dimension_semantics=("arbitrary", "arbitrary"),
            vmem_limit_bytes=VMEM_LIMIT_BYTES),
        name="l0_diff_attn",
    )(qkvg, qkvg, qkvg, qkvg, qkvg_meta, qkvg_meta, lq1, lk1, lq2, lk2, subg, *side_arrays)


def _meta_layer0_kernel(q_ref, k_ref, v_ref, gate_ref, lq1_ref, lk1_ref, lq2_ref, lk2_ref,
                        subg_ref, w_ref, x_ref, g_ref, o_ref, y_ref, *, dh):
    h = pl.program_id(0)
    q = q_ref[...]
    k = k_ref[...]
    v = v_ref[...]

    def attend(qm, km):
        s = _dot_nt(qm, km)
        p = jnp.exp2(s - jnp.max(s, axis=-1, keepdims=True))
        return _dot(p.astype(BF16), v), 1.0 / jnp.sum(p, axis=-1, keepdims=True)

    lam = _lambda_full(lq1_ref, lk1_ref, lq2_ref, lk2_ref)
    acc1, inv_l1 = attend(q[:, :dh], k[:, :dh])
    acc2, inv_l2 = attend(q[:, dh:], k[:, dh:])
    a = _diff_finalize(acc1, acc2, inv_l1, inv_l2, lam, subg_ref[...], gate_ref[...])
    part = _dot(a, w_ref[...])

    @pl.when(h == 0)
    def _():
        y_ref[...] = part

    @pl.when(h > 0)
    def _():
        y_ref[...] += part

    @pl.when(h == pl.num_programs(0) - 1)
    def _():
        o_ref[...] = x_ref[...] + _rms(y_ref[...], g_ref[...])


def _meta_layer0(qkvg_meta, lq1, lk1, lq2, lk2, subg, w_out, x, post_g, *, d_model):
    dv = d_model // DA_HEADS
    dh = dv // 2
    h_ = DA_HEADS
    vec = lambda n: pl.BlockSpec((1, n), lambda h: (0, 0))
    blk = lambda part: pl.BlockSpec((N_META, dv), lambda h: (0, part * h_ + h))
    rows = pl.BlockSpec((N_META, d_model), lambda h: (0, 0))
    return pl.pallas_call(
        functools.partial(_meta_layer0_kernel, dh=dh),
        grid=(h_,),
        in_specs=[blk(0), blk(1), blk(2), blk(3), vec(dh), vec(dh), vec(dh), vec(dh), vec(dv),
                  pl.BlockSpec((dv, d_model), lambda h: (h, 0)), rows, vec(d_model)],
        out_specs=rows,
        out_shape=jax.ShapeDtypeStruct((N_META, d_model), F32),
        scratch_shapes=[pltpu.VMEM((N_META, d_model), F32)],
        compiler_params=pltpu.CompilerParams(dimension_semantics=("arbitrary",)),
        name="l0_meta_attn_out",
    )(qkvg_meta, qkvg_meta, qkvg_meta, qkvg_meta, lq1, lk1, lq2, lk2, subg, w_out, x, post_g)


def _tail_kernel(a_ref, x_ref, hm_ref, w_out0_ref, post_g0_ref, pre_g_ref, w_in_u_ref,
                 w_in_g_ref, w_grp_ref, scale_ref, w_out_ref, post_g_ref, o_ref, u_ext_ref,
                 z_ref, *, tm, d, halo):
    i = pl.program_id(1)
    pre_g = pre_g_ref[...]

    @pl.when(i == 0)
    def _():
        hmn = _rms(hm_ref[...], pre_g).astype(BF16)
        u_ext_ref[0:halo, :] = _dot(hmn, w_in_u_ref[...])

    @pl.when(i > 0)
    def _():
        u_ext_ref[0:halo, :] = u_ext_ref[tm:tm + halo, :]

    h = x_ref[...] + _rms(_dot(a_ref[...], w_out0_ref[...]), post_g0_ref[...])
    o_ref[...] = h
    hn = _rms(h, pre_g).astype(BF16)
    u_ext_ref[halo:halo + tm, :] = _dot(hn, w_in_u_ref[...])

    gsz = d // len(POOL_WINDOWS)
    for gi, w in enumerate(POOL_WINDOWS):
        c0 = gi * gsz
        u = u_ext_ref[halo:halo + tm, c0:c0 + gsz]
        s = u
        for back in range(1, w):
            s = s + u_ext_ref[halo - back:halo - back + tm, c0:c0 + gsz]
        mixed = (s * (1.0 / w) - u).astype(BF16)
        mo = _dot(mixed, w_grp_ref[gi]) * scale_ref[:, c0:c0 + gsz]
        gate = _dot(hn, w_in_g_ref[:, c0:c0 + gsz])
        z_ref[:, c0:c0 + gsz] = (mo * _silu(gate)).astype(BF16)

    y = _dot(z_ref[...], w_out_ref[...])
    o_ref[...] = o_ref[...] + _rms(y, post_g_ref[...])


def _tail(a, x, h_meta, w_out0, post_g0, pre_g, w_in_u, w_in_g, w_grp, scale, w_out, post_g, *,
          batch, seq, tm):
    m, d = x.shape
    nt = seq // tm
    halo = N_META
    assert max(POOL_WINDOWS) <= halo and halo <= tm
    const2 = lambda b, i: (0, 0)
    row_tile = lambda b, i: (b * nt + i, 0)
    resident = lambda shape, imap: pl.BlockSpec(shape, imap, pipeline_mode=pl.Buffered(1))
    kern = functools.partial(_tail_kernel, tm=tm, d=d, halo=halo)
    return pl.pallas_call(
        kern,
        grid=(batch, nt),
        in_specs=[
            pl.BlockSpec((tm, d), row_tile),
            pl.BlockSpec((tm, d), row_tile),
            resident((N_META, d), const2),
            resident((d, d), const2),
            pl.BlockSpec((1, d), const2),
            pl.BlockSpec((1, d), const2),
            resident((d, d), const2),
            resident((d, d), const2),
            resident(w_grp.shape, lambda b, i: (0, 0, 0)),
            pl.BlockSpec((1, d), const2),
            resident((d, d), const2),
            pl.BlockSpec((1, d), const2),
        ],
        out_specs=pl.BlockSpec((tm, d), row_tile),
        out_shape=jax.ShapeDtypeStruct((m, d), F32),
        scratch_shapes=[pltpu.VMEM((halo + tm, d), F32), pltpu.VMEM((tm, d), BF16)],
        compiler_params=pltpu.CompilerParams(
            dimension_semantics=("arbitrary", "arbitrary"),
            vmem_limit_bytes=VMEM_LIMIT_BYTES),
        name="l0_out_l1_pool",
    )(a, x, h_meta, w_out0, post_g0, pre_g, w_in_u, w_in_g, w_grp, scale, w_out, post_g)


def _rope_tables(length, rope_dim):
    half = rope_dim // 2
    pos = np.arange(length, dtype=np.float32)
    inv = np.float32(ROPE_THETA) ** (-np.arange(0, rope_dim, 2, dtype=np.float32)
                                     / np.float32(rope_dim))
    ang = pos[:, None] * inv[None, :]
    cos, sin = np.cos(ang), np.sin(ang)
    zeros = lambda n: np.zeros((length, n), np.float32)
    cos_t = np.concatenate([cos, cos, np.ones((length, LANES - rope_dim), np.float32)], axis=1)
    sa_t = np.concatenate([zeros(half), sin, zeros(LANES - rope_dim)], axis=1)
    sb_t = np.concatenate([-sin, zeros(LANES - half)], axis=1)
    return cos_t, sa_t, sb_t


def kernel(x, meta_tokens, pre_norm_g, post_norm_g, attn_w_in, attn_w_out, attn_lambda_q1,
           attn_lambda_k1, attn_lambda_q2, attn_lambda_k2, attn_subln_g, pool_w_in,
           pool_w_group, pool_scale, pool_w_out):
    batch, seq, d = x.shape
    dv = d // DA_HEADS
    dh = dv // 2
    assert dh == LANES and seq % CHUNK == 0
    xf = x.reshape(batch * seq, d)
    meta = meta_tokens.astype(x.dtype)

    cos_t, sa_t, sb_t = _rope_tables(N_META + seq, dh // 4)
    pre_g0, pre_g1 = pre_norm_g[0:1], pre_norm_g[1:2]
    post_g0, post_g1 = post_norm_g[0:1], post_norm_g[1:2]
    q_scale = float(dh) ** -0.5 * math.log2(math.e)
    lam = [p[0:1] for p in (attn_lambda_q1, attn_lambda_k1, attn_lambda_q2, attn_lambda_k2)]
    subg = attn_subln_g[0:1]

    qkvg_meta, w_in0 = _norm_proj(
        meta, pre_g0, attn_w_in[0], cos_t[:N_META], sa_t[:N_META], sb_t[:N_META], tm=N_META,
        tn=PROJ_META_TN, rows_per_seq=N_META, q_scale=q_scale, emit_weight=True)
    qkvg = _norm_proj(xf, pre_g0, w_in0, cos_t[N_META:], sa_t[N_META:], sb_t[N_META:],
                      tm=PROJ_TM, tn=PROJ_TN, rows_per_seq=seq, q_scale=q_scale)
    side_weights = [(attn_w_out[0], 0), (pool_w_in[0], 0), (pool_w_in[0], 1), (pool_w_out[0], 0)]
    a, w_out0, w_in1_u, w_in1_g, w_out1 = _attention(
        qkvg, qkvg_meta, *lam, subg, side_weights, batch=batch, seq=seq, d_model=d)
    h1_meta = _meta_layer0(qkvg_meta, *lam, subg, w_out0, meta, post_g0, d_model=d)

    out = _tail(a, xf, h1_meta, w_out0, post_g0, pre_g1, w_in1_u, w_in1_g,
                pool_w_group[0].astype(BF16), pool_scale[0:1], w_out1, post_g1,
                batch=batch, seq=seq, tm=POOL_TM)
    return out.reshape(batch, seq, d)
```

```python
import functools
import math

import jax
import jax.numpy as jnp
import numpy as np
from jax import lax
from jax.experimental import pallas as pl
from jax.experimental.pallas import tpu as pltpu

F32 = jnp.float32
BF16 = jnp.bfloat16

N_META = 16
CHUNK = 64
RMS_EPS = 1e-6
DA_HEADS = 8
ROPE_THETA = 500000.0
POOL_WINDOWS = (2, 4, 8, 16)
LAMBDA_INIT_L0 = 0.8 - 0.6 * math.exp(-0.3 * 0)

LANES = 128
VMEM_LIMIT_BYTES = 56 * 1024 * 1024

PROJ_TM = 1024
PROJ_TN = 2048
PROJ_META_TN = 512
PROJ_SUB_ROWS = 256
PROJ_SUB_COLS = 256
ATTN_TQ = 2048
ATTN_TK = 256
ATTN_SUBTILES = 8
ATTN_KEY_UNROLL = 4
POOL_TM = 256

NEG_INF = float(jnp.finfo(jnp.float32).min)


def _rms(x, g):
    ms = jnp.mean(x * x, axis=-1, keepdims=True)
    return x * lax.rsqrt(ms + RMS_EPS) * g


def _silu(g):
    return g * (1.0 / (1.0 + jnp.exp(-g)))


def _dot(a, b):
    return jnp.dot(a, b, preferred_element_type=F32)


def _dot_nt(a, b):
    return lax.dot_general(a, b, (((1,), (1,)), ((), ())), preferred_element_type=F32)


def _norm_proj_kernel(x_ref, g_ref, w_ref, cos_ref, sa_ref, sb_ref, o_ref, *rest,
                      n_col_tiles_per_part, q_scale, sub_rows, sub_cols, emit_weight):
    wb_ref, hn_ref = rest if emit_weight else (None,) + rest
    j = pl.program_id(1)
    npt = n_col_tiles_per_part

    tm, tn = o_ref.shape
    mc, nc = min(tm, sub_rows), min(tn, sub_cols)
    is_rope = j < 2 * npt
    is_gate = j >= 3 * npt
    scale = jnp.where(j < npt, q_scale, 1.0).astype(F32)
    half = 16

    def tile(first):
        for mi in range(tm // mc):
            rows = slice(mi * mc, (mi + 1) * mc)
            if first:
                hn = _rms(x_ref[rows, :], g_ref[...]).astype(BF16)
                hn_ref[rows, :] = hn
            else:
                hn = hn_ref[rows, :]
            c = jnp.where(is_rope, cos_ref[rows, :] * scale, 1.0)
            sa = jnp.where(is_rope, sa_ref[rows, :] * scale, 0.0)
            sb = jnp.where(is_rope, sb_ref[rows, :] * scale, 0.0)
            for ni in range(tn // nc):
                w_sub = w_ref[:, ni * nc:(ni + 1) * nc]
                if emit_weight:
                    w_sub = w_sub.astype(BF16)
                    if mi == 0:
                        wb_ref[:, ni * nc:(ni + 1) * nc] = w_sub
                acc = _dot(hn, w_sub)
                for blk in range(nc // LANES):
                    t = acc[:, blk * LANES:(blk + 1) * LANES]
                    r = (t * c + pltpu.roll(t, half, 1) * sa
                         + pltpu.roll(t, LANES - half, 1) * sb)
                    f = jnp.where(is_gate, 1.0 / (1.0 + jnp.exp(-t)), 1.0)
                    col = ni * nc + blk * LANES
                    o_ref[rows, col:col + LANES] = (r * f).astype(o_ref.dtype)

    pl.when(j == 0)(functools.partial(tile, True))
    pl.when(j > 0)(functools.partial(tile, False))


def _norm_proj(x, g, w, cos_t, sa_t, sb_t, *, tm, tn, rows_per_seq, q_scale,
               emit_weight=False):
    m, d = x.shape
    n = w.shape[1]
    tiles_per_seq = rows_per_seq // tm
    assert not emit_weight or m == tm
    kern = functools.partial(_norm_proj_kernel, n_col_tiles_per_part=(n // 4) // tn,
                             q_scale=q_scale, sub_rows=PROJ_SUB_ROWS, sub_cols=PROJ_SUB_COLS,
                             emit_weight=emit_weight)
    tab_spec = pl.BlockSpec((tm, LANES), lambda i, j: (i % tiles_per_seq, 0))
    out_specs = pl.BlockSpec((tm, tn), lambda i, j: (i, j))
    out_shape = jax.ShapeDtypeStruct((m, n), BF16)
    if emit_weight:
        out_specs = [out_specs, pl.BlockSpec((d, tn), lambda i, j: (0, j))]
        out_shape = [out_shape, jax.ShapeDtypeStruct((d, n), BF16)]
    return pl.pallas_call(
        kern,
        grid=(m // tm, n // tn),
        in_specs=[
            pl.BlockSpec((tm, d), lambda i, j: (i, 0)),
            pl.BlockSpec((1, d), lambda i, j: (0, 0)),
            pl.BlockSpec((d, tn), lambda i, j: (0, j)),
            tab_spec, tab_spec, tab_spec,
        ],
        out_specs=out_specs,
        out_shape=out_shape,
        scratch_shapes=[pltpu.VMEM((tm, d), BF16)],
        compiler_params=pltpu.CompilerParams(
            dimension_semantics=("parallel", "arbitrary"),
            vmem_limit_bytes=VMEM_LIMIT_BYTES),
        name="l0_norm_proj",
    )(x, g, w, cos_t, sa_t, sb_t)


def _lambda_full(lq1_ref, lk1_ref, lq2_ref, lk2_ref):
    a = jnp.sum(lq1_ref[...] * lk1_ref[...], axis=-1, keepdims=True)
    b = jnp.sum(lq2_ref[...] * lk2_ref[...], axis=-1, keepdims=True)
    return jnp.exp(a) - jnp.exp(b) + LAMBDA_INIT_L0


def _diff_finalize(acc1, acc2, inv_l1, inv_l2, lam, subg, gate):
    o = acc1 * inv_l1 - acc2 * (lam * inv_l2)
    o = _rms(o, subg * (1.0 - LAMBDA_INIT_L0))
    return (o * gate.astype(F32)).astype(BF16)


def _attn_kernel(q_ref, k_ref, v_ref, gate_ref, km_ref, vm_ref, lq1_ref, lk1_ref, lq2_ref,
                 lk2_ref, subg_ref, w0_ref, w1_ref, w2_ref, w3_ref, o_ref, wb0_ref, wb1_ref,
                 wb2_ref, wb3_ref, m_ref, l_ref, acc_ref, *, tq, tk, dh, n_sub, unroll,
                 conv_steps):
    i = pl.program_id(2)
    ts = tq // n_sub
    dv = acc_ref.shape[2]

    step = (pl.program_id(0) * pl.num_programs(1) + pl.program_id(1)) * pl.num_programs(2) + i
    for k, (w_ref, wb_ref) in enumerate(((w0_ref, wb0_ref), (w1_ref, wb1_ref),
                                         (w2_ref, wb2_ref), (w3_ref, wb3_ref))):
        @pl.when(jnp.logical_and(step >= k * conv_steps, step < (k + 1) * conv_steps))
        def _(w_ref=w_ref, wb_ref=wb_ref):
            wb_ref[...] = w_ref[...].astype(wb_ref.dtype)

    def scores(c, kb):
        q = q_ref[c * ts:(c + 1) * ts, :]
        return jnp.concatenate([_dot_nt(q[:, :dh], kb[:, :dh]), _dot_nt(q[:, dh:], kb[:, dh:])],
                               axis=0)

    def lane_sum(p):
        psum = p[:, 0:LANES]
        for blk in range(1, p.shape[1] // LANES):
            psum = psum + p[:, blk * LANES:(blk + 1) * LANES]
        return psum

    row = lax.broadcasted_iota(jnp.int32, (2 * ts, tk), 0) % ts
    col = lax.broadcasted_iota(jnp.int32, (2 * ts, tk), 1)
    visible = col // CHUNK <= row // CHUNK
    for c in range(n_sub):
        off = pl.multiple_of(i * tq + c * ts, ts)
        s = jnp.where(visible, scores(c, k_ref[pl.ds(off, tk), :]), NEG_INF)
        m0 = jnp.max(s, axis=-1, keepdims=True)
        p = jnp.exp2(s - m0)
        m_ref[c] = jnp.broadcast_to(m0, (2 * ts, LANES))
        l_ref[c] = lane_sum(p)
        acc_ref[c] = _dot(p.astype(BF16), v_ref[pl.ds(off, tk), :])

    def update(c, s, vb):
        m_prev = m_ref[c]
        m_new = jnp.maximum(m_prev, jnp.max(s, axis=-1, keepdims=True))
        alpha = jnp.exp2(m_prev - m_new)
        p = jnp.exp2(s - jnp.tile(m_new, (1, s.shape[1] // LANES)))
        l_ref[c] = alpha * l_ref[c] + lane_sum(p)
        m_ref[c] = m_new
        acc_ref[c] = (acc_ref[c] * jnp.tile(alpha, (1, dv // LANES))
                      + _dot(p.astype(BF16), vb))

    def full_blocks(jb, carry):
        for u in range(unroll):
            off = pl.multiple_of((jb * unroll + u) * tk, tk)
            kb = k_ref[pl.ds(off, tk), :]
            vb = v_ref[pl.ds(off, tk), :]
            for c in range(n_sub):
                update(c, scores(c, kb), vb)
        return carry

    lax.fori_loop(0, i * (tq // tk // unroll), full_blocks, 0)

    for d in range(n_sub - 1):
        off = pl.multiple_of(i * tq + d * tk, tk)
        kb = k_ref[pl.ds(off, tk), :]
        vb = v_ref[pl.ds(off, tk), :]
        for c in range(d + 1, n_sub):
            update(c, scores(c, kb), vb)

    lam = _lambda_full(lq1_ref, lk1_ref, lq2_ref, lk2_ref)
    km = km_ref[...]
    n_meta = km.shape[0]
    zfeat = jnp.zeros((n_meta, dh), km.dtype)
    zpad = jnp.zeros((LANES - n_meta, 2 * dh), km.dtype)
    km_bd = jnp.concatenate([jnp.concatenate([km[:, :dh], zfeat], axis=1), zpad,
                             jnp.concatenate([zfeat, km[:, dh:]], axis=1), zpad], axis=0)
    vm_pad = jnp.concatenate([vm_ref[...], jnp.zeros((LANES - n_meta, dv), km.dtype)], axis=0)
    real_key = lax.broadcasted_iota(jnp.int32, (2 * ts, LANES), 1) < n_meta
    for c in range(n_sub):
        r = _dot_nt(q_ref[c * ts:(c + 1) * ts, :], km_bd)
        s = jnp.where(real_key, jnp.concatenate([r[:, :LANES], r[:, LANES:]], axis=0), NEG_INF)
        m_prev = m_ref[c]
        m_new = jnp.maximum(m_prev, jnp.max(s, axis=-1, keepdims=True))
        alpha = jnp.exp2(m_prev - m_new)
        p = jnp.exp2(s - m_new)
        l = alpha * l_ref[c] + p
        acc = (acc_ref[c] * jnp.tile(alpha, (1, dv // LANES)) + _dot(p.astype(BF16), vm_pad))
        inv_l = 1.0 / jnp.sum(l, axis=-1, keepdims=True)
        o_ref[c * ts:(c + 1) * ts, :] = _diff_finalize(
            acc[:ts], acc[ts:], inv_l[:ts], inv_l[ts:], lam, subg_ref[...],
            gate_ref[c * ts:(c + 1) * ts, :])


def _attention(qkvg, qkvg_meta, lq1, lk1, lq2, lk2, subg, side_weights, *, batch, seq,
               d_model):
    dv = d_model // DA_HEADS
    dh = dv // 2
    tq, tk, n_sub = ATTN_TQ, ATTN_TK, ATTN_SUBTILES
    ts = tq // n_sub
    unroll = ATTN_KEY_UNROLL
    assert ts % CHUNK == 0 and tk % CHUNK == 0 and tq % (tk * unroll) == 0
    nq = seq // tq
    h_ = DA_HEADS
    n_steps = batch * h_ * nq
    conv_steps = n_steps // len(side_weights)
    conv_rows = d_model // conv_steps
    assert len(side_weights) == 4 and conv_steps * 4 == n_steps and conv_rows * conv_steps == d_model
    kern = functools.partial(_attn_kernel, tq=tq, tk=tk, dh=dh, n_sub=n_sub, unroll=unroll,
                             conv_steps=conv_steps)
    vec = lambda n: pl.BlockSpec((1, n), lambda b, h, i: (0, 0))

    def conv_spec(k, col_block):
        def imap(b, h, i):
            step = (b * h_ + h) * nq + i
            return (jnp.clip(step - k * conv_steps, 0, conv_steps - 1), col_block)
        return pl.BlockSpec((conv_rows, d_model), imap)

    side_arrays = [w for w, _ in side_weights]
    out_specs = [pl.BlockSpec((tq, dv), lambda b, h, i: (b * nq + i, h))]
    out_specs += [conv_spec(k, 0) for k in range(len(side_weights))]
    out_shape = [jax.ShapeDtypeStruct((batch * seq, d_model), BF16)]
    out_shape += [jax.ShapeDtypeStruct((d_model, d_model), BF16) for _ in side_weights]
    return pl.pallas_call(
        kern,
        grid=(batch, h_, nq),
        in_specs=[
            pl.BlockSpec((tq, dv), lambda b, h, i: (b * nq + i, h)),
            pl.BlockSpec((seq, dv), lambda b, h, i: (b, h_ + h)),
            pl.BlockSpec((seq, dv), lambda b, h, i: (b, 2 * h_ + h)),
            pl.BlockSpec((tq, dv), lambda b, h, i: (b * nq + i, 3 * h_ + h)),
            pl.BlockSpec((N_META, dv), lambda b, h, i: (0, h_ + h)),
            pl.BlockSpec((N_META, dv), lambda b, h, i: (0, 2 * h_ + h)),
            vec(dh), vec(dh), vec(dh), vec(dh), vec(dv),
        ] + [conv_spec(k, cb) for k, (_, cb) in enumerate(side_weights)],
        out_specs=out_specs,
        out_shape=out_shape,
        scratch_shapes=[pltpu.VMEM((n_sub, 2 * ts, LANES), F32),
                        pltpu.VMEM((n_sub, 2 * ts, LANES), F32),
                        pltpu.VMEM((n_sub, 2 * ts, dv), F32)],
        compiler_params=pltpu.CompilerParams(
            dimension_semantics=("arbitrary", "arbitrary", "arbitrary"),
            vmem_limit_bytes=VMEM_LIMIT_BYTES),
        name="l0_diff_attn",
    )(qkvg, qkvg, qkvg, qkvg, qkvg_meta, qkvg_meta, lq1, lk1, lq2, lk2, subg, *side_arrays)


def _meta_layer0_kernel(q_ref, k_ref, v_ref, gate_ref, lq1_ref, lk1_ref, lq2_ref, lk2_ref,
                        subg_ref, w_ref, x_ref, g_ref, o_ref, y_ref, *, dh):
    h = pl.program_id(0)
    q = q_ref[...]
    k = k_ref[...]
    v = v_ref[...]

    def attend(qm, km):
        s = _dot_nt(qm, km)
        p = jnp.exp2(s - jnp.max(s, axis=-1, keepdims=True))
        return _dot(p.astype(BF16), v), 1.0 / jnp.sum(p, axis=-1, keepdims=True)

    lam = _lambda_full(lq1_ref, lk1_ref, lq2_ref, lk2_ref)
    acc1, inv_l1 = attend(q[:, :dh], k[:, :dh])
    acc2, inv_l2 = attend(q[:, dh:], k[:, dh:])
    a = _diff_finalize(acc1, acc2, inv_l1, inv_l2, lam, subg_ref[...], gate_ref[...])
    part = _dot(a, w_ref[...])

    @pl.when(h == 0)
    def _():
        y_ref[...] = part

    @pl.when(h > 0)
    def _():
        y_ref[...] += part

    @pl.when(h == pl.num_programs(0) - 1)
    def _():
        o_ref[...] = x_ref[...] + _rms(y_ref[...], g_ref[...])


def _meta_layer0(qkvg_meta, lq1, lk1, lq2, lk2, subg, w_out, x, post_g, *, d_model):
    dv = d_model // DA_HEADS
    dh = dv // 2
    h_ = DA_HEADS
    vec = lambda n: pl.BlockSpec((1, n), lambda h: (0, 0))
    blk = lambda part: pl.BlockSpec((N_META, dv), lambda h: (0, part * h_ + h))
    rows = pl.BlockSpec((N_META, d_model), lambda h: (0, 0))
    return pl.pallas_call(
        functools.partial(_meta_layer0_kernel, dh=dh),
        grid=(h_,),
        in_specs=[blk(0), blk(1), blk(2), blk(3), vec(dh), vec(dh), vec(dh), vec(dh), vec(dv),
                  pl.BlockSpec((dv, d_model), lambda h: (h, 0)), rows, vec(d_model)],
        out_specs=rows,
        out_shape=jax.ShapeDtypeStruct((N_META, d_model), F32),
        scratch_shapes=[pltpu.VMEM((N_META, d_model), F32)],
        compiler_params=pltpu.CompilerParams(dimension_semantics=("arbitrary",)),
        name="l0_meta_attn_out",
    )(qkvg_meta, qkvg_meta, qkvg_meta, qkvg_meta, lq1, lk1, lq2, lk2, subg, w_out, x, post_g)


def _tail_kernel(a_ref, x_ref, hm_ref, w_out0_ref, post_g0_ref, pre_g_ref, w_in_u_ref,
                 w_in_g_ref, w_grp_ref, scale_ref, w_out_ref, post_g_ref, o_ref, u_ext_ref,
                 z_ref, *, tm, d, halo):
    i = pl.program_id(1)
    pre_g = pre_g_ref[...]

    @pl.when(i == 0)
    def _():
        hmn = _rms(hm_ref[...], pre_g).astype(BF16)
        u_ext_ref[0:halo, :] = _dot(hmn, w_in_u_ref[...])

    @pl.when(i > 0)
    def _():
        u_ext_ref[0:halo, :] = u_ext_ref[tm:tm + halo, :]

    h = x_ref[...] + _rms(_dot(a_ref[...], w_out0_ref[...]), post_g0_ref[...])
    o_ref[...] = h
    hn = _rms(h, pre_g).astype(BF16)
    u_ext_ref[halo:halo + tm, :] = _dot(hn, w_in_u_ref[...])

    gsz = d // len(POOL_WINDOWS)
    for gi, w in enumerate(POOL_WINDOWS):
        c0 = gi * gsz
        u = u_ext_ref[halo:halo + tm, c0:c0 + gsz]
        s = u
        for back in range(1, w):
            s = s + u_ext_ref[halo - back:halo - back + tm, c0:c0 + gsz]
        mixed = (s * (1.0 / w) - u).astype(BF16)
        mo = _dot(mixed, w_grp_ref[gi]) * scale_ref[:, c0:c0 + gsz]
        gate = _dot(hn, w_in_g_ref[:, c0:c0 + gsz])
        z_ref[:, c0:c0 + gsz] = (mo * _silu(gate)).astype(BF16)

    y = _dot(z_ref[...], w_out_ref[...])
    o_ref[...] = o_ref[...] + _rms(y, post_g_ref[...])


def _tail(a, x, h_meta, w_out0, post_g0, pre_g, w_in_u, w_in_g, w_grp, scale, w_out, post_g, *,
          batch, seq, tm):
    m, d = x.shape
    nt = seq // tm
    halo = N_META
    assert max(POOL_WINDOWS) <= halo and halo <= tm
    const2 = lambda b, i: (0, 0)
    row_tile = lambda b, i: (b * nt + i, 0)
    resident = lambda shape, imap: pl.BlockSpec(shape, imap, pipeline_mode=pl.Buffered(1))
    kern = functools.partial(_tail_kernel, tm=tm, d=d, halo=halo)
    return pl.pallas_call(
        kern,
        grid=(batch, nt),
        in_specs=[
            pl.BlockSpec((tm, d), row_tile),
            pl.BlockSpec((tm, d), row_tile),
            resident((N_META, d), const2),
            resident((d, d), const2),
            pl.BlockSpec((1, d), const2),
            pl.BlockSpec((1, d), const2),
            resident((d, d), const2),
            resident((d, d), const2),
            resident(w_grp.shape, lambda b, i: (0, 0, 0)),
            pl.BlockSpec((1, d), const2),
            resident((d, d), const2),
            pl.BlockSpec((1, d), const2),
        ],
        out_specs=pl.BlockSpec((tm, d), row_tile),
        out_shape=jax.ShapeDtypeStruct((m, d), F32),
        scratch_shapes=[pltpu.VMEM((halo + tm, d), F32), pltpu.VMEM((tm, d), BF16)],
        compiler_params=pltpu.CompilerParams(
            dimension_semantics=("arbitrary", "arbitrary"),
            vmem_limit_bytes=VMEM_LIMIT_BYTES),
        name="l0_out_l1_pool",
    )(a, x, h_meta, w_out0, post_g0, pre_g, w_in_u, w_in_g, w_grp, scale, w_out, post_g)


def _rope_tables(length, rope_dim):
    half = rope_dim // 2
    pos = np.arange(length, dtype=np.float32)
    inv = np.float32(ROPE_THETA) ** (-np.arange(0, rope_dim, 2, dtype=np.float32)
                                     / np.float32(rope_dim))
    ang = pos[:, None] * inv[None, :]
    cos, sin = np.cos(ang), np.sin(ang)
    zeros = lambda n: np.zeros((length, n), np.float32)
    cos_t = np.concatenate([cos, cos, np.ones((length, LANES - rope_dim), np.float32)], axis=1)
    sa_t = np.concatenate([zeros(half), sin, zeros(LANES - rope_dim)], axis=1)
    sb_t = np.concatenate([-sin, zeros(LANES - half)], axis=1)
    return cos_t, sa_t, sb_t


def kernel(x, meta_tokens, pre_norm_g, post_norm_g, attn_w_in, attn_w_out, attn_lambda_q1,
           attn_lambda_k1, attn_lambda_q2, attn_lambda_k2, attn_subln_g, pool_w_in,
           pool_w_group, pool_scale, pool_w_out):
    batch, seq, d = x.shape
    dv = d // DA_HEADS
    dh = dv // 2
    assert dh == LANES and seq % CHUNK == 0
    xf = x.reshape(batch * seq, d)
    meta = meta_tokens.astype(x.dtype)

    cos_t, sa_t, sb_t = _rope_tables(N_META + seq, dh // 4)
    pre_g0, pre_g1 = pre_norm_g[0:1], pre_norm_g[1:2]
    post_g0, post_g1 = post_norm_g[0:1], post_norm_g[1:2]
    q_scale = float(dh) ** -0.5 * math.log2(math.e)
    lam = [p[0:1] for p in (attn_lambda_q1, attn_lambda_k1, attn_lambda_q2, attn_lambda_k2)]
    subg = attn_subln_g[0:1]

    qkvg_meta, w_in0 = _norm_proj(
        meta, pre_g0, attn_w_in[0], cos_t[:N_META], sa_t[:N_META], sb_t[:N_META], tm=N_META,
        tn=PROJ_META_TN, rows_per_seq=N_META, q_scale=q_scale, emit_weight=True)
    qkvg = _norm_proj(xf, pre_g0, w_in0, cos_t[N_META:], sa_t[N_META:], sb_t[N_META:],
                      tm=PROJ_TM, tn=PROJ_TN, rows_per_seq=seq, q_scale=q_scale)
    side_weights = [(attn_w_out[0], 0), (pool_w_in[0], 0), (pool_w_in[0], 1), (pool_w_out[0], 0)]
    a, w_out0, w_in1_u, w_in1_g, w_out1 = _attention(
        qkvg, qkvg_meta, *lam, subg, side_weights, batch=batch, seq=seq, d_model=d)
    h1_meta = _meta_layer0(qkvg_meta, *lam, subg, w_out0, meta, post_g0, d_model=d)

    out = _tail(a, xf, h1_meta, w_out0, post_g0, pre_g1, w_in1_u, w_in1_g,
                pool_w_group[0].astype(BF16), pool_scale[0:1], w_out1, post_g1,
                batch=batch, seq=seq, tm=POOL_TM)
    return out.reshape(batch, seq, d)
```

```python
import functools
import math

import jax
import jax.numpy as jnp
import numpy as np
from jax import lax
from jax.experimental import pallas as pl
from jax.experimental.pallas import tpu as pltpu

F32 = jnp.float32
BF16 = jnp.bfloat16

N_META = 16
CHUNK = 64
RMS_EPS = 1e-6
DA_HEADS = 8
ROPE_THETA = 500000.0
POOL_WINDOWS = (2, 4, 8, 16)
LAMBDA_INIT_L0 = 0.8 - 0.6 * math.exp(-0.3 * 0)

LANES = 128
VMEM_LIMIT_BYTES = 56 * 1024 * 1024

PROJ_TM = 1024
PROJ_TN = 2048
PROJ_META_TN = 1024
PROJ_SUB_ROWS = 256
PROJ_SUB_COLS = 256
ATTN_TQ = 2048
ATTN_TK = 256
ATTN_SUBTILES = 8
ATTN_KEY_UNROLL = 4
POOL_TM = 256

NEG_INF = float(jnp.finfo(jnp.float32).min)


def _rms(x, g):
    ms = jnp.mean(x * x, axis=-1, keepdims=True)
    return x * lax.rsqrt(ms + RMS_EPS) * g


def _silu(g):
    return g * (1.0 / (1.0 + jnp.exp(-g)))


def _dot(a, b):
    return jnp.dot(a, b, preferred_element_type=F32)


def _dot_nt(a, b):
    return lax.dot_general(a, b, (((1,), (1,)), ((), ())), preferred_element_type=F32)


def _norm_proj_kernel(x_ref, g_ref, w_ref, cos_ref, sa_ref, sb_ref, o_ref, *rest,
                      n_col_tiles_per_part, q_scale, sub_rows, sub_cols, emit_weight):
    wb_ref, hn_ref = rest if emit_weight else (None,) + rest
    j = pl.program_id(1)
    npt = n_col_tiles_per_part

    tm, tn = o_ref.shape
    mc, nc = min(tm, sub_rows), min(tn, sub_cols)
    is_rope = j < 2 * npt
    is_gate = j >= 3 * npt
    scale = jnp.where(j < npt, q_scale, 1.0).astype(F32)
    half = 16

    def tile(first):
        for mi in range(tm // mc):
            rows = slice(mi * mc, (mi + 1) * mc)
            if first:
                hn = _rms(x_ref[rows, :], g_ref[...]).astype(BF16)
                hn_ref[rows, :] = hn
            else:
                hn = hn_ref[rows, :]
            c = jnp.where(is_rope, cos_ref[rows, :] * scale, 1.0)
            sa = jnp.where(is_rope, sa_ref[rows, :] * scale, 0.0)
            sb = jnp.where(is_rope, sb_ref[rows, :] * scale, 0.0)
            for ni in range(tn // nc):
                w_sub = w_ref[:, ni * nc:(ni + 1) * nc]
                if emit_weight:
                    w_sub = w_sub.astype(BF16)
                    if mi == 0:
                        wb_ref[:, ni * nc:(ni + 1) * nc] = w_sub
                acc = _dot(hn, w_sub)
                for blk in range(nc // LANES):
                    t = acc[:, blk * LANES:(blk + 1) * LANES]
                    r = (t * c + pltpu.roll(t, half, 1) * sa
                         + pltpu.roll(t, LANES - half, 1) * sb)
                    f = jnp.where(is_gate, 1.0 / (1.0 + jnp.exp(-t)), 1.0)
                    col = ni * nc + blk * LANES
                    o_ref[rows, col:col + LANES] = (r * f).astype(o_ref.dtype)

    pl.when(j == 0)(functools.partial(tile, True))
    pl.when(j > 0)(functools.partial(tile, False))


def _norm_proj(x, g, w, cos_t, sa_t, sb_t, *, tm, tn, rows_per_seq, q_scale,
               emit_weight=False):
    m, d = x.shape
    n = w.shape[1]
    tiles_per_seq = rows_per_seq // tm
    assert not emit_weight or m == tm
    kern = functools.partial(_norm_proj_kernel, n_col_tiles_per_part=(n // 4) // tn,
                             q_scale=q_scale, sub_rows=PROJ_SUB_ROWS, sub_cols=PROJ_SUB_COLS,
                             emit_weight=emit_weight)
    tab_spec = pl.BlockSpec((tm, LANES), lambda i, j: (i % tiles_per_seq, 0))
    out_specs = pl.BlockSpec((tm, tn), lambda i, j: (i, j))
    out_shape = jax.ShapeDtypeStruct((m, n), BF16)
    if emit_weight:
        out_specs = [out_specs, pl.BlockSpec((d, tn), lambda i, j: (0, j))]
        out_shape = [out_shape, jax.ShapeDtypeStruct((d, n), BF16)]
    return pl.pallas_call(
        kern,
        grid=(m // tm, n // tn),
        in_specs=[
            pl.BlockSpec((tm, d), lambda i, j: (i, 0)),
            pl.BlockSpec((1, d), lambda i, j: (0, 0)),
            pl.BlockSpec((d, tn), lambda i, j: (0, j)),
            tab_spec, tab_spec, tab_spec,
        ],
        out_specs=out_specs,
        out_shape=out_shape,
        scratch_shapes=[pltpu.VMEM((tm, d), BF16)],
        compiler_params=pltpu.CompilerParams(
            dimension_semantics=("parallel", "arbitrary"),
            vmem_limit_bytes=VMEM_LIMIT_BYTES),
        name="l0_norm_proj",
    )(x, g, w, cos_t, sa_t, sb_t)


def _lambda_full(lq1_ref, lk1_ref, lq2_ref, lk2_ref):
    a = jnp.sum(lq1_ref[...] * lk1_ref[...], axis=-1, keepdims=True)
    b = jnp.sum(lq2_ref[...] * lk2_ref[...], axis=-1, keepdims=True)
    return jnp.exp(a) - jnp.exp(b) + LAMBDA_INIT_L0


def _diff_finalize(acc1, acc2, inv_l1, inv_l2, lam, subg, gate):
    o = acc1 * inv_l1 - acc2 * (lam * inv_l2)
    o = _rms(o, subg * (1.0 - LAMBDA_INIT_L0))
    return (o * gate.astype(F32)).astype(BF16)


def _attn_kernel(q_ref, k_ref, v_ref, gate_ref, km_ref, vm_ref, lq1_ref, lk1_ref, lq2_ref,
                 lk2_ref, subg_ref, w0_ref, w1_ref, w2_ref, w3_ref, o_ref, wb0_ref, wb1_ref,
                 wb2_ref, wb3_ref, m_ref, l_ref, acc_ref, *, tq, tk, dh, n_sub, unroll,
                 conv_steps):
    i = pl.program_id(2)
    ts = tq // n_sub
    dv = acc_ref.shape[2]

    step = (pl.program_id(0) * pl.num_programs(1) + pl.program_id(1)) * pl.num_programs(2) + i
    for k, (w_ref, wb_ref) in enumerate(((w0_ref, wb0_ref), (w1_ref, wb1_ref),
                                         (w2_ref, wb2_ref), (w3_ref, wb3_ref))):
        @pl.when(jnp.logical_and(step >= k * conv_steps, step < (k + 1) * conv_steps))
        def _(w_ref=w_ref, wb_ref=wb_ref):
            wb_ref[...] = w_ref[...].astype(wb_ref.dtype)

    def scores(c, kb):
        q = q_ref[c * ts:(c + 1) * ts, :]
        return jnp.concatenate([_dot_nt(q[:, :dh], kb[:, :dh]), _dot_nt(q[:, dh:], kb[:, dh:])],
                               axis=0)

    def lane_sum(p):
        psum = p[:, 0:LANES]
        for blk in range(1, p.shape[1] // LANES):
            psum = psum + p[:, blk * LANES:(blk + 1) * LANES]
        return psum

    row = lax.broadcasted_iota(jnp.int32, (2 * ts, tk), 0) % ts
    col = lax.broadcasted_iota(jnp.int32, (2 * ts, tk), 1)
    visible = col // CHUNK <= row // CHUNK
    for c in range(n_sub):
        off = pl.multiple_of(i * tq + c * ts, ts)
        s = jnp.where(visible, scores(c, k_ref[pl.ds(off, tk), :]), NEG_INF)
        m0 = jnp.max(s, axis=-1, keepdims=True)
        p = jnp.exp2(s - m0)
        m_ref[c] = jnp.broadcast_to(m0, (2 * ts, LANES))
        l_ref[c] = lane_sum(p)
        acc_ref[c] = _dot(p.astype(BF16), v_ref[pl.ds(off, tk), :])

    def update(c, s, vb):
        m_prev = m_ref[c]
        m_new = jnp.maximum(m_prev, jnp.max(s, axis=-1, keepdims=True))
        alpha = jnp.exp2(m_prev - m_new)
        p = jnp.exp2(s - jnp.tile(m_new, (1, s.shape[1] // LANES)))
        l_ref[c] = alpha * l_ref[c] + lane_sum(p)
        m_ref[c] = m_new
        acc_ref[c] = (acc_ref[c] * jnp.tile(alpha, (1, dv // LANES))
                      + _dot(p.astype(BF16), vb))

    def full_blocks(jb, carry):
        for u in range(unroll):
            off = pl.multiple_of((jb * unroll + u) * tk, tk)
            kb = k_ref[pl.ds(off, tk), :]
            vb = v_ref[pl.ds(off, tk), :]
            for c in range(n_sub):
                update(c, scores(c, kb), vb)
        return carry

    lax.fori_loop(0, i * (tq // tk // unroll), full_blocks, 0)

    for d in range(n_sub - 1):
        off = pl.multiple_of(i * tq + d * tk, tk)
        kb = k_ref[pl.ds(off, tk), :]
        vb = v_ref[pl.ds(off, tk), :]
        for c in range(d + 1, n_sub):
            update(c, scores(c, kb), vb)

    lam = _lambda_full(lq1_ref, lk1_ref, lq2_ref, lk2_ref)
    km = km_ref[...]
    n_meta = km.shape[0]
    zfeat = jnp.zeros((n_meta, dh), km.dtype)
    zpad = jnp.zeros((LANES - n_meta, 2 * dh), km.dtype)
    km_bd = jnp.concatenate([jnp.concatenate([km[:, :dh], zfeat], axis=1), zpad,
                             jnp.concatenate([zfeat, km[:, dh:]], axis=1), zpad], axis=0)
    vm_pad = jnp.concatenate([vm_ref[...], jnp.zeros((LANES - n_meta, dv), km.dtype)], axis=0)
    real_key = lax.broadcasted_iota(jnp.int32, (2 * ts, LANES), 1) < n_meta
    for c in range(n_sub):
        r = _dot_nt(q_ref[c * ts:(c + 1) * ts, :], km_bd)
        s = jnp.where(real_key, jnp.concatenate([r[:, :LANES], r[:, LANES:]], axis=0), NEG_INF)
        m_prev = m_ref[c]
        m_new = jnp.maximum(m_prev, jnp.max(s, axis=-1, keepdims=True))
        alpha = jnp.exp2(m_prev - m_new)
        p = jnp.exp2(s - m_new)
        l = alpha * l_ref[c] + p
        acc = (acc_ref[c] * jnp.tile(alpha, (1, dv // LANES)) + _dot(p.astype(BF16), vm_pad))
        inv_l = 1.0 / jnp.sum(l, axis=-1, keepdims=True)
        o_ref[c * ts:(c + 1) * ts, :] = _diff_finalize(
            acc[:ts], acc[ts:], inv_l[:ts], inv_l[ts:], lam, subg_ref[...],
            gate_ref[c * ts:(c + 1) * ts, :])


def _attention(qkvg, qkvg_meta, lq1, lk1, lq2, lk2, subg, side_weights, *, batch, seq,
               d_model):
    dv = d_model // DA_HEADS
    dh = dv // 2
    tq, tk, n_sub = ATTN_TQ, ATTN_TK, ATTN_SUBTILES
    ts = tq // n_sub
    unroll = ATTN_KEY_UNROLL
    assert ts % CHUNK == 0 and tk % CHUNK == 0 and tq % (tk * unroll) == 0
    nq = seq // tq
    h_ = DA_HEADS
    n_steps = batch * h_ * nq
    conv_steps = n_steps // len(side_weights)
    conv_rows = d_model // conv_steps
    assert len(side_weights) == 4 and conv_steps * 4 == n_steps and conv_rows * conv_steps == d_model
    kern = functools.partial(_attn_kernel, tq=tq, tk=tk, dh=dh, n_sub=n_sub, unroll=unroll,
                             conv_steps=conv_steps)
    vec = lambda n: pl.BlockSpec((1, n), lambda b, h, i: (0, 0))

    def conv_spec(k, col_block):
        def imap(b, h, i):
            step = (b * h_ + h) * nq + i
            return (jnp.clip(step - k * conv_steps, 0, conv_steps - 1), col_block)
        return pl.BlockSpec((conv_rows, d_model), imap)

    side_arrays = [w for w, _ in side_weights]
    out_specs = [pl.BlockSpec((tq, dv), lambda b, h, i: (b * nq + i, h))]
    out_specs += [conv_spec(k, 0) for k in range(len(side_weights))]
    out_shape = [jax.ShapeDtypeStruct((batch * seq, d_model), BF16)]
    out_shape += [jax.ShapeDtypeStruct((d_model, d_model), BF16) for _ in side_weights]
    return pl.pallas_call(
        kern,
        grid=(batch, h_, nq),
        in_specs=[
            pl.BlockSpec((tq, dv), lambda b, h, i: (b * nq + i, h)),
            pl.BlockSpec((seq, dv), lambda b, h, i: (b, h_ + h)),
            pl.BlockSpec((seq, dv), lambda b, h, i: (b, 2 * h_ + h)),
            pl.BlockSpec((tq, dv), lambda b, h, i: (b * nq + i, 3 * h_ + h)),
            pl.BlockSpec((N_META, dv), lambda b, h, i: (0, h_ + h)),
            pl.BlockSpec((N_META, dv), lambda b, h, i: (0, 2 * h_ + h)),
            vec(dh), vec(dh), vec(dh), vec(dh), vec(dv),
        ] + [conv_spec(k, cb) for k, (_, cb) in enumerate(side_weights)],
        out_specs=out_specs,
        out_shape=out_shape,
        scratch_shapes=[pltpu.VMEM((n_sub, 2 * ts, LANES), F32),
                        pltpu.VMEM((n_sub, 2 * ts, LANES), F32),
                        pltpu.VMEM((n_sub, 2 * ts, dv), F32)],
        compiler_params=pltpu.CompilerParams(
            dimension_semantics=("arbitrary", "arbitrary", "arbitrary"),
            vmem_limit_bytes=VMEM_LIMIT_BYTES),
        name="l0_diff_attn",
    )(qkvg, qkvg, qkvg, qkvg, qkvg_meta, qkvg_meta, lq1, lk1, lq2, lk2, subg, *side_arrays)


def _meta_layer0_kernel(q_ref, k_ref, v_ref, gate_ref, lq1_ref, lk1_ref, lq2_ref, lk2_ref,
                        subg_ref, w_ref, x_ref, g_ref, o_ref, y_ref, *, dh):
    h = pl.program_id(0)
    q = q_ref[...]
    k = k_ref[...]
    v = v_ref[...]

    def attend(qm, km):
        s = _dot_nt(qm, km)
        p = jnp.exp2(s - jnp.max(s, axis=-1, keepdims=True))
        return _dot(p.astype(BF16), v), 1.0 / jnp.sum(p, axis=-1, keepdims=True)

    lam = _lambda_full(lq1_ref, lk1_ref, lq2_ref, lk2_ref)
    acc1, inv_l1 = attend(q[:, :dh], k[:, :dh])
    acc2, inv_l2 = attend(q[:, dh:], k[:, dh:])
    a = _diff_finalize(acc1, acc2, inv_l1, inv_l2, lam, subg_ref[...], gate_ref[...])
    part = _dot(a, w_ref[...])

    @pl.when(h == 0)
    def _():
        y_ref[...] = part

    @pl.when(h > 0)
    def _():
        y_ref[...] += part

    @pl.when(h == pl.num_programs(0) - 1)
    def _():
        o_ref[...] = x_ref[...] + _rms(y_ref[...], g_ref[...])


def _meta_layer0(qkvg_meta, lq1, lk1, lq2, lk2, subg, w_out, x, post_g, *, d_model):
    dv = d_model // DA_HEADS
    dh = dv // 2
    h_ = DA_HEADS
    vec = lambda n: pl.BlockSpec((1, n), lambda h: (0, 0))
    blk = lambda part: pl.BlockSpec((N_META, dv), lambda h: (0, part * h_ + h))
    rows = pl.BlockSpec((N_META, d_model), lambda h: (0, 0))
    return pl.pallas_call(
        functools.partial(_meta_layer0_kernel, dh=dh),
        grid=(h_,),
        in_specs=[blk(0), blk(1), blk(2), blk(3), vec(dh), vec(dh), vec(dh), vec(dh), vec(dv),
                  pl.BlockSpec((dv, d_model), lambda h: (h, 0)), rows, vec(d_model)],
        out_specs=rows,
        out_shape=jax.ShapeDtypeStruct((N_META, d_model), F32),
        scratch_shapes=[pltpu.VMEM((N_META, d_model), F32)],
        compiler_params=pltpu.CompilerParams(dimension_semantics=("arbitrary",)),
        name="l0_meta_attn_out",
    )(qkvg_meta, qkvg_meta, qkvg_meta, qkvg_meta, lq1, lk1, lq2, lk2, subg, w_out, x, post_g)


def _tail_kernel(a_ref, x_ref, hm_ref, w_out0_ref, post_g0_ref, pre_g_ref, w_in_u_ref,
                 w_in_g_ref, w_grp_ref, scale_ref, w_out_ref, post_g_ref, o_ref, u_ext_ref,
                 z_ref, *, tm, d, halo):
    i = pl.program_id(1)
    pre_g = pre_g_ref[...]

    @pl.when(i == 0)
    def _():
        hmn = _rms(hm_ref[...], pre_g).astype(BF16)
        u_ext_ref[0:halo, :] = _dot(hmn, w_in_u_ref[...])

    @pl.when(i > 0)
    def _():
        u_ext_ref[0:halo, :] = u_ext_ref[tm:tm + halo, :]

    h = x_ref[...] + _rms(_dot(a_ref[...], w_out0_ref[...]), post_g0_ref[...])
    o_ref[...] = h
    hn = _rms(h, pre_g).astype(BF16)
    u_ext_ref[halo:halo + tm, :] = _dot(hn, w_in_u_ref[...])

    gsz = d // len(POOL_WINDOWS)
    for gi, w in enumerate(POOL_WINDOWS):
        c0 = gi * gsz
        u = u_ext_ref[halo:halo + tm, c0:c0 + gsz]
        s = u
        for back in range(1, w):
            s = s + u_ext_ref[halo - back:halo - back + tm, c0:c0 + gsz]
        mixed = (s * (1.0 / w) - u).astype(BF16)
        mo = _dot(mixed, w_grp_ref[gi]) * scale_ref[:, c0:c0 + gsz]
        gate = _dot(hn, w_in_g_ref[:, c0:c0 + gsz])
        z_ref[:, c0:c0 + gsz] = (mo * _silu(gate)).astype(BF16)

    y = _dot(z_ref[...], w_out_ref[...])
    o_ref[...] = o_ref[...] + _rms(y, post_g_ref[...])


def _tail(a, x, h_meta, w_out0, post_g0, pre_g, w_in_u, w_in_g, w_grp, scale, w_out, post_g, *,
          batch, seq, tm):
    m, d = x.shape
    nt = seq // tm
    halo = N_META
    assert max(POOL_WINDOWS) <= halo and halo <= tm
    const2 = lambda b, i: (0, 0)
    row_tile = lambda b, i: (b * nt + i, 0)
    resident = lambda shape, imap: pl.BlockSpec(shape, imap, pipeline_mode=pl.Buffered(1))
    kern = functools.partial(_tail_kernel, tm=tm, d=d, halo=halo)
    return pl.pallas_call(
        kern,
        grid=(batch, nt),
        in_specs=[
            pl.BlockSpec((tm, d), row_tile),
            pl.BlockSpec((tm, d), row_tile),
            resident((N_META, d), const2),
            resident((d, d), const2),
            pl.BlockSpec((1, d), const2),
            pl.BlockSpec((1, d), const2),
            resident((d, d), const2),
            resident((d, d), const2),
            resident(w_grp.shape, lambda b, i: (0, 0, 0)),
            pl.BlockSpec((1, d), const2),
            resident((d, d), const2),
            pl.BlockSpec((1, d), const2),
        ],
        out_specs=pl.BlockSpec((tm, d), row_tile),
        out_shape=jax.ShapeDtypeStruct((m, d), F32),
        scratch_shapes=[pltpu.VMEM((halo + tm, d), F32), pltpu.VMEM((tm, d), BF16)],
        compiler_params=pltpu.CompilerParams(
            dimension_semantics=("arbitrary", "arbitrary"),
            vmem_limit_bytes=VMEM_LIMIT_BYTES),
        name="l0_out_l1_pool",
    )(a, x, h_meta, w_out0, post_g0, pre_g, w_in_u, w_in_g, w_grp, scale, w_out, post_g)


def _rope_tables(length, rope_dim):
    half = rope_dim // 2
    pos = np.arange(length, dtype=np.float32)
    inv = np.float32(ROPE_THETA) ** (-np.arange(0, rope_dim, 2, dtype=np.float32)
                                     / np.float32(rope_dim))
    ang = pos[:, None] * inv[None, :]
    cos, sin = np.cos(ang), np.sin(ang)
    zeros = lambda n: np.zeros((length, n), np.float32)
    cos_t = np.concatenate([cos, cos, np.ones((length, LANES - rope_dim), np.float32)], axis=1)
    sa_t = np.concatenate([zeros(half), sin, zeros(LANES - rope_dim)], axis=1)
    sb_t = np.concatenate([-sin, zeros(LANES - half)], axis=1)
    return cos_t, sa_t, sb_t


def kernel(x, meta_tokens, pre_norm_g, post_norm_g, attn_w_in, attn_w_out, attn_lambda_q1,
           attn_lambda_k1, attn_lambda_q2, attn_lambda_k2, attn_subln_g, pool_w_in,
           pool_w_group, pool_scale, pool_w_out):
    batch, seq, d = x.shape
    dv = d // DA_HEADS
    dh = dv // 2
    assert dh == LANES and seq % CHUNK == 0
    xf = x.reshape(batch * seq, d)
    meta = meta_tokens.astype(x.dtype)

    cos_t, sa_t, sb_t = _rope_tables(N_META + seq, dh // 4)
    pre_g0, pre_g1 = pre_norm_g[0:1], pre_norm_g[1:2]
    post_g0, post_g1 = post_norm_g[0:1], post_norm_g[1:2]
    q_scale = float(dh) ** -0.5 * math.log2(math.e)
    lam = [p[0:1] for p in (attn_lambda_q1, attn_lambda_k1, attn_lambda_q2, attn_lambda_k2)]
    subg = attn_subln_g[0:1]

    qkvg_meta, w_in0 = _norm_proj(
        meta, pre_g0, attn_w_in[0], cos_t[:N_META], sa_t[:N_META], sb_t[:N_META], tm=N_META,
        tn=PROJ_META_TN, rows_per_seq=N_META, q_scale=q_scale, emit_weight=True)
    qkvg = _norm_proj(xf, pre_g0, w_in0, cos_t[N_META:], sa_t[N_META:], sb_t[N_META:],
                      tm=PROJ_TM, tn=PROJ_TN, rows_per_seq=seq, q_scale=q_scale)
    side_weights = [(attn_w_out[0], 0), (pool_w_in[0], 0), (pool_w_in[0], 1), (pool_w_out[0], 0)]
    a, w_out0, w_in1_u, w_in1_g, w_out1 = _attention(
        qkvg, qkvg_meta, *lam, subg, side_weights, batch=batch, seq=seq, d_model=d)
    h1_meta = _meta_layer0(qkvg_meta, *lam, subg, w_out0, meta, post_g0, d_model=d)

    out = _tail(a, xf, h1_meta, w_out0, post_g0, pre_g1, w_in1_u, w_in1_g,
                pool_w_group[0].astype(BF16), pool_scale[0:1], w_out1, post_g1,
                batch=batch, seq=seq, tm=POOL_TM)
    return out.reshape(batch, seq, d)
```

```python
import functools
import math

import jax
import jax.numpy as jnp
import numpy as np
from jax import lax
from jax.experimental import pallas as pl
from jax.experimental.pallas import tpu as pltpu

F32 = jnp.float32
BF16 = jnp.bfloat16

N_META = 16
CHUNK = 64
RMS_EPS = 1e-6
DA_HEADS = 8
ROPE_THETA = 500000.0
POOL_WINDOWS = (2, 4, 8, 16)
LAMBDA_INIT_L0 = 0.8 - 0.6 * math.exp(-0.3 * 0)

LANES = 128
VMEM_LIMIT_BYTES = 56 * 1024 * 1024

PROJ_TM = 1024
PROJ_TN = 2048
PROJ_META_TN = 1024
PROJ_SUB_ROWS = 256
PROJ_SUB_COLS = 256
ATTN_TQ = 2048
ATTN_TK = 256
ATTN_SUBTILES = 8
ATTN_KEY_UNROLL = 8
POOL_TM = 256

NEG_INF = float(jnp.finfo(jnp.float32).min)


def _rms(x, g):
    ms = jnp.mean(x * x, axis=-1, keepdims=True)
    return x * lax.rsqrt(ms + RMS_EPS) * g


def _silu(g):
    return g * (1.0 / (1.0 + jnp.exp(-g)))


def _dot(a, b):
    return jnp.dot(a, b, preferred_element_type=F32)


def _dot_nt(a, b):
    return lax.dot_general(a, b, (((1,), (1,)), ((), ())), preferred_element_type=F32)


def _norm_proj_kernel(x_ref, g_ref, w_ref, cos_ref, sa_ref, sb_ref, o_ref, *rest,
                      n_col_tiles_per_part, q_scale, sub_rows, sub_cols, emit_weight):
    wb_ref, hn_ref = rest if emit_weight else (None,) + rest
    j = pl.program_id(1)
    npt = n_col_tiles_per_part

    tm, tn = o_ref.shape
    mc, nc = min(tm, sub_rows), min(tn, sub_cols)
    is_rope = j < 2 * npt
    is_gate = j >= 3 * npt
    scale = jnp.where(j < npt, q_scale, 1.0).astype(F32)
    half = 16

    def tile(first):
        for mi in range(tm // mc):
            rows = slice(mi * mc, (mi + 1) * mc)
            if first:
                hn = _rms(x_ref[rows, :], g_ref[...]).astype(BF16)
                hn_ref[rows, :] = hn
            else:
                hn = hn_ref[rows, :]
            c = jnp.where(is_rope, cos_ref[rows, :] * scale, 1.0)
            sa = jnp.where(is_rope, sa_ref[rows, :] * scale, 0.0)
            sb = jnp.where(is_rope, sb_ref[rows, :] * scale, 0.0)
            for ni in range(tn // nc):
                w_sub = w_ref[:, ni * nc:(ni + 1) * nc]
                if emit_weight:
                    w_sub = w_sub.astype(BF16)
                    if mi == 0:
                        wb_ref[:, ni * nc:(ni + 1) * nc] = w_sub
                acc = _dot(hn, w_sub)
                for blk in range(nc // LANES):
                    t = acc[:, blk * LANES:(blk + 1) * LANES]
                    r = (t * c + pltpu.roll(t, half, 1) * sa
                         + pltpu.roll(t, LANES - half, 1) * sb)
                    f = jnp.where(is_gate, 1.0 / (1.0 + jnp.exp(-t)), 1.0)
                    col = ni * nc + blk * LANES
                    o_ref[rows, col:col + LANES] = (r * f).astype(o_ref.dtype)

    pl.when(j == 0)(functools.partial(tile, True))
    pl.when(j > 0)(functools.partial(tile, False))


def _norm_proj(x, g, w, cos_t, sa_t, sb_t, *, tm, tn, rows_per_seq, q_scale,
               emit_weight=False):
    m, d = x.shape
    n = w.shape[1]
    tiles_per_seq = rows_per_seq // tm
    assert not emit_weight or m == tm
    kern = functools.partial(_norm_proj_kernel, n_col_tiles_per_part=(n // 4) // tn,
                             q_scale=q_scale, sub_rows=PROJ_SUB_ROWS, sub_cols=PROJ_SUB_COLS,
                             emit_weight=emit_weight)
    tab_spec = pl.BlockSpec((tm, LANES), lambda i, j: (i % tiles_per_seq, 0))
    out_specs = pl.BlockSpec((tm, tn), lambda i, j: (i, j))
    out_shape = jax.ShapeDtypeStruct((m, n), BF16)
    if emit_weight:
        out_specs = [out_specs, pl.BlockSpec((d, tn), lambda i, j: (0, j))]
        out_shape = [out_shape, jax.ShapeDtypeStruct((d, n), BF16)]
    return pl.pallas_call(
        kern,
        grid=(m // tm, n // tn),
        in_specs=[
            pl.BlockSpec((tm, d), lambda i, j: (i, 0)),
            pl.BlockSpec((1, d), lambda i, j: (0, 0)),
            pl.BlockSpec((d, tn), lambda i, j: (0, j)),
            tab_spec, tab_spec, tab_spec,
        ],
        out_specs=out_specs,
        out_shape=out_shape,
        scratch_shapes=[pltpu.VMEM((tm, d), BF16)],
        compiler_params=pltpu.CompilerParams(
            dimension_semantics=("parallel", "arbitrary"),
            vmem_limit_bytes=VMEM_LIMIT_BYTES),
        name="l0_norm_proj",
    )(x, g, w, cos_t, sa_t, sb_t)


def _lambda_full(lq1_ref, lk1_ref, lq2_ref, lk2_ref):
    a = jnp.sum(lq1_ref[...] * lk1_ref[...], axis=-1, keepdims=True)
    b = jnp.sum(lq2_ref[...] * lk2_ref[...], axis=-1, keepdims=True)
    return jnp.exp(a) - jnp.exp(b) + LAMBDA_INIT_L0


def _diff_finalize(acc1, acc2, inv_l1, inv_l2, lam, subg, gate):
    o = acc1 * inv_l1 - acc2 * (lam * inv_l2)
    o = _rms(o, subg * (1.0 - LAMBDA_INIT_L0))
    return (o * gate.astype(F32)).astype(BF16)


def _attn_kernel(q_ref, k_ref, v_ref, gate_ref, km_ref, vm_ref, lq1_ref, lk1_ref, lq2_ref,
                 lk2_ref, subg_ref, w0_ref, w1_ref, w2_ref, w3_ref, o_ref, wb0_ref, wb1_ref,
                 wb2_ref, wb3_ref, m_ref, l_ref, acc_ref, *, tq, tk, dh, n_sub, unroll,
                 conv_steps):
    i = pl.program_id(2)
    ts = tq // n_sub
    dv = acc_ref.shape[2]

    step = (pl.program_id(0) * pl.num_programs(1) + pl.program_id(1)) * pl.num_programs(2) + i
    for k, (w_ref, wb_ref) in enumerate(((w0_ref, wb0_ref), (w1_ref, wb1_ref),
                                         (w2_ref, wb2_ref), (w3_ref, wb3_ref))):
        @pl.when(jnp.logical_and(step >= k * conv_steps, step < (k + 1) * conv_steps))
        def _(w_ref=w_ref, wb_ref=wb_ref):
            wb_ref[...] = w_ref[...].astype(wb_ref.dtype)

    def scores(c, kb):
        q = q_ref[c * ts:(c + 1) * ts, :]
        return jnp.concatenate([_dot_nt(q[:, :dh], kb[:, :dh]), _dot_nt(q[:, dh:], kb[:, dh:])],
                               axis=0)

    def lane_sum(p):
        psum = p[:, 0:LANES]
        for blk in range(1, p.shape[1] // LANES):
            psum = psum + p[:, blk * LANES:(blk + 1) * LANES]
        return psum

    row = lax.broadcasted_iota(jnp.int32, (2 * ts, tk), 0) % ts
    col = lax.broadcasted_iota(jnp.int32, (2 * ts, tk), 1)
    visible = col // CHUNK <= row // CHUNK
    for c in range(n_sub):
        off = pl.multiple_of(i * tq + c * ts, ts)
        s = jnp.where(visible, scores(c, k_ref[pl.ds(off, tk), :]), NEG_INF)
        m0 = jnp.max(s, axis=-1, keepdims=True)
        p = jnp.exp2(s - m0)
        m_ref[c] = jnp.broadcast_to(m0, (2 * ts, LANES))
        l_ref[c] = lane_sum(p)
        acc_ref[c] = _dot(p.astype(BF16), v_ref[pl.ds(off, tk), :])

    def update(c, s, vb):
        m_prev = m_ref[c]
        m_new = jnp.maximum(m_prev, jnp.max(s, axis=-1, keepdims=True))
        alpha = jnp.exp2(m_prev - m_new)
        p = jnp.exp2(s - jnp.tile(m_new, (1, s.shape[1] // LANES)))
        l_ref[c] = alpha * l_ref[c] + lane_sum(p)
        m_ref[c] = m_new
        acc_ref[c] = (acc_ref[c] * jnp.tile(alpha, (1, dv // LANES))
                      + _dot(p.astype(BF16), vb))

    def full_blocks(jb, carry):
        for u in range(unroll):
            off = pl.multiple_of((jb * unroll + u) * tk, tk)
            kb = k_ref[pl.ds(off, tk), :]
            vb = v_ref[pl.ds(off, tk), :]
            for c in range(n_sub):
                update(c, scores(c, kb), vb)
        return carry

    lax.fori_loop(0, i * (tq // tk // unroll), full_blocks, 0)

    for d in range(n_sub - 1):
        off = pl.multiple_of(i * tq + d * tk, tk)
        kb = k_ref[pl.ds(off, tk), :]
        vb = v_ref[pl.ds(off, tk), :]
        for c in range(d + 1, n_sub):
            update(c, scores(c, kb), vb)

    lam = _lambda_full(lq1_ref, lk1_ref, lq2_ref, lk2_ref)
    km = km_ref[...]
    n_meta = km.shape[0]
    zfeat = jnp.zeros((n_meta, dh), km.dtype)
    zpad = jnp.zeros((LANES - n_meta, 2 * dh), km.dtype)
    km_bd = jnp.concatenate([jnp.concatenate([km[:, :dh], zfeat], axis=1), zpad,
                             jnp.concatenate([zfeat, km[:, dh:]], axis=1), zpad], axis=0)
    vm_pad = jnp.concatenate([vm_ref[...], jnp.zeros((LANES - n_meta, dv), km.dtype)], axis=0)
    real_key = lax.broadcasted_iota(jnp.int32, (2 * ts, LANES), 1) < n_meta
    for c in range(n_sub):
        r = _dot_nt(q_ref[c * ts:(c + 1) * ts, :], km_bd)
        s = jnp.where(real_key, jnp.concatenate([r[:, :LANES], r[:, LANES:]], axis=0), NEG_INF)
        m_prev = m_ref[c]
        m_new = jnp.maximum(m_prev, jnp.max(s, axis=-1, keepdims=True))
        alpha = jnp.exp2(m_prev - m_new)
        p = jnp.exp2(s - m_new)
        l = alpha * l_ref[c] + p
        acc = (acc_ref[c] * jnp.tile(alpha, (1, dv // LANES)) + _dot(p.astype(BF16), vm_pad))
        inv_l = 1.0 / jnp.sum(l, axis=-1, keepdims=True)
        o_ref[c * ts:(c + 1) * ts, :] = _diff_finalize(
            acc[:ts], acc[ts:], inv_l[:ts], inv_l[ts:], lam, subg_ref[...],
            gate_ref[c * ts:(c + 1) * ts, :])


def _attention(qkvg, qkvg_meta, lq1, lk1, lq2, lk2, subg, side_weights, *, batch, seq,
               d_model):
    dv = d_model // DA_HEADS
    dh = dv // 2
    tq, tk, n_sub = ATTN_TQ, ATTN_TK, ATTN_SUBTILES
    ts = tq // n_sub
    unroll = ATTN_KEY_UNROLL
    assert ts % CHUNK == 0 and tk % CHUNK == 0 and tq % (tk * unroll) == 0
    nq = seq // tq
    h_ = DA_HEADS
    n_steps = batch * h_ * nq
    conv_steps = n_steps // len(side_weights)
    conv_rows = d_model // conv_steps
    assert len(side_weights) == 4 and conv_steps * 4 == n_steps and conv_rows * conv_steps == d_model
    kern = functools.partial(_attn_kernel, tq=tq, tk=tk, dh=dh, n_sub=n_sub, unroll=unroll,
                             conv_steps=conv_steps)
    vec = lambda n: pl.BlockSpec((1, n), lambda b, h, i: (0, 0))

    def conv_spec(k, col_block):
        def imap(b, h, i):
            step = (b * h_ + h) * nq + i
            return (jnp.clip(step - k * conv_steps, 0, conv_steps - 1), col_block)
        return pl.BlockSpec((conv_rows, d_model), imap)

    side_arrays = [w for w, _ in side_weights]
    out_specs = [pl.BlockSpec((tq, dv), lambda b, h, i: (b * nq + i, h))]
    out_specs += [conv_spec(k, 0) for k in range(len(side_weights))]
    out_shape = [jax.ShapeDtypeStruct((batch * seq, d_model), BF16)]
    out_shape += [jax.ShapeDtypeStruct((d_model, d_model), BF16) for _ in side_weights]
    return pl.pallas_call(
        kern,
        grid=(batch, h_, nq),
        in_specs=[
            pl.BlockSpec((tq, dv), lambda b, h, i: (b * nq + i, h)),
            pl.BlockSpec((seq, dv), lambda b, h, i: (b, h_ + h)),
            pl.BlockSpec((seq, dv), lambda b, h, i: (b, 2 * h_ + h)),
            pl.BlockSpec((tq, dv), lambda b, h, i: (b * nq + i, 3 * h_ + h)),
            pl.BlockSpec((N_META, dv), lambda b, h, i: (0, h_ + h)),
            pl.BlockSpec((N_META, dv), lambda b, h, i: (0, 2 * h_ + h)),
            vec(dh), vec(dh), vec(dh), vec(dh), vec(dv),
        ] + [conv_spec(k, cb) for k, (_, cb) in enumerate(side_weights)],
        out_specs=out_specs,
        out_shape=out_shape,
        scratch_shapes=[pltpu.VMEM((n_sub, 2 * ts, LANES), F32),
                        pltpu.VMEM((n_sub, 2 * ts, LANES), F32),
                        pltpu.VMEM((n_sub, 2 * ts, dv), F32)],
        compiler_params=pltpu.CompilerParams(
            dimension_semantics=("arbitrary", "arbitrary", "arbitrary"),
            vmem_limit_bytes=VMEM_LIMIT_BYTES),
        name="l0_diff_attn",
    )(qkvg, qkvg, qkvg, qkvg, qkvg_meta, qkvg_meta, lq1, lk1, lq2, lk2, subg, *side_arrays)


def _meta_layer0_kernel(q_ref, k_ref, v_ref, gate_ref, lq1_ref, lk1_ref, lq2_ref, lk2_ref,
                        subg_ref, w_ref, x_ref, g_ref, o_ref, y_ref, *, dh):
    h = pl.program_id(0)
    q = q_ref[...]
    k = k_ref[...]
    v = v_ref[...]

    def attend(qm, km):
        s = _dot_nt(qm, km)
        p = jnp.exp2(s - jnp.max(s, axis=-1, keepdims=True))
        return _dot(p.astype(BF16), v), 1.0 / jnp.sum(p, axis=-1, keepdims=True)

    lam = _lambda_full(lq1_ref, lk1_ref, lq2_ref, lk2_ref)
    acc1, inv_l1 = attend(q[:, :dh], k[:, :dh])
    acc2, inv_l2 = attend(q[:, dh:], k[:, dh:])
    a = _diff_finalize(acc1, acc2, inv_l1, inv_l2, lam, subg_ref[...], gate_ref[...])
    part = _dot(a, w_ref[...])

    @pl.when(h == 0)
    def _():
        y_ref[...] = part

    @pl.when(h > 0)
    def _():
        y_ref[...] += part

    @pl.when(h == pl.num_programs(0) - 1)
    def _():
        o_ref[...] = x_ref[...] + _rms(y_ref[...], g_ref[...])


def _meta_layer0(qkvg_meta, lq1, lk1, lq2, lk2, subg, w_out, x, post_g, *, d_model):
    dv = d_model // DA_HEADS
    dh = dv // 2
    h_ = DA_HEADS
    vec = lambda n: pl.BlockSpec((1, n), lambda h: (0, 0))
    blk = lambda part: pl.BlockSpec((N_META, dv), lambda h: (0, part * h_ + h))
    rows = pl.BlockSpec((N_META, d_model), lambda h: (0, 0))
    return pl.pallas_call(
        functools.partial(_meta_layer0_kernel, dh=dh),
        grid=(h_,),
        in_specs=[blk(0), blk(1), blk(2), blk(3), vec(dh), vec(dh), vec(dh), vec(dh), vec(dv),
                  pl.BlockSpec((dv, d_model), lambda h: (h, 0)), rows, vec(d_model)],
        out_specs=rows,
        out_shape=jax.ShapeDtypeStruct((N_META, d_model), F32),
        scratch_shapes=[pltpu.VMEM((N_META, d_model), F32)],
        compiler_params=pltpu.CompilerParams(dimension_semantics=("arbitrary",)),
        name="l0_meta_attn_out",
    )(qkvg_meta, qkvg_meta, qkvg_meta, qkvg_meta, lq1, lk1, lq2, lk2, subg, w_out, x, post_g)


def _tail_kernel(a_ref, x_ref, hm_ref, w_out0_ref, post_g0_ref, pre_g_ref, w_in_u_ref,
                 w_in_g_ref, w_grp_ref, scale_ref, w_out_ref, post_g_ref, o_ref, u_ext_ref,
                 z_ref, *, tm, d, halo):
    i = pl.program_id(1)
    pre_g = pre_g_ref[...]

    @pl.when(i == 0)
    def _():
        hmn = _rms(hm_ref[...], pre_g).astype(BF16)
        u_ext_ref[0:halo, :] = _dot(hmn, w_in_u_ref[...])

    @pl.when(i > 0)
    def _():
        u_ext_ref[0:halo, :] = u_ext_ref[tm:tm + halo, :]

    h = x_ref[...] + _rms(_dot(a_ref[...], w_out0_ref[...]), post_g0_ref[...])
    o_ref[...] = h
    hn = _rms(h, pre_g).astype(BF16)
    u_ext_ref[halo:halo + tm, :] = _dot(hn, w_in_u_ref[...])

    gsz = d // len(POOL_WINDOWS)
    for gi, w in enumerate(POOL_WINDOWS):
        c0 = gi * gsz
        u = u_ext_ref[halo:halo + tm, c0:c0 + gsz]
        s = u
        for back in range(1, w):
            s = s + u_ext_ref[halo - back:halo - back + tm, c0:c0 + gsz]
        mixed = (s * (1.0 / w) - u).astype(BF16)
        mo = _dot(mixed, w_grp_ref[gi]) * scale_ref[:, c0:c0 + gsz]
        gate = _dot(hn, w_in_g_ref[:, c0:c0 + gsz])
        z_ref[:, c0:c0 + gsz] = (mo * _silu(gate)).astype(BF16)

    y = _dot(z_ref[...], w_out_ref[...])
    o_ref[...] = o_ref[...] + _rms(y, post_g_ref[...])


def _tail(a, x, h_meta, w_out0, post_g0, pre_g, w_in_u, w_in_g, w_grp, scale, w_out, post_g, *,
          batch, seq, tm):
    m, d = x.shape
    nt = seq // tm
    halo = N_META
    assert max(POOL_WINDOWS) <= halo and halo <= tm
    const2 = lambda b, i: (0, 0)
    row_tile = lambda b, i: (b * nt + i, 0)
    resident = lambda shape, imap: pl.BlockSpec(shape, imap, pipeline_mode=pl.Buffered(1))
    kern = functools.partial(_tail_kernel, tm=tm, d=d, halo=halo)
    return pl.pallas_call(
        kern,
        grid=(batch, nt),
        in_specs=[
            pl.BlockSpec((tm, d), row_tile),
            pl.BlockSpec((tm, d), row_tile),
            resident((N_META, d), const2),
            resident((d, d), const2),
            pl.BlockSpec((1, d), const2),
            pl.BlockSpec((1, d), const2),
            resident((d, d), const2),
            resident((d, d), const2),
            resident(w_grp.shape, lambda b, i: (0, 0, 0)),
            pl.BlockSpec((1, d), const2),
            resident((d, d), const2),
            pl.BlockSpec((1, d), const2),
        ],
        out_specs=pl.BlockSpec((tm, d), row_tile),
        out_shape=jax.ShapeDtypeStruct((m, d), F32),
        scratch_shapes=[pltpu.VMEM((halo + tm, d), F32), pltpu.VMEM((tm, d), BF16)],
        compiler_params=pltpu.CompilerParams(
            dimension_semantics=("arbitrary", "arbitrary"),
            vmem_limit_bytes=VMEM_LIMIT_BYTES),
        name="l0_out_l1_pool",
    )(a, x, h_meta, w_out0, post_g0, pre_g, w_in_u, w_in_g, w_grp, scale, w_out, post_g)


def _rope_tables(length, rope_dim):
    half = rope_dim // 2
    pos = np.arange(length, dtype=np.float32)
    inv = np.float32(ROPE_THETA) ** (-np.arange(0, rope_dim, 2, dtype=np.float32)
                                     / np.float32(rope_dim))
    ang = pos[:, None] * inv[None, :]
    cos, sin = np.cos(ang), np.sin(ang)
    zeros = lambda n: np.zeros((length, n), np.float32)
    cos_t = np.concatenate([cos, cos, np.ones((length, LANES - rope_dim), np.float32)], axis=1)
    sa_t = np.concatenate([zeros(half), sin, zeros(LANES - rope_dim)], axis=1)
    sb_t = np.concatenate([-sin, zeros(LANES - half)], axis=1)
    return cos_t, sa_t, sb_t


def kernel(x, meta_tokens, pre_norm_g, post_norm_g, attn_w_in, attn_w_out, attn_lambda_q1,
           attn_lambda_k1, attn_lambda_q2, attn_lambda_k2, attn_subln_g, pool_w_in,
           pool_w_group, pool_scale, pool_w_out):
    batch, seq, d = x.shape
    dv = d // DA_HEADS
    dh = dv // 2
    assert dh == LANES and seq % CHUNK == 0
    xf = x.reshape(batch * seq, d)
    meta = meta_tokens.astype(x.dtype)

    cos_t, sa_t, sb_t = _rope_tables(N_META + seq, dh // 4)
    pre_g0, pre_g1 = pre_norm_g[0:1], pre_norm_g[1:2]
    post_g0, post_g1 = post_norm_g[0:1], post_norm_g[1:2]
    q_scale = float(dh) ** -0.5 * math.log2(math.e)
    lam = [p[0:1] for p in (attn_lambda_q1, attn_lambda_k1, attn_lambda_q2, attn_lambda_k2)]
    subg = attn_subln_g[0:1]

    qkvg_meta, w_in0 = _norm_proj(
        meta, pre_g0, attn_w_in[0], cos_t[:N_META], sa_t[:N_META], sb_t[:N_META], tm=N_META,
        tn=PROJ_META_TN, rows_per_seq=N_META, q_scale=q_scale, emit_weight=True)
    qkvg = _norm_proj(xf, pre_g0, w_in0, cos_t[N_META:], sa_t[N_META:], sb_t[N_META:],
                      tm=PROJ_TM, tn=PROJ_TN, rows_per_seq=seq, q_scale=q_scale)
    side_weights = [(attn_w_out[0], 0), (pool_w_in[0], 0), (pool_w_in[0], 1), (pool_w_out[0], 0)]
    a, w_out0, w_in1_u, w_in1_g, w_out1 = _attention(
        qkvg, qkvg_meta, *lam, subg, side_weights, batch=batch, seq=seq, d_model=d)
    h1_meta = _meta_layer0(qkvg_meta, *lam, subg, w_out0, meta, post_g0, d_model=d)

    out = _tail(a, xf, h1_meta, w_out0, post_g0, pre_g1, w_in1_u, w_in1_g,
                pool_w_group[0].astype(BF16), pool_scale[0:1], w_out1, post_g1,
                batch=batch, seq=seq, tm=POOL_TM)
    return out.reshape(batch, seq, d)
```
